```python
import jax, jax.numpy as jnp
from jax import lax
import numpy as np

D_MODEL = 2048
BATCH = 4
SEQ = 4096
DEPTH = 1

D_FF = 5632
CONV_WIDTH = 1024
CONV_K = 3
N_HEADS = 16
N_KV = 4
GROUP = N_HEADS // N_KV
HEAD_DIM = 64
ATTN_WIDTH = N_HEADS * HEAD_DIM
MIX_WIDTH = CONV_WIDTH + ATTN_WIDTH
KV_WIDTH = N_KV * HEAD_DIM
CMP_BLOCK = 32
CMP_STRIDE = 16
CMP_HIDDEN = 256
SLC_BLOCK = 64
SLC_TOPK = 16
N_LOCAL = 2
WINDOW = 512
Q_BLOCK = 128
N_GATES = 3
ROPE_THETA = 10000.0
EPS = 1e-6
NEG = -1e30
FORCE = 1e9
IN_WIDTH = 3 * CONV_WIDTH + ATTN_WIDTH + 6 * KV_WIDTH + N_GATES * N_HEADS

kernel_name = "hybrid_conv_nsa_macaron"


def rmsnorm(x, g):
    xf = x.astype(jnp.float32)
    y = xf * lax.rsqrt(jnp.mean(xf * xf, axis=-1, keepdims=True) + EPS)
    return (y * g.astype(jnp.float32)).astype(x.dtype)


def swiglu(h, wg, wu, wd):
    return (jax.nn.silu(h @ wg) * (h @ wu)) @ wd


def rope_tables(s):
    pos = jnp.arange(s, dtype=jnp.float32)
    inv = ROPE_THETA ** (-jnp.arange(0, HEAD_DIM, 2, dtype=jnp.float32) / HEAD_DIM)
    ang = pos[:, None] * inv[None, :]
    return jnp.cos(ang), jnp.sin(ang)


def apply_rope(x, cos, sin):
    x1, x2 = jnp.split(x, 2, axis=-1)
    c = cos[None, :, None, :]
    s = sin[None, :, None, :]
    return jnp.concatenate([x1 * c - x2 * s, x2 * c + x1 * s], axis=-1).astype(x.dtype)


def masked_softmax(s, mask):
    s = jnp.where(mask, s.astype(jnp.float32), NEG)
    return jax.nn.softmax(s, axis=-1) * mask


def short_conv_mixer(h, b_gate, c_gate, conv_w):
    u = c_gate * h
    y = lax.conv_general_dilated(
        u, conv_w[:, None, :].astype(u.dtype), window_strides=(1,),
        padding=[(CONV_K - 1, 0)], dimension_numbers=("NWC", "WIO", "NWC"),
        feature_group_count=CONV_WIDTH)
    return b_gate * y


def compress(tok, w1, w2, pe):
    s = tok.shape[1]
    n_cmp = (s - CMP_BLOCK) // CMP_STRIDE + 1
    idx = jnp.arange(n_cmp)[:, None] * CMP_STRIDE + jnp.arange(CMP_BLOCK)[None, :]
    blocks = tok[:, idx] + pe[None, None, :, None, :]
    hid = jax.nn.gelu(jnp.einsum('bnlhd,ldf->bnhf', blocks, w1))
    return jnp.einsum('bnhf,fd->bnhd', hid, w2)


def nsa_attention(q, kc, vc, ks, vs, kw, vw, gates):
    bsz, s = q.shape[0], q.shape[1]
    n_cmp = kc.shape[1]
    n_slc = s // SLC_BLOCK
    top_k = min(SLC_TOPK, n_slc)
    n_qb = s // Q_BLOCK
    q = (q * (HEAD_DIM ** -0.5)).reshape(bsz, s, N_KV, GROUP, HEAD_DIM)
    gates = gates.reshape(bsz, s, N_KV, GROUP, N_GATES)

    cmp_start = jnp.arange(n_cmp) * CMP_STRIDE
    cmp_end = cmp_start + CMP_BLOCK - 1
    slc_start = jnp.arange(n_slc) * SLC_BLOCK
    overlap = jnp.clip(jnp.minimum(cmp_start[:, None] + CMP_BLOCK, slc_start[None, :] + SLC_BLOCK)
                       - jnp.maximum(cmp_start[:, None], slc_start[None, :]), 0, None)
    overlap = overlap.astype(jnp.float32) / CMP_BLOCK

    ks_blk = ks.reshape(bsz, n_slc, SLC_BLOCK, N_KV, HEAD_DIM).transpose(0, 3, 1, 2, 4)
    vs_blk = vs.reshape(bsz, n_slc, SLC_BLOCK, N_KV, HEAD_DIM).transpose(0, 3, 1, 2, 4)
    kw_pad = jnp.pad(kw, ((0, 0), (WINDOW, 0), (0, 0), (0, 0)))
    vw_pad = jnp.pad(vw, ((0, 0), (WINDOW, 0), (0, 0), (0, 0)))
    bi = jnp.arange(bsz)[:, None, None, None]
    hi = jnp.arange(N_KV)[None, None, :, None]
    jblk = jnp.arange(n_slc)

    def block(c):
        s0 = c * Q_BLOCK
        qc = lax.dynamic_slice_in_dim(q, s0, Q_BLOCK, axis=1)
        gc = lax.dynamic_slice_in_dim(gates, s0, Q_BLOCK, axis=1)
        t = s0 + jnp.arange(Q_BLOCK)

        sc = jnp.einsum('bqhgd,bnhd->bqhgn', qc, kc)
        mc = (cmp_end[None, :] <= t[:, None])[None, :, None, None, :]
        pc = masked_softmax(sc, mc)
        o_cmp = jnp.einsum('bqhgn,bnhd->bqhgd', pc.astype(vc.dtype), vc)

        imp = jnp.einsum('bqhgn,nj->bqhj', pc, overlap)
        tb = t // SLC_BLOCK
        forced = (jblk[None, :] == 0) | ((jblk[None, :] <= tb[:, None])
                                         & (jblk[None, :] > tb[:, None] - N_LOCAL))
        valid = slc_start[None, :] <= t[:, None]
        imp = jnp.where(forced[None, :, None, :], FORCE, imp)
        imp = jnp.where(valid[None, :, None, :], imp, NEG)
        top_val, top_idx = lax.top_k(imp, top_k)
        sel_ok = top_val > 0.5 * NEG
        k_sel = ks_blk[bi, hi, top_idx]
        v_sel = vs_blk[bi, hi, top_idx]
        k_sel = k_sel.reshape(bsz, Q_BLOCK, N_KV, top_k * SLC_BLOCK, HEAD_DIM)
        v_sel = v_sel.reshape(bsz, Q_BLOCK, N_KV, top_k * SLC_BLOCK, HEAD_DIM)
        pos = top_idx[..., None] * SLC_BLOCK + jnp.arange(SLC_BLOCK)
        ms = sel_ok[..., None] & (pos <= t[None, :, None, None, None])
        ms = ms.reshape(bsz, Q_BLOCK, N_KV, 1, top_k * SLC_BLOCK)
        ss = jnp.einsum('bqhgd,bqhmd->bqhgm', qc, k_sel)
        ps = masked_softmax(ss, ms)
        o_slc = jnp.einsum('bqhgm,bqhmd->bqhgd', ps.astype(v_sel.dtype), v_sel)

        kwc = lax.dynamic_slice_in_dim(kw_pad, s0, Q_BLOCK + WINDOW, axis=1)
        vwc = lax.dynamic_slice_in_dim(vw_pad, s0, Q_BLOCK + WINDOW, axis=1)
        p = s0 - WINDOW + jnp.arange(Q_BLOCK + WINDOW)
        dist = t[:, None] - p[None, :]
        mw = ((dist >= 0) & (dist < WINDOW) & (p >= 0)[None, :])[None, :, None, None, :]
        sw = jnp.einsum('bqhgd,bkhd->bqhgk', qc, kwc)
        pw = masked_softmax(sw, mw)
        o_win = jnp.einsum('bqhgk,bkhd->bqhgd', pw.astype(vwc.dtype), vwc)

        o = gc[..., 0:1] * o_cmp + gc[..., 1:2] * o_slc + gc[..., 2:3] * o_win
        return o.reshape(bsz, Q_BLOCK, ATTN_WIDTH)

    out = lax.map(block, jnp.arange(n_qb))
    return out.transpose(1, 0, 2, 3).reshape(bsz, s, ATTN_WIDTH)


def hybrid_mixer(h, w_in, conv_w, cmp_k_w1, cmp_k_w2, cmp_k_pe, cmp_v_w1, cmp_v_w2,
                 cmp_v_pe, conv_out_norm, attn_out_norm, w_out, cos, sin):
    bsz, s, _ = h.shape
    proj = h @ w_in
    sizes = [CONV_WIDTH, CONV_WIDTH, CONV_WIDTH, ATTN_WIDTH] + [KV_WIDTH] * 6
    cuts = list(np.cumsum(sizes))
    (c_h, c_b, c_c, q, k_c, v_c, k_s, v_s, k_w, v_w, g) = jnp.split(proj, cuts, axis=-1)

    y_conv = short_conv_mixer(c_h, c_b, c_c, conv_w)

    hd = lambda a, n: a.reshape(bsz, s, n, HEAD_DIM)
    q = apply_rope(hd(q, N_HEADS), cos, sin)
    k_c = apply_rope(hd(k_c, N_KV), cos, sin)
    k_s = apply_rope(hd(k_s, N_KV), cos, sin)
    k_w = apply_rope(hd(k_w, N_KV), cos, sin)
    kc = compress(k_c, cmp_k_w1, cmp_k_w2, cmp_k_pe)
    vc = compress(hd(v_c, N_KV), cmp_v_w1, cmp_v_w2, cmp_v_pe)
    gates = jax.nn.sigmoid(g).reshape(bsz, s, N_HEADS, N_GATES)
    y_attn = nsa_attention(q, kc, vc, k_s, hd(v_s, N_KV), k_w, hd(v_w, N_KV), gates)

    y = jnp.concatenate([rmsnorm(y_conv, conv_out_norm), rmsnorm(y_attn, attn_out_norm)], axis=-1)
    return y @ w_out


def setup_inputs(seed: int = 0) -> dict:
    key = jax.random.key(seed)
    ks = jax.random.split(key, 24)
    f32 = jnp.float32
    nrm = lambda k, shape, scale: (jax.random.normal(k, shape, f32) * scale)
    gain = lambda k, n: 1.0 + 0.02 * jax.random.normal(k, (DEPTH, n), f32)
    L = DEPTH
    return {
        "x": jax.random.normal(ks[0], (BATCH, SEQ, D_MODEL), f32),
        "ffn1_norm": gain(ks[1], D_MODEL),
        "ffn1_wg": nrm(ks[2], (L, D_MODEL, D_FF), D_MODEL ** -0.5),
        "ffn1_wu": nrm(ks[3], (L, D_MODEL, D_FF), D_MODEL ** -0.5),
        "ffn1_wd": nrm(ks[4], (L, D_FF, D_MODEL), D_FF ** -0.5),
        "mix_norm": gain(ks[5], D_MODEL),
        "w_in": nrm(ks[6], (L, D_MODEL, IN_WIDTH), D_MODEL ** -0.5),
        "conv_w": nrm(ks[7], (L, CONV_K, CONV_WIDTH), CONV_K ** -0.5),
        "cmp_k_w1": nrm(ks[8], (L, CMP_BLOCK, HEAD_DIM, CMP_HIDDEN), (CMP_BLOCK * HEAD_DIM) ** -0.5),
        "cmp_k_w2": nrm(ks[9], (L, CMP_HIDDEN, HEAD_DIM), CMP_HIDDEN ** -0.5),
        "cmp_k_pe": nrm(ks[10], (L, CMP_BLOCK, HEAD_DIM), 0.1),
        "cmp_v_w1": nrm(ks[11], (L, CMP_BLOCK, HEAD_DIM, CMP_HIDDEN), (CMP_BLOCK * HEAD_DIM) ** -0.5),
        "cmp_v_w2": nrm(ks[12], (L, CMP_HIDDEN, HEAD_DIM), CMP_HIDDEN ** -0.5),
        "cmp_v_pe": nrm(ks[13], (L, CMP_BLOCK, HEAD_DIM), 0.1),
        "conv_out_norm": gain(ks[14], CONV_WIDTH),
        "attn_out_norm": gain(ks[15], ATTN_WIDTH),
        "w_out": nrm(ks[16], (L, MIX_WIDTH, D_MODEL), MIX_WIDTH ** -0.5),
        "ffn2_norm": gain(ks[17], D_MODEL),
        "ffn2_wg": nrm(ks[18], (L, D_MODEL, D_FF), D_MODEL ** -0.5),
        "ffn2_wu": nrm(ks[19], (L, D_MODEL, D_FF), D_MODEL ** -0.5),
        "ffn2_wd": nrm(ks[20], (L, D_FF, D_MODEL), D_FF ** -0.5),
        "final_norm": 1.0 + 0.02 * jax.random.normal(ks[21], (D_MODEL,), f32),
    }


def reference(x, ffn1_norm, ffn1_wg, ffn1_wu, ffn1_wd, mix_norm, w_in, conv_w,
              cmp_k_w1, cmp_k_w2, cmp_k_pe, cmp_v_w1, cmp_v_w2, cmp_v_pe,
              conv_out_norm, attn_out_norm, w_out, ffn2_norm, ffn2_wg, ffn2_wu,
              ffn2_wd, final_norm):
    cos, sin = rope_tables(x.shape[1])
    for i in range(DEPTH):
        x = x + 0.5 * swiglu(rmsnorm(x, ffn1_norm[i]), ffn1_wg[i], ffn1_wu[i], ffn1_wd[i])
        x = x + hybrid_mixer(rmsnorm(x, mix_norm[i]), w_in[i], conv_w[i],
                             cmp_k_w1[i], cmp_k_w2[i], cmp_k_pe[i],
                             cmp_v_w1[i], cmp_v_w2[i], cmp_v_pe[i],
                             conv_out_norm[i], attn_out_norm[i], w_out[i], cos, sin)
        x = x + 0.5 * swiglu(rmsnorm(x, ffn2_norm[i]), ffn2_wg[i], ffn2_wu[i], ffn2_wd[i])
    return rmsnorm(x, final_norm)
```

```python
import functools

import jax
import jax.numpy as jnp
from jax import lax
from jax.experimental import pallas as pl
from jax.experimental.pallas import tpu as pltpu

F32 = jnp.float32
BF16 = jnp.bfloat16

D_MODEL = 2048
D_FF = 5632
CONV_WIDTH = 1024
N_HEADS = 16
N_KV = 4
GROUP = N_HEADS // N_KV
HEAD_DIM = 64
ATTN_WIDTH = N_HEADS * HEAD_DIM
KV_WIDTH = N_KV * HEAD_DIM
CMP_BLOCK = 32
CMP_STRIDE = 16
CMP_HIDDEN = 256
SLC_BLOCK = 64
SLC_TOPK = 16
N_LOCAL = 2
WINDOW = 512
Q_BLOCK = 128
N_GATES = 3
ROPE_THETA = 10000.0
EPS = 1e-6
NEG = -1e30
FORCE = 1e9

LANES = 128
GROUP_WIDTH = GROUP * HEAD_DIM
GATE_PAD = LANES
TOKEN_TILE = 512
FF_TILE = 512
KEY_TILE = 512
WIN_KEYS = WINDOW + Q_BLOCK
VMEM_LIMIT = 56 * 1024 * 1024


def _rms(x):
    return x * lax.rsqrt(jnp.mean(x * x, axis=-1, keepdims=True) + EPS)


def _dot(a, b):
    return jnp.dot(a, b, preferred_element_type=F32)


def _dot_nt(a, b):
    return lax.dot_general(a, b, (((1,), (1,)), ((), ())), preferred_element_type=F32)


def _div_pow2(x, n):
    assert n & (n - 1) == 0
    return jnp.right_shift(x, n.bit_length() - 1)


def _resident(shape):
    return pl.BlockSpec(shape, lambda *_: (0,) * len(shape), pipeline_mode=pl.Buffered(1))


def _ffn_accumulate(h_ref, wg_ref, wu_ref, wd_ref, acc_ref):
    h = h_ref[...]
    a = _dot(h, wg_ref[...])
    u = _dot(h, wu_ref[...])
    act = (jax.nn.silu(a) * u).astype(BF16)
    acc_ref[...] += _dot(act, wd_ref[...])


def _ffn1_kernel(x_ref, g_ref, wg_ref, wu_ref, wd_ref, o_ref, h_ref, acc_ref):
    j = pl.program_id(1)

    @pl.when(j == 0)
    def _():
        h_ref[...] = (_rms(x_ref[...]) * g_ref[...]).astype(BF16)
        acc_ref[...] = jnp.zeros_like(acc_ref)

    _ffn_accumulate(h_ref, wg_ref, wu_ref, wd_ref, acc_ref)

    @pl.when(j == pl.num_programs(1) - 1)
    def _():
        o_ref[...] = x_ref[...] + 0.5 * acc_ref[...]


def _ffn1(x, g, wg, wu, wd):
    t, d = x.shape
    f = wg.shape[1]
    return pl.pallas_call(
        _ffn1_kernel,
        name="ffn1",
        grid=(t // TOKEN_TILE, f // FF_TILE),
        in_specs=[
            pl.BlockSpec((TOKEN_TILE, d), lambda i, j: (i, 0)),
            pl.BlockSpec((1, d), lambda i, j: (0, 0)),
            pl.BlockSpec((d, FF_TILE), lambda i, j: (0, j)),
            pl.BlockSpec((d, FF_TILE), lambda i, j: (0, j)),
            pl.BlockSpec((FF_TILE, d), lambda i, j: (j, 0)),
        ],
        out_specs=pl.BlockSpec((TOKEN_TILE, d), lambda i, j: (i, 0)),
        out_shape=jax.ShapeDtypeStruct((t, d), F32),
        scratch_shapes=[pltpu.VMEM((TOKEN_TILE, d), BF16), pltpu.VMEM((TOKEN_TILE, d), F32)],
        compiler_params=pltpu.CompilerParams(
            dimension_semantics=("parallel", "arbitrary"), vmem_limit_bytes=VMEM_LIMIT),
    )(x, g, wg, wu, wd)


def _out_ffn2_kernel(x_ref, yc_ref, ya_ref, ga_ref, wo_ref, g_ref, wg_ref, wu_ref, wd_ref,
                     fin_ref, o_ref, h_ref, acc_ref):
    j = pl.program_id(1)

    @pl.when(j == 0)
    def _():
        ya = (_rms(ya_ref[...]) * ga_ref[...]).astype(BF16)
        mix = _dot(yc_ref[...], wo_ref[0:CONV_WIDTH, :]) + _dot(ya, wo_ref[CONV_WIDTH:, :])
        x2 = x_ref[...] + mix
        o_ref[...] = x2
        h_ref[...] = (_rms(x2) * g_ref[...]).astype(BF16)
        acc_ref[...] = jnp.zeros_like(acc_ref)

    _ffn_accumulate(h_ref, wg_ref, wu_ref, wd_ref, acc_ref)

    @pl.when(j == pl.num_programs(1) - 1)
    def _():
        o_ref[...] = _rms(o_ref[...] + 0.5 * acc_ref[...]) * fin_ref[...]


def _out_ffn2(x1, yc, ya, ga, wo, g, wg, wu, wd, fin):
    t, d = x1.shape
    f = wg.shape[1]
    tm = TOKEN_TILE
    return pl.pallas_call(
        _out_ffn2_kernel,
        name="out_ffn2",
        grid=(t // tm, f // FF_TILE),
        in_specs=[
            pl.BlockSpec((tm, d), lambda i, j: (i, 0)),
            pl.BlockSpec((tm, CONV_WIDTH), lambda i, j: (i, 0)),
            pl.BlockSpec((tm, ATTN_WIDTH), lambda i, j: (i, 0)),
            pl.BlockSpec((1, ATTN_WIDTH), lambda i, j: (0, 0)),
            _resident(wo.shape),
            pl.BlockSpec((1, d), lambda i, j: (0, 0)),
            pl.BlockSpec((d, FF_TILE), lambda i, j: (0, j)),
            pl.BlockSpec((d, FF_TILE), lambda i, j: (0, j)),
            pl.BlockSpec((FF_TILE, d), lambda i, j: (j, 0)),
            pl.BlockSpec((1, d), lambda i, j: (0, 0)),
        ],
        out_specs=pl.BlockSpec((tm, d), lambda i, j: (i, 0)),
        out_shape=jax.ShapeDtypeStruct((t, d), F32),
        scratch_shapes=[pltpu.VMEM((tm, d), BF16), pltpu.VMEM((tm, d), F32)],
        compiler_params=pltpu.CompilerParams(
            dimension_semantics=("parallel", "arbitrary"), vmem_limit_bytes=VMEM_LIMIT),
    )(x1, yc, ya, ga, wo, g, wg, wu, wd, fin)


def _conv_proj_kernel(x_ref, g_ref, w_ref, cw_ref, gc_ref, o_ref, carry_ref, *, tiles_per_seq):
    i = pl.program_id(0)

    @pl.when(i % tiles_per_seq == 0)
    def _():
        carry_ref[...] = jnp.zeros_like(carry_ref)

    h = (_rms(x_ref[...]) * g_ref[...]).astype(BF16)
    c_h = _dot(h, w_ref[:, 0:CONV_WIDTH])
    c_b = _dot(h, w_ref[:, CONV_WIDTH:2 * CONV_WIDTH])
    c_c = _dot(h, w_ref[:, 2 * CONV_WIDTH:3 * CONV_WIDTH])
    u = c_c * c_h
    tm = u.shape[0]
    row = lax.broadcasted_iota(jnp.int32, u.shape, 0)
    prev1 = carry_ref[7:8, :]
    prev2 = carry_ref[6:7, :]
    u1 = jnp.where(row == 0, prev1, pltpu.roll(u, 1, 0))
    u2 = jnp.where(row == 0, prev2, jnp.where(row == 1, prev1, pltpu.roll(u, 2, 0)))
    carry_ref[...] = u[tm - 8:tm, :]
    y = c_b * (cw_ref[0:1, :] * u2 + cw_ref[1:2, :] * u1 + cw_ref[2:3, :] * u)
    o_ref[...] = (_rms(y) * gc_ref[...]).astype(BF16)


def _conv_proj(x1, g, w_conv, conv_w, gc, seq):
    t, d = x1.shape
    tm = TOKEN_TILE
    return pl.pallas_call(
        functools.partial(_conv_proj_kernel, tiles_per_seq=seq // tm),
        name="conv_proj",
        grid=(t // tm,),
        in_specs=[
            pl.BlockSpec((tm, d), lambda i: (i, 0)),
            pl.BlockSpec((1, d), lambda i: (0, 0)),
            _resident(w_conv.shape),
            pl.BlockSpec(conv_w.shape, lambda i: (0, 0)),
            pl.BlockSpec((1, CONV_WIDTH), lambda i: (0, 0)),
        ],
        out_specs=pl.BlockSpec((tm, CONV_WIDTH), lambda i: (i, 0)),
        out_shape=jax.ShapeDtypeStruct((t, CONV_WIDTH), BF16),
        scratch_shapes=[pltpu.VMEM((8, CONV_WIDTH), F32)],
        compiler_params=pltpu.CompilerParams(
            dimension_semantics=("arbitrary",), vmem_limit_bytes=VMEM_LIMIT),
    )(x1, g, w_conv, conv_w, gc)


def _rope(x, cos, sin_signed, first_half):
    outs = []
    for k in range(x.shape[1] // LANES):
        xc = x[:, k * LANES:(k + 1) * LANES]
        partner = jnp.where(first_half, pltpu.roll(xc, LANES - HEAD_DIM // 2, 1),
                            pltpu.roll(xc, HEAD_DIM // 2, 1))
        outs.append(xc * cos + partner * sin_signed)
    return outs


_Q0 = 0
_KC0 = ATTN_WIDTH
_VC0 = _KC0 + KV_WIDTH
_KS0 = _VC0 + KV_WIDTH
_VS0 = _KS0 + KV_WIDTH
_KW0 = _VS0 + KV_WIDTH
_VW0 = _KW0 + KV_WIDTH
_G0 = _VW0 + KV_WIDTH
ATTN_PROJ_WIDTH = _G0 + N_KV * GATE_PAD


def _attn_proj_kernel(x_ref, g_ref, w_ref, cos_ref, sin_ref,
                      q_ref, kc_ref, vc_ref, ks_ref, vs_ref, kw_ref, vw_ref, gate_ref):
    h = (_rms(x_ref[...]) * g_ref[...]).astype(BF16)
    cos = cos_ref[...]
    sin = sin_ref[...]
    lane = lax.broadcasted_iota(jnp.int32, cos.shape, 1)
    first_half = (lane & (HEAD_DIM - 1)) < HEAD_DIM // 2

    def proj(c0, width):
        return _dot(h, w_ref[:, c0:c0 + width])

    def rope_to(ref, c0, width, scale, dtype):
        for k, piece in enumerate(_rope(proj(c0, width), cos, sin, first_half)):
            if scale is not None:
                piece = piece * scale
            ref[:, k * LANES:(k + 1) * LANES] = piece.astype(dtype)

    rope_to(q_ref, _Q0, ATTN_WIDTH, HEAD_DIM ** -0.5, BF16)
    rope_to(kc_ref, _KC0, KV_WIDTH, None, F32)
    vc_ref[...] = proj(_VC0, KV_WIDTH)
    rope_to(ks_ref, _KS0, KV_WIDTH, None, BF16)
    vs_ref[...] = proj(_VS0, KV_WIDTH).astype(BF16)
    rope_to(kw_ref, _KW0, KV_WIDTH, None, BF16)
    vw_ref[...] = proj(_VW0, KV_WIDTH).astype(BF16)
    gate_ref[...] = jax.nn.sigmoid(proj(_G0, N_KV * GATE_PAD))


def _attn_proj(x1, g, w_attn, cos, sin, seq):
    t, d = x1.shape
    tm = TOKEN_TILE
    tps = seq // tm
    row = lambda w: pl.BlockSpec((tm, w), lambda i: (i, 0))
    out_widths = [ATTN_WIDTH] + [KV_WIDTH] * 6 + [N_KV * GATE_PAD]
    out_dtypes = [BF16, F32, F32, BF16, BF16, BF16, BF16, F32]
    return pl.pallas_call(
        _attn_proj_kernel,
        name="attn_proj",
        grid=(t // tm,),
        in_specs=[
            row(d),
            pl.BlockSpec((1, d), lambda i: (0, 0)),
            _resident(w_attn.shape),
            pl.BlockSpec((tm, LANES), lambda i: (i % tps, 0)),
            pl.BlockSpec((tm, LANES), lambda i: (i % tps, 0)),
        ],
        out_specs=[row(w) for w in out_widths],
        out_shape=[jax.ShapeDtypeStruct((t, w), dt) for w, dt in zip(out_widths, out_dtypes)],
        compiler_params=pltpu.CompilerParams(
            dimension_semantics=("parallel",), vmem_limit_bytes=VMEM_LIMIT),
    )(x1, g, w_attn, cos, sin)


def _compress_kernel(c_ref, pe_ref, w1_ref, w2_ref, o_ref):
    c = c_ref[0, 0]
    a = _dot((c + pe_ref[0]).astype(BF16), w1_ref[0])
    b = _dot((c + pe_ref[1]).astype(BF16), w1_ref[1])
    n = a.shape[0]
    hid = jax.nn.gelu(a + pltpu.roll(b, n - 1, 0))
    o_ref[0, 0] = _dot(hid.astype(BF16), w2_ref[...]).astype(BF16)


def _compress(chunks, pe, w1, w2_rep):
    bsz, nkv, n, width = chunks.shape
    return pl.pallas_call(
        _compress_kernel,
        name="compress",
        grid=(bsz, nkv),
        in_specs=[
            pl.BlockSpec((1, 1, n, width), lambda b, h: (b, h, 0, 0)),
            pl.BlockSpec(pe.shape, lambda b, h: (0, 0, 0)),
            pl.BlockSpec(w1.shape, lambda b, h: (0, 0, 0)),
            pl.BlockSpec(w2_rep.shape, lambda b, h: (0, 0)),
        ],
        out_specs=pl.BlockSpec((1, 1, n, GROUP_WIDTH), lambda b, h: (b, h, 0, 0)),
        out_shape=jax.ShapeDtypeStruct((bsz, nkv, n, GROUP_WIDTH), BF16),
        compiler_params=pltpu.CompilerParams(
            dimension_semantics=("parallel", "parallel"), vmem_limit_bytes=VMEM_LIMIT),
    )(chunks, pe, w1, w2_rep)


def _stack4(x):
    return jnp.concatenate([x] * GROUP, axis=0)


def _nsa_kernel(q_ref, gate_ref, kc_ref, vc_ref, ks_ref, vs_ref, kw_ref, vw_ref, o_ref,
                ksr_ref, vsr_ref, kwr_ref, vwr_ref, imp_ref, m_ref, l_ref, acc_ref, *, seq):
    h = pl.program_id(1)
    c = pl.program_id(2)
    t0 = c * Q_BLOCK
    rows = GROUP * Q_BLOCK
    n_cmp = kc_ref.shape[2]
    n_slc = seq // SLC_BLOCK

    @pl.when(c == 0)
    def _():
        r = lax.broadcasted_iota(jnp.int32, (KV_WIDTH, GROUP_WIDTH), 0)
        col = lax.broadcasted_iota(jnp.int32, (KV_WIDTH, GROUP_WIDTH), 1)
        rep = (r == h * HEAD_DIM + (col & (HEAD_DIM - 1))).astype(BF16)
        for src, dst in ((ks_ref, ksr_ref), (vs_ref, vsr_ref), (kw_ref, kwr_ref), (vw_ref, vwr_ref)):
            for k in range(seq // KEY_TILE):
                sl = pl.ds(k * KEY_TILE, KEY_TILE)
                dst[sl, :] = _dot(src[0, sl, :], rep).astype(BF16)

    lane_group = _div_pow2(lax.broadcasted_iota(jnp.int32, (Q_BLOCK, GROUP_WIDTH), 1), HEAD_DIM)
    qb = q_ref[0]
    qpad = jnp.concatenate(
        [jnp.where(lane_group == g, qb, jnp.zeros_like(qb)) for g in range(GROUP)], axis=0)
    t_col = t0 + lax.broadcasted_iota(jnp.int32, (Q_BLOCK, 1), 0)

    sc = _dot_nt(qpad, kc_ref[0, 0])
    ci = lax.broadcasted_iota(jnp.int32, (Q_BLOCK, n_cmp), 1)
    c_ok = (ci * CMP_STRIDE + (CMP_BLOCK - 1)) <= t_col
    c_okf = _stack4(c_ok.astype(F32))
    sc = sc + _stack4(jnp.where(c_ok, 0.0, NEG))
    pc = jnp.exp(sc - jnp.max(sc, axis=1, keepdims=True)) * c_okf
    lc = jnp.sum(pc, axis=1, keepdims=True)
    pc = pc / jnp.where(lc > 0.0, lc, 1.0)
    o_cmp = _dot(pc.astype(BF16), vc_ref[0, 0])

    pcs = pc[0:Q_BLOCK]
    for g in range(1, GROUP):
        pcs = pcs + pc[g * Q_BLOCK:(g + 1) * Q_BLOCK]
    pcs_hi = pcs.astype(BF16)
    pcs_lo = (pcs - pcs_hi.astype(F32)).astype(BF16)
    jj = lax.broadcasted_iota(jnp.int32, (n_slc, n_cmp), 0)
    ii = lax.broadcasted_iota(jnp.int32, (n_slc, n_cmp), 1)
    ov = (jnp.minimum(ii * CMP_STRIDE + CMP_BLOCK, jj * SLC_BLOCK + SLC_BLOCK)
          - jnp.maximum(ii * CMP_STRIDE, jj * SLC_BLOCK))
    ov = (jnp.maximum(ov, 0).astype(F32) * (1.0 / CMP_BLOCK)).astype(BF16)
    imp = _dot_nt(ov, pcs_hi) + _dot_nt(ov, pcs_lo)

    jb = lax.broadcasted_iota(jnp.int32, (n_slc, Q_BLOCK), 0)
    tb = _div_pow2(t0 + lax.broadcasted_iota(jnp.int32, (n_slc, Q_BLOCK), 1), SLC_BLOCK)
    forced = (jb == 0) | ((jb <= tb) & (jb > tb - N_LOCAL))
    imp = jnp.where(forced, FORCE, imp)
    imp = jnp.where(jb <= tb, imp, NEG)
    imp_ref[...] = imp

    def rank_step(j, cnt):
        other = imp_ref[pl.ds(j, 1), :]
        ahead = (other > imp) | ((other == imp) & (j < jb))
        return cnt + jnp.where(ahead, 1, 0)

    cnt = lax.fori_loop(0, n_slc, rank_step, jnp.zeros((n_slc, Q_BLOCK), jnp.int32))
    sel = ((cnt < SLC_TOPK) & (imp > 0.5 * NEG)).astype(F32)
    sel = jnp.concatenate([sel, jnp.zeros((LANES - n_slc, Q_BLOCK), F32)], axis=0)
    sel_tj = sel.T.astype(BF16)

    m_ref[...] = jnp.full_like(m_ref, NEG)
    l_ref[...] = jnp.zeros_like(l_ref)
    acc_ref[...] = jnp.zeros_like(acc_ref)
    blocks_per_tile = KEY_TILE // SLC_BLOCK

    def key_tile(kt, _):
        k0 = pl.multiple_of(kt * KEY_TILE, KEY_TILE)
        s = _dot_nt(qpad, ksr_ref[pl.ds(k0, KEY_TILE), :])
        jrow = lax.broadcasted_iota(jnp.int32, (LANES, KEY_TILE), 0)
        pcol = lax.broadcasted_iota(jnp.int32, (LANES, KEY_TILE), 1)
        expand = (jrow == kt * blocks_per_tile + _div_pow2(pcol, SLC_BLOCK)).astype(BF16)
        picked = _dot(sel_tj, expand)
        pos = k0 + lax.broadcasted_iota(jnp.int32, (Q_BLOCK, KEY_TILE), 1)
        ok = (picked > 0.5) & (pos <= t_col)
        s = s + _stack4(jnp.where(ok, 0.0, NEG))
        m_old = m_ref[...]
        m_new = jnp.maximum(m_old, jnp.max(s, axis=1, keepdims=True))
        alpha = jnp.exp(m_old - m_new)
        p = jnp.exp(s - m_new)
        l_ref[...] = alpha * l_ref[...] + jnp.sum(p, axis=1, keepdims=True)
        acc_ref[...] = alpha * acc_ref[...] + _dot(p.astype(BF16), vsr_ref[pl.ds(k0, KEY_TILE), :])
        m_ref[...] = m_new
        return 0

    lax.fori_loop(0, _div_pow2(t0 + Q_BLOCK + KEY_TILE - 1, KEY_TILE), key_tile, 0)
    o_slc = acc_ref[...] / l_ref[...]

    w0 = pl.multiple_of(jnp.maximum(t0 - WINDOW, 0), Q_BLOCK)
    sw = _dot_nt(qpad, kwr_ref[pl.ds(w0, WIN_KEYS), :])
    dist = t_col - (w0 + lax.broadcasted_iota(jnp.int32, (Q_BLOCK, WIN_KEYS), 1))
    sw = sw + _stack4(jnp.where((dist >= 0) & (dist < WINDOW), 0.0, NEG))
    pw = jnp.exp(sw - jnp.max(sw, axis=1, keepdims=True))
    lw = jnp.sum(pw, axis=1, keepdims=True)
    o_win = _dot(pw.astype(BF16), vwr_ref[pl.ds(w0, WIN_KEYS), :]) / lw

    gates = gate_ref[0]
    out = jnp.zeros((Q_BLOCK, GROUP_WIDTH), F32)
    for g in range(GROUP):
        rs = slice(g * Q_BLOCK, (g + 1) * Q_BLOCK)
        col = g * N_GATES
        og = (gates[:, col:col + 1] * o_cmp[rs] + gates[:, col + 1:col + 2] * o_slc[rs]
              + gates[:, col + 2:col + 3] * o_win[rs])
        out = out + jnp.where(lane_group == g, og, 0.0)
    o_ref[0] = out


def _nsa(q, gates, kc, vc, ks, vs, kw, vw):
    bsz, seq, _ = q.shape
    n_cmp = kc.shape[2]
    rows = GROUP * Q_BLOCK
    kv_spec = pl.BlockSpec((1, seq, KV_WIDTH), lambda b, h, c: (b, 0, 0))
    cmp_spec = pl.BlockSpec((1, 1, n_cmp, GROUP_WIDTH), lambda b, h, c: (b, h, 0, 0))
    return pl.pallas_call(
        functools.partial(_nsa_kernel, seq=seq),
        name="nsa",
        grid=(bsz, N_KV, seq // Q_BLOCK),
        in_specs=[
            pl.BlockSpec((1, Q_BLOCK, GROUP_WIDTH), lambda b, h, c: (b, c, h)),
            pl.BlockSpec((1, Q_BLOCK, GATE_PAD), lambda b, h, c: (b, c, h)),
            cmp_spec, cmp_spec, kv_spec, kv_spec, kv_spec, kv_spec,
        ],
        out_specs=pl.BlockSpec((1, Q_BLOCK, GROUP_WIDTH), lambda b, h, c: (b, c, h)),
        out_shape=jax.ShapeDtypeStruct((bsz, seq, ATTN_WIDTH), F32),
        scratch_shapes=[pltpu.VMEM((seq, GROUP_WIDTH), BF16)] * 4 + [
            pltpu.VMEM((seq // SLC_BLOCK, Q_BLOCK), F32),
            pltpu.VMEM((rows, 1), F32),
            pltpu.VMEM((rows, 1), F32),
            pltpu.VMEM((rows, GROUP_WIDTH), F32),
        ],
        compiler_params=pltpu.CompilerParams(
            dimension_semantics=("parallel", "parallel", "arbitrary"),
            vmem_limit_bytes=VMEM_LIMIT),
    )(q, gates, kc, vc, ks, vs, kw, vw)


def _rope_tables(seq):
    pos = jnp.arange(seq, dtype=F32)
    inv = ROPE_THETA ** (-jnp.arange(0, HEAD_DIM, 2, dtype=F32) / HEAD_DIM)
    ang = pos[:, None] * inv[None, :]
    cos, sin = jnp.cos(ang), jnp.sin(ang)
    reps = LANES // HEAD_DIM
    return (jnp.tile(jnp.concatenate([cos, cos], axis=1), (1, reps)),
            jnp.tile(jnp.concatenate([-sin, sin], axis=1), (1, reps)))


def _head_chunks(tok, bsz, seq):
    x = tok.reshape(bsz, seq // CMP_STRIDE, CMP_STRIDE, N_KV, HEAD_DIM)
    return x.transpose(0, 3, 1, 2, 4).reshape(bsz, N_KV, seq // CMP_STRIDE, CMP_STRIDE * HEAD_DIM)


def _compress_params(w1, w2, pe):
    halves = CMP_BLOCK // CMP_STRIDE
    w1 = w1.reshape(halves, CMP_STRIDE * HEAD_DIM, CMP_HIDDEN).astype(BF16)
    pe = pe.reshape(halves, 1, CMP_STRIDE * HEAD_DIM)
    return pe, w1, jnp.tile(w2, (1, GROUP)).astype(BF16)


def kernel(x, ffn1_norm, ffn1_wg, ffn1_wu, ffn1_wd, mix_norm, w_in, conv_w, cmp_k_w1, cmp_k_w2,
           cmp_k_pe, cmp_v_w1, cmp_v_w2, cmp_v_pe, conv_out_norm, attn_out_norm, w_out, ffn2_norm,
           ffn2_wg, ffn2_wu, ffn2_wd, final_norm):
    bsz, seq, d = x.shape
    assert CMP_BLOCK == 2 * CMP_STRIDE and seq % TOKEN_TILE == 0 and seq % KEY_TILE == 0
    assert x.shape[0] * seq % TOKEN_TILE == 0 and ffn1_wg.shape[0] == 1
    xt = x.reshape(bsz * seq, d)
    cos, sin = _rope_tables(seq)

    w = w_in[0]
    w_conv = w[:, :3 * CONV_WIDTH].astype(BF16)
    g0 = 3 * CONV_WIDTH + ATTN_WIDTH + 6 * KV_WIDTH
    wg = w[:, g0:].reshape(d, N_KV, GROUP * N_GATES)
    wg = jnp.pad(wg, ((0, 0), (0, 0), (0, GATE_PAD - GROUP * N_GATES))).reshape(d, N_KV * GATE_PAD)
    w_attn = jnp.concatenate([w[:, 3 * CONV_WIDTH:g0], wg], axis=1).astype(BF16)

    x1 = _ffn1(xt, ffn1_norm, ffn1_wg[0].astype(BF16), ffn1_wu[0].astype(BF16),
               ffn1_wd[0].astype(BF16))
    yc = _conv_proj(x1, mix_norm, w_conv, conv_w[0], conv_out_norm, seq)
    q, k_c, v_c, k_s, v_s, k_w, v_w, gates = _attn_proj(x1, mix_norm, w_attn, cos, sin, seq)

    kc = _compress(_head_chunks(k_c, bsz, seq), *_compress_params(cmp_k_w1[0], cmp_k_w2[0], cmp_k_pe[0]))
    vc = _compress(_head_chunks(v_c, bsz, seq), *_compress_params(cmp_v_w1[0], cmp_v_w2[0], cmp_v_pe[0]))

    per_batch = lambda a: a.reshape(bsz, seq, a.shape[-1])
    ya = _nsa(per_batch(q), per_batch(gates), kc, vc,
              per_batch(k_s), per_batch(v_s), per_batch(k_w), per_batch(v_w))

    out = _out_ffn2(x1, yc, ya.reshape(bsz * seq, ATTN_WIDTH), attn_out_norm,
                    w_out[0].astype(BF16), ffn2_norm, ffn2_wg[0].astype(BF16),
                    ffn2_wu[0].astype(BF16), ffn2_wd[0].astype(BF16), final_norm.reshape(1, d))
    return out.reshape(bsz, seq, d)
```

```python
import functools

import jax
import jax.numpy as jnp
from jax import lax
from jax.experimental import pallas as pl
from jax.experimental.pallas import tpu as pltpu

F32 = jnp.float32
BF16 = jnp.bfloat16

D_MODEL = 2048
D_FF = 5632
CONV_WIDTH = 1024
N_HEADS = 16
N_KV = 4
GROUP = N_HEADS // N_KV
HEAD_DIM = 64
ATTN_WIDTH = N_HEADS * HEAD_DIM
KV_WIDTH = N_KV * HEAD_DIM
CMP_BLOCK = 32
CMP_STRIDE = 16
CMP_HIDDEN = 256
SLC_BLOCK = 64
SLC_TOPK = 16
N_LOCAL = 2
WINDOW = 512
Q_BLOCK = 128
N_GATES = 3
ROPE_THETA = 10000.0
EPS = 1e-6
NEG = -1e30
FORCE = 1e9

LANES = 128
GROUP_WIDTH = GROUP * HEAD_DIM
GATE_PAD = LANES
TOKEN_TILE = 512
FF_TILE = 512
KEY_TILE = 512
SUB_TILE = 256
WIN_KEYS = WINDOW + Q_BLOCK
VMEM_LIMIT = 56 * 1024 * 1024
Q_SCALE = HEAD_DIM ** -0.5 * 1.4426950408889634


def _rms(x):
    return x * lax.rsqrt(jnp.mean(x * x, axis=-1, keepdims=True) + EPS)


def _dot(a, b):
    return jnp.dot(a, b, preferred_element_type=F32)


def _dot_nt(a, b):
    return lax.dot_general(a, b, (((1,), (1,)), ((), ())), preferred_element_type=F32)


def _div_pow2(x, n):
    assert n & (n - 1) == 0
    return jnp.right_shift(x, n.bit_length() - 1)


def _iota(shape, dim):
    return lax.broadcasted_iota(jnp.int32, shape, dim)


def _resident(shape):
    return pl.BlockSpec(shape, lambda *_: (0,) * len(shape), pipeline_mode=pl.Buffered(1))


def _ffn_accumulate(h_ref, wg_ref, wu_ref, wd_ref, acc_ref):
    h = h_ref[...]
    a = _dot(h, wg_ref[...])
    u = _dot(h, wu_ref[...])
    act = (jax.nn.silu(a) * u).astype(BF16)
    acc_ref[...] += _dot(act, wd_ref[...])


def _ffn1_kernel(x_ref, g_ref, wg_ref, wu_ref, wd_ref, o_ref, h_ref, acc_ref):
    j = pl.program_id(1)

    @pl.when(j == 0)
    def _():
        h_ref[...] = (_rms(x_ref[...]) * g_ref[...]).astype(BF16)
        acc_ref[...] = jnp.zeros_like(acc_ref)

    _ffn_accumulate(h_ref, wg_ref, wu_ref, wd_ref, acc_ref)

    @pl.when(j == pl.num_programs(1) - 1)
    def _():
        o_ref[...] = x_ref[...] + 0.5 * acc_ref[...]


def _ffn1(x, g, wg, wu, wd):
    t, d = x.shape
    f = wg.shape[1]
    return pl.pallas_call(
        _ffn1_kernel,
        name="ffn1",
        grid=(t // TOKEN_TILE, f // FF_TILE),
        in_specs=[
            pl.BlockSpec((TOKEN_TILE, d), lambda i, j: (i, 0)),
            pl.BlockSpec((1, d), lambda i, j: (0, 0)),
            pl.BlockSpec((d, FF_TILE), lambda i, j: (0, j)),
            pl.BlockSpec((d, FF_TILE), lambda i, j: (0, j)),
            pl.BlockSpec((FF_TILE, d), lambda i, j: (j, 0)),
        ],
        out_specs=pl.BlockSpec((TOKEN_TILE, d), lambda i, j: (i, 0)),
        out_shape=jax.ShapeDtypeStruct((t, d), F32),
        scratch_shapes=[pltpu.VMEM((TOKEN_TILE, d), BF16), pltpu.VMEM((TOKEN_TILE, d), F32)],
        compiler_params=pltpu.CompilerParams(
            dimension_semantics=("parallel", "arbitrary"), vmem_limit_bytes=VMEM_LIMIT),
    )(x, g, wg, wu, wd)


def _out_ffn2_kernel(x_ref, yc_ref, ya_ref, ga_ref, wo_ref, g_ref, wg_ref, wu_ref, wd_ref,
                     fin_ref, o_ref, h_ref, acc_ref):
    j = pl.program_id(1)

    @pl.when(j == 0)
    def _():
        ya = (_rms(ya_ref[0].T) * ga_ref[...]).astype(BF16)
        mix = _dot(yc_ref[...], wo_ref[0:CONV_WIDTH, :]) + _dot(ya, wo_ref[CONV_WIDTH:, :])
        x2 = x_ref[...] + mix
        o_ref[...] = x2
        h_ref[...] = (_rms(x2) * g_ref[...]).astype(BF16)
        acc_ref[...] = jnp.zeros_like(acc_ref)

    _ffn_accumulate(h_ref, wg_ref, wu_ref, wd_ref, acc_ref)

    @pl.when(j == pl.num_programs(1) - 1)
    def _():
        o_ref[...] = _rms(o_ref[...] + 0.5 * acc_ref[...]) * fin_ref[...]


def _out_ffn2(x1, yc, ya_t, ga, wo, g, wg, wu, wd, fin):
    t, d = x1.shape
    f = wg.shape[1]
    tm = TOKEN_TILE
    tps = ya_t.shape[2] // tm
    return pl.pallas_call(
        _out_ffn2_kernel,
        name="out_ffn2",
        grid=(t // tm, f // FF_TILE),
        in_specs=[
            pl.BlockSpec((tm, d), lambda i, j: (i, 0)),
            pl.BlockSpec((tm, CONV_WIDTH), lambda i, j: (i, 0)),
            pl.BlockSpec((1, ATTN_WIDTH, tm), lambda i, j: (i // tps, 0, i % tps)),
            pl.BlockSpec((1, ATTN_WIDTH), lambda i, j: (0, 0)),
            _resident(wo.shape),
            pl.BlockSpec((1, d), lambda i, j: (0, 0)),
            pl.BlockSpec((d, FF_TILE), lambda i, j: (0, j)),
            pl.BlockSpec((d, FF_TILE), lambda i, j: (0, j)),
            pl.BlockSpec((FF_TILE, d), lambda i, j: (j, 0)),
            pl.BlockSpec((1, d), lambda i, j: (0, 0)),
        ],
        out_specs=pl.BlockSpec((tm, d), lambda i, j: (i, 0)),
        out_shape=jax.ShapeDtypeStruct((t, d), F32),
        scratch_shapes=[pltpu.VMEM((tm, d), BF16), pltpu.VMEM((tm, d), F32)],
        compiler_params=pltpu.CompilerParams(
            dimension_semantics=("parallel", "arbitrary"), vmem_limit_bytes=VMEM_LIMIT),
    )(x1, yc, ya_t, ga, wo, g, wg, wu, wd, fin)


def _conv_proj_kernel(x_ref, g_ref, w_ref, cw_ref, gc_ref, o_ref, carry_ref, *, tiles_per_seq):
    i = pl.program_id(0)

    @pl.when(i % tiles_per_seq == 0)
    def _():
        carry_ref[...] = jnp.zeros_like(carry_ref)

    h = (_rms(x_ref[...]) * g_ref[...]).astype(BF16)
    c_h = _dot(h, w_ref[:, 0:CONV_WIDTH])
    c_b = _dot(h, w_ref[:, CONV_WIDTH:2 * CONV_WIDTH])
    c_c = _dot(h, w_ref[:, 2 * CONV_WIDTH:3 * CONV_WIDTH])
    u = c_c * c_h
    tm = u.shape[0]
    row = lax.broadcasted_iota(jnp.int32, u.shape, 0)
    prev1 = carry_ref[7:8, :]
    prev2 = carry_ref[6:7, :]
    u1 = jnp.where(row == 0, prev1, pltpu.roll(u, 1, 0))
    u2 = jnp.where(row == 0, prev2, jnp.where(row == 1, prev1, pltpu.roll(u, 2, 0)))
    carry_ref[...] = u[tm - 8:tm, :]
    y = c_b * (cw_ref[0:1, :] * u2 + cw_ref[1:2, :] * u1 + cw_ref[2:3, :] * u)
    o_ref[...] = (_rms(y) * gc_ref[...]).astype(BF16)


def _conv_proj(x1, g, w_conv, conv_w, gc, seq):
    t, d = x1.shape
    tm = TOKEN_TILE
    return pl.pallas_call(
        functools.partial(_conv_proj_kernel, tiles_per_seq=seq // tm),
        name="conv_proj",
        grid=(t // tm,),
        in_specs=[
            pl.BlockSpec((tm, d), lambda i: (i, 0)),
            pl.BlockSpec((1, d), lambda i: (0, 0)),
            _resident(w_conv.shape),
            pl.BlockSpec(conv_w.shape, lambda i: (0, 0)),
            pl.BlockSpec((1, CONV_WIDTH), lambda i: (0, 0)),
        ],
        out_specs=pl.BlockSpec((tm, CONV_WIDTH), lambda i: (i, 0)),
        out_shape=jax.ShapeDtypeStruct((t, CONV_WIDTH), BF16),
        scratch_shapes=[pltpu.VMEM((8, CONV_WIDTH), F32)],
        compiler_params=pltpu.CompilerParams(
            dimension_semantics=("arbitrary",), vmem_limit_bytes=VMEM_LIMIT),
    )(x1, g, w_conv, conv_w, gc)


def _rope(x, cos, sin_signed, first_half):
    outs = []
    for k in range(x.shape[1] // LANES):
        xc = x[:, k * LANES:(k + 1) * LANES]
        partner = jnp.where(first_half, pltpu.roll(xc, LANES - HEAD_DIM // 2, 1),
                            pltpu.roll(xc, HEAD_DIM // 2, 1))
        outs.append(xc * cos + partner * sin_signed)
    return outs


_Q0 = 0
_KC0 = ATTN_WIDTH
_VC0 = _KC0 + KV_WIDTH
_KS0 = _VC0 + KV_WIDTH
_VS0 = _KS0 + KV_WIDTH
_KW0 = _VS0 + KV_WIDTH
_VW0 = _KW0 + KV_WIDTH
_G0 = _VW0 + KV_WIDTH
ATTN_PROJ_WIDTH = _G0 + N_KV * GATE_PAD


def _attn_proj_kernel(x_ref, g_ref, w_ref, cos_ref, sin_ref,
                      q_ref, kc_ref, vc_ref, ks_ref, vs_ref, kw_ref, vw_ref, gate_ref):
    h = (_rms(x_ref[...]) * g_ref[...]).astype(BF16)
    cos = cos_ref[...]
    sin = sin_ref[...]
    lane = lax.broadcasted_iota(jnp.int32, cos.shape, 1)
    first_half = (lane & (HEAD_DIM - 1)) < HEAD_DIM // 2

    def proj(c0, width):
        return _dot(h, w_ref[:, c0:c0 + width])

    def rope_to(ref, c0, width, dtype):
        for k, piece in enumerate(_rope(proj(c0, width), cos, sin, first_half)):
            ref[:, k * LANES:(k + 1) * LANES] = piece.astype(dtype)

    for k, piece in enumerate(_rope(proj(_Q0, ATTN_WIDTH), cos, sin, first_half)):
        q_ref[0, k * LANES:(k + 1) * LANES, :] = (piece * Q_SCALE).T.astype(BF16)
    rope_to(kc_ref, _KC0, KV_WIDTH, F32)
    vc_ref[...] = proj(_VC0, KV_WIDTH)
    rope_to(ks_ref, _KS0, KV_WIDTH, BF16)
    vs_ref[...] = proj(_VS0, KV_WIDTH).astype(BF16)
    rope_to(kw_ref, _KW0, KV_WIDTH, BF16)
    vw_ref[...] = proj(_VW0, KV_WIDTH).astype(BF16)
    gate_ref[0] = jax.nn.sigmoid(proj(_G0, N_KV * GATE_PAD)).T


def _attn_proj(x1, g, w_attn, cos, sin, seq):
    t, d = x1.shape
    tm = TOKEN_TILE
    tps = seq // tm
    bsz = t // seq
    row = lambda w: pl.BlockSpec((tm, w), lambda i: (i, 0))
    col = lambda w: pl.BlockSpec((1, w, tm), lambda i: (i // tps, 0, i % tps))
    kv_dtypes = [F32, F32, BF16, BF16, BF16, BF16]
    out_specs = [col(ATTN_WIDTH)] + [row(KV_WIDTH)] * 6 + [col(N_KV * GATE_PAD)]
    out_shape = ([jax.ShapeDtypeStruct((bsz, ATTN_WIDTH, seq), BF16)]
                 + [jax.ShapeDtypeStruct((t, KV_WIDTH), dt) for dt in kv_dtypes]
                 + [jax.ShapeDtypeStruct((bsz, N_KV * GATE_PAD, seq), F32)])
    return pl.pallas_call(
        _attn_proj_kernel,
        name="attn_proj",
        grid=(t // tm,),
        in_specs=[
            row(d),
            pl.BlockSpec((1, d), lambda i: (0, 0)),
            _resident(w_attn.shape),
            pl.BlockSpec((tm, LANES), lambda i: (i % tps, 0)),
            pl.BlockSpec((tm, LANES), lambda i: (i % tps, 0)),
        ],
        out_specs=out_specs,
        out_shape=out_shape,
        compiler_params=pltpu.CompilerParams(
            dimension_semantics=("parallel",), vmem_limit_bytes=VMEM_LIMIT),
    )(x1, g, w_attn, cos, sin)


def _compress_kernel(c_ref, pe_ref, w1_ref, w2_ref, o_ref, *, transposed):
    c = c_ref[0, 0]
    a = _dot((c + pe_ref[0]).astype(BF16), w1_ref[0])
    b = _dot((c + pe_ref[1]).astype(BF16), w1_ref[1])
    n = a.shape[0]
    hid = jax.nn.gelu(a + pltpu.roll(b, n - 1, 0)).astype(BF16)
    if transposed:
        o_ref[0, 0] = _dot_nt(w2_ref[...], hid).astype(BF16)
    else:
        o_ref[0, 0] = _dot(hid, w2_ref[...]).astype(BF16)


def _compress(chunks, pe, w1, w2, transposed):
    bsz, nkv, n, width = chunks.shape
    out_block = (1, 1, HEAD_DIM, n) if transposed else (1, 1, n, LANES)
    return pl.pallas_call(
        functools.partial(_compress_kernel, transposed=transposed),
        name="compress_v" if transposed else "compress_k",
        grid=(bsz, nkv),
        in_specs=[
            pl.BlockSpec((1, 1, n, width), lambda b, h: (b, h, 0, 0)),
            pl.BlockSpec(pe.shape, lambda b, h: (0, 0, 0)),
            pl.BlockSpec(w1.shape, lambda b, h: (0, 0, 0)),
            pl.BlockSpec(w2.shape, lambda b, h: (0, 0)),
        ],
        out_specs=pl.BlockSpec(out_block, lambda b, h: (b, h, 0, 0)),
        out_shape=jax.ShapeDtypeStruct((bsz, nkv) + out_block[2:], BF16),
        compiler_params=pltpu.CompilerParams(
            dimension_semantics=("parallel", "parallel"), vmem_limit_bytes=VMEM_LIMIT),
    )(chunks, pe, w1, w2)


def _tile4(x):
    return jnp.concatenate([x] * GROUP, axis=1)


def _col_max(x):
    return jnp.max(x, axis=0, keepdims=True)


def _col_sum(x):
    return jnp.sum(x, axis=0, keepdims=True)


def _nsa_kernel(q_ref, gate_ref, kc_ref, vc_ref, ks_ref, vs_ref, kw_ref, vw_ref, o_ref,
                kse_ref, kwe_ref, vst_ref, vwt_ref, imp_ref, *, seq):
    h = pl.program_id(1)
    c = pl.program_id(2)
    t0 = c * Q_BLOCK
    n_cmp = kc_ref.shape[2]
    n_slc = seq // SLC_BLOCK
    cols = GROUP * Q_BLOCK
    assert n_slc + HEAD_DIM == LANES and n_slc % 8 == 0

    @pl.when(c == 0)
    def _():
        r = _iota((KV_WIDTH, LANES), 0)
        ln = _iota((KV_WIDTH, LANES), 1)
        pick = ((r == h * HEAD_DIM + ln) & (ln < HEAD_DIM)).astype(BF16)
        pick_t = (_iota((HEAD_DIM, KV_WIDTH), 1)
                  == h * HEAD_DIM + _iota((HEAD_DIM, KV_WIDTH), 0)).astype(BF16)

        def build_k(i, _):
            sl = pl.ds(pl.multiple_of(i * KEY_TILE, KEY_TILE), KEY_TILE)
            pos = i * KEY_TILE + _iota((KEY_TILE, LANES), 0)
            lane = _iota((KEY_TILE, LANES), 1)
            onehot = jnp.where(lane - HEAD_DIM == _div_pow2(pos, SLC_BLOCK), 1.0, 0.0)
            kse_ref[sl, :] = (_dot(ks_ref[0, sl, :], pick) + onehot).astype(BF16)
            kwe_ref[sl, :] = _dot(kw_ref[0, sl, :], pick).astype(BF16)
            return 0

        def build_vs(i, _):
            sl = pl.ds(pl.multiple_of(i * SUB_TILE, SUB_TILE), SUB_TILE)
            vst_ref[i] = _dot_nt(pick_t, vs_ref[0, sl, :]).astype(BF16)
            return 0

        def build_vw(i, _):
            sl = pl.ds(pl.multiple_of(i * Q_BLOCK, Q_BLOCK), Q_BLOCK)
            vwt_ref[i] = _dot_nt(pick_t, vw_ref[0, sl, :]).astype(BF16)
            return 0

        lax.fori_loop(0, seq // KEY_TILE, build_k, 0)
        lax.fori_loop(0, seq // SUB_TILE, build_vs, 0)
        lax.fori_loop(0, seq // Q_BLOCK, build_vw, 0)

    qt = q_ref[0]
    wq = jnp.concatenate([qt[g * HEAD_DIM:(g + 1) * HEAD_DIM, :] for g in range(GROUP)], axis=1)
    w_plain = jnp.concatenate([wq, jnp.zeros_like(wq)], axis=0)

    sc = _dot(kc_ref[0, 0], w_plain)
    c_ok = (_iota((n_cmp, Q_BLOCK), 0) * CMP_STRIDE + (CMP_BLOCK - 1)
            <= t0 + _iota((n_cmp, Q_BLOCK), 1))
    sc = sc + _tile4(jnp.where(c_ok, 0.0, NEG))
    pc = jnp.exp2(sc - _col_max(sc)) * _tile4(jnp.where(c_ok, 1.0, 0.0))
    lc = _col_sum(pc)
    pc = pc * (1.0 / jnp.where(lc > 0.0, lc, 1.0))
    o_cmp = _dot(vc_ref[0, 0], pc.astype(BF16))

    pcs = pc[:, 0:Q_BLOCK]
    for g in range(1, GROUP):
        pcs = pcs + pc[:, g * Q_BLOCK:(g + 1) * Q_BLOCK]
    pcs_hi = pcs.astype(BF16)
    pcs_lo = (pcs - pcs_hi.astype(F32)).astype(BF16)
    jj = _iota((n_slc, n_cmp), 0)
    ii = _iota((n_slc, n_cmp), 1)
    ov = (jnp.minimum(ii * CMP_STRIDE + CMP_BLOCK, jj * SLC_BLOCK + SLC_BLOCK)
          - jnp.maximum(ii * CMP_STRIDE, jj * SLC_BLOCK))
    ov = (jnp.maximum(ov, 0).astype(F32) * (1.0 / CMP_BLOCK)).astype(BF16)
    imp = _dot(ov, pcs_hi) + _dot(ov, pcs_lo)

    jb = _iota((n_slc, Q_BLOCK), 0)
    tb = _div_pow2(t0 + _iota((n_slc, Q_BLOCK), 1), SLC_BLOCK)
    forced = (jb == 0) | ((jb <= tb) & (jb > tb - N_LOCAL))
    imp = jnp.where(forced, FORCE, imp)
    imp = jnp.where(jb <= tb, imp, NEG)
    imp_ref[...] = imp

    n_groups = n_slc // 8
    mine = [imp[8 * r:8 * r + 8, :] for r in range(n_groups)]
    sub = _iota((8, Q_BLOCK), 0)
    ahead_count = [jnp.zeros((8, Q_BLOCK), F32)] * n_groups
    for j in range(n_slc):
        other = imp_ref[j:j + 1, :]
        for r in range(n_groups):
            if 8 * r > j:
                ahead = other >= mine[r]
            elif 8 * r + 7 <= j:
                ahead = other > mine[r]
            else:
                ahead = (other > mine[r]) | ((other == mine[r]) & (sub > j - 8 * r))
            ahead_count[r] = ahead_count[r] + jnp.where(ahead, 1.0, 0.0)
    sel = (jnp.concatenate(ahead_count, axis=0) < SLC_TOPK) & (imp > 0.5 * NEG)
    sel_bias = jnp.where(sel, 0.0, NEG).astype(BF16)
    w_sel = jnp.concatenate([wq, _tile4(sel_bias)], axis=0)

    w0 = pl.multiple_of(jnp.maximum(t0 - WINDOW, 0), Q_BLOCK)
    sw = _dot(kwe_ref[pl.ds(w0, WIN_KEYS), :], w_plain)
    dist = (t0 + _iota((WIN_KEYS, Q_BLOCK), 1)) - (w0 + _iota((WIN_KEYS, Q_BLOCK), 0))
    sw = sw + _tile4(jnp.where((dist >= 0) & (dist < WINDOW), 0.0, NEG))
    pw = jnp.exp2(sw - _col_max(sw))
    lw = _col_sum(pw)
    pw = pw.astype(BF16)
    w_blk = _div_pow2(w0, Q_BLOCK)
    o_win = _dot(vwt_ref[w_blk], pw[0:Q_BLOCK, :])
    for i in range(1, WIN_KEYS // Q_BLOCK):
        o_win = o_win + _dot(vwt_ref[w_blk + i], pw[i * Q_BLOCK:(i + 1) * Q_BLOCK, :])

    def sub_tile(k0, v_idx, causal, state):
        m, l, acc = state
        s = _dot(kse_ref[pl.ds(k0, SUB_TILE), :], w_sel)
        if causal:
            pos = k0 + _iota((SUB_TILE, Q_BLOCK), 0)
            s = s + _tile4(jnp.where(pos <= t0 + _iota((SUB_TILE, Q_BLOCK), 1), 0.0, NEG))
        m_new = jnp.maximum(m, _col_max(s))
        alpha = jnp.exp2(m - m_new)
        p = jnp.exp2(s - m_new)
        l = alpha * l + _col_sum(p)
        acc = alpha * acc + _dot(vst_ref[v_idx], p.astype(BF16))
        return m_new, l, acc

    subs = KEY_TILE // SUB_TILE

    def key_tile(kt, causal, state):
        for half in range(subs):
            k0 = pl.multiple_of(kt * KEY_TILE + half * SUB_TILE, SUB_TILE)
            state = sub_tile(k0, kt * subs + half, causal, state)
        return state

    state = (jnp.full((1, cols), NEG, F32), jnp.zeros((1, cols), F32),
             jnp.zeros((HEAD_DIM, cols), F32))
    n_full = _div_pow2(t0, KEY_TILE)
    state = lax.fori_loop(0, n_full, lambda kt, st: key_tile(kt, False, st), state)
    _, l_s, o_slc = key_tile(n_full, True, state)

    gates = gate_ref[0]
    inv_ls = 1.0 / l_s
    inv_lw = 1.0 / lw
    pieces = []
    for g in range(GROUP):
        cs = slice(g * Q_BLOCK, (g + 1) * Q_BLOCK)
        row = g * N_GATES
        pieces.append(gates[row:row + 1, :] * o_cmp[:, cs]
                      + gates[row + 1:row + 2, :] * (o_slc[:, cs] * inv_ls[:, cs])
                      + gates[row + 2:row + 3, :] * (o_win[:, cs] * inv_lw[:, cs]))
    o_ref[0] = jnp.concatenate(pieces, axis=0)


def _nsa(q_t, gates_t, kc, vc_t, ks, vs, kw, vw):
    bsz, _, seq = q_t.shape
    n_cmp = kc.shape[2]
    kv_spec = pl.BlockSpec((1, seq, KV_WIDTH), lambda b, h, c: (b, 0, 0))
    return pl.pallas_call(
        functools.partial(_nsa_kernel, seq=seq),
        name="nsa",
        grid=(bsz, N_KV, seq // Q_BLOCK),
        in_specs=[
            pl.BlockSpec((1, GROUP_WIDTH, Q_BLOCK), lambda b, h, c: (b, h, c)),
            pl.BlockSpec((1, GATE_PAD, Q_BLOCK), lambda b, h, c: (b, h, c)),
            pl.BlockSpec((1, 1, n_cmp, LANES), lambda b, h, c: (b, h, 0, 0)),
            pl.BlockSpec((1, 1, HEAD_DIM, n_cmp), lambda b, h, c: (b, h, 0, 0)),
            kv_spec, kv_spec, kv_spec, kv_spec,
        ],
        out_specs=pl.BlockSpec((1, GROUP_WIDTH, Q_BLOCK), lambda b, h, c: (b, h, c)),
        out_shape=jax.ShapeDtypeStruct((bsz, ATTN_WIDTH, seq), F32),
        scratch_shapes=[
            pltpu.VMEM((seq, LANES), BF16),
            pltpu.VMEM((seq, LANES), BF16),
            pltpu.VMEM((seq // SUB_TILE, HEAD_DIM, SUB_TILE), BF16),
            pltpu.VMEM((seq // Q_BLOCK, HEAD_DIM, Q_BLOCK), BF16),
            pltpu.VMEM((seq // SLC_BLOCK, Q_BLOCK), F32),
        ],
        compiler_params=pltpu.CompilerParams(
            dimension_semantics=("parallel", "parallel", "arbitrary"),
            vmem_limit_bytes=VMEM_LIMIT),
    )(q_t, gates_t, kc, vc_t, ks, vs, kw, vw)


def _rope_tables(seq):
    pos = jnp.arange(seq, dtype=F32)
    inv = ROPE_THETA ** (-jnp.arange(0, HEAD_DIM, 2, dtype=F32) / HEAD_DIM)
    ang = pos[:, None] * inv[None, :]
    cos, sin = jnp.cos(ang), jnp.sin(ang)
    reps = LANES // HEAD_DIM
    return (jnp.tile(jnp.concatenate([cos, cos], axis=1), (1, reps)),
            jnp.tile(jnp.concatenate([-sin, sin], axis=1), (1, reps)))


def _head_chunks(tok, bsz, seq):
    x = tok.reshape(bsz, seq // CMP_STRIDE, CMP_STRIDE, N_KV, HEAD_DIM)
    return x.transpose(0, 3, 1, 2, 4).reshape(bsz, N_KV, seq // CMP_STRIDE, CMP_STRIDE * HEAD_DIM)


def _compress_params(w1, w2, pe):
    halves = CMP_BLOCK // CMP_STRIDE
    w1 = w1.reshape(halves, CMP_STRIDE * HEAD_DIM, CMP_HIDDEN).astype(BF16)
    pe = pe.reshape(halves, 1, CMP_STRIDE * HEAD_DIM)
    return pe, w1, w2.astype(BF16)


def kernel(x, ffn1_norm, ffn1_wg, ffn1_wu, ffn1_wd, mix_norm, w_in, conv_w, cmp_k_w1, cmp_k_w2,
           cmp_k_pe, cmp_v_w1, cmp_v_w2, cmp_v_pe, conv_out_norm, attn_out_norm, w_out, ffn2_norm,
           ffn2_wg, ffn2_wu, ffn2_wd, final_norm):
    bsz, seq, d = x.shape
    assert CMP_BLOCK == 2 * CMP_STRIDE and seq % TOKEN_TILE == 0 and seq % KEY_TILE == 0
    assert x.shape[0] * seq % TOKEN_TILE == 0 and ffn1_wg.shape[0] == 1
    xt = x.reshape(bsz * seq, d)
    cos, sin = _rope_tables(seq)

    w = w_in[0]
    w_conv = w[:, :3 * CONV_WIDTH].astype(BF16)
    g0 = 3 * CONV_WIDTH + ATTN_WIDTH + 6 * KV_WIDTH
    wg = w[:, g0:].reshape(d, N_KV, GROUP * N_GATES)
    wg = jnp.pad(wg, ((0, 0), (0, 0), (0, GATE_PAD - GROUP * N_GATES))).reshape(d, N_KV * GATE_PAD)
    w_attn = jnp.concatenate([w[:, 3 * CONV_WIDTH:g0], wg], axis=1).astype(BF16)

    x1 = _ffn1(xt, ffn1_norm, ffn1_wg[0].astype(BF16), ffn1_wu[0].astype(BF16),
               ffn1_wd[0].astype(BF16))
    yc = _conv_proj(x1, mix_norm, w_conv, conv_w[0], conv_out_norm, seq)
    q_t, k_c, v_c, k_s, v_s, k_w, v_w, gates_t = _attn_proj(x1, mix_norm, w_attn, cos, sin, seq)

    pe_k, w1_k, w2_k = _compress_params(cmp_k_w1[0], cmp_k_w2[0], cmp_k_pe[0])
    pe_v, w1_v, w2_v = _compress_params(cmp_v_w1[0], cmp_v_w2[0], cmp_v_pe[0])
    kc = _compress(_head_chunks(k_c, bsz, seq), pe_k, w1_k,
                   jnp.pad(w2_k, ((0, 0), (0, LANES - HEAD_DIM))), transposed=False)
    vc_t = _compress(_head_chunks(v_c, bsz, seq), pe_v, w1_v, w2_v.T, transposed=True)

    per_batch = lambda a: a.reshape(bsz, seq, a.shape[-1])
    ya_t = _nsa(q_t, gates_t, kc, vc_t,
                per_batch(k_s), per_batch(v_s), per_batch(k_w), per_batch(v_w))

    out = _out_ffn2(x1, yc, ya_t, attn_out_norm,
                    w_out[0].astype(BF16), ffn2_norm, ffn2_wg[0].astype(BF16),
                    ffn2_wu[0].astype(BF16), ffn2_wd[0].astype(BF16), final_norm.reshape(1, d))
    return out.reshape(bsz, seq, d)
```

```python
import functools

import jax
import jax.numpy as jnp
from jax import lax
from jax.experimental import pallas as pl
from jax.experimental.pallas import tpu as pltpu

F32 = jnp.float32
BF16 = jnp.bfloat16

D_MODEL = 2048
D_FF = 5632
CONV_WIDTH = 1024
N_HEADS = 16
N_KV = 4
GROUP = N_HEADS // N_KV
HEAD_DIM = 64
ATTN_WIDTH = N_HEADS * HEAD_DIM
KV_WIDTH = N_KV * HEAD_DIM
CMP_BLOCK = 32
CMP_STRIDE = 16
CMP_HIDDEN = 256
SLC_BLOCK = 64
SLC_TOPK = 16
N_LOCAL = 2
WINDOW = 512
Q_BLOCK = 128
N_GATES = 3
ROPE_THETA = 10000.0
EPS = 1e-6
NEG = -1e30
FORCE = 1e9

LANES = 128
GROUP_WIDTH = GROUP * HEAD_DIM
GATE_PAD = LANES
TOKEN_TILE = 512
FF_TILE = 512
KEY_TILE = 512
WIN_KEYS = WINDOW + Q_BLOCK
V_ROWS = HEAD_DIM + 16
VMEM_LIMIT = 56 * 1024 * 1024
Q_SCALE = HEAD_DIM ** -0.5 * 1.4426950408889634


def _rms(x):
    return x * lax.rsqrt(jnp.mean(x * x, axis=-1, keepdims=True) + EPS)


def _dot(a, b):
    return jnp.dot(a, b, preferred_element_type=F32)


def _dot_nt(a, b):
    return lax.dot_general(a, b, (((1,), (1,)), ((), ())), preferred_element_type=F32)


def _div_pow2(x, n):
    assert n & (n - 1) == 0
    return jnp.right_shift(x, n.bit_length() - 1)


def _iota(shape, dim):
    return lax.broadcasted_iota(jnp.int32, shape, dim)


def _resident(shape):
    return pl.BlockSpec(shape, lambda *_: (0,) * len(shape), pipeline_mode=pl.Buffered(1))


def _ffn_accumulate(h_ref, wg_ref, wu_ref, wd_ref, acc_ref):
    h = h_ref[...]
    a = _dot(h, wg_ref[...])
    u = _dot(h, wu_ref[...])
    act = (jax.nn.silu(a) * u).astype(BF16)
    acc_ref[...] += _dot(act, wd_ref[...])


def _ffn1_kernel(x_ref, g_ref, wg_ref, wu_ref, wd_ref, o_ref, h_ref, acc_ref):
    j = pl.program_id(1)

    @pl.when(j == 0)
    def _():
        h_ref[...] = (_rms(x_ref[...]) * g_ref[...]).astype(BF16)
        acc_ref[...] = jnp.zeros_like(acc_ref)

    _ffn_accumulate(h_ref, wg_ref, wu_ref, wd_ref, acc_ref)

    @pl.when(j == pl.num_programs(1) - 1)
    def _():
        o_ref[...] = x_ref[...] + 0.5 * acc_ref[...]


def _ffn1(x, g, wg, wu, wd):
    t, d = x.shape
    f = wg.shape[1]
    return pl.pallas_call(
        _ffn1_kernel,
        name="ffn1",
        grid=(t // TOKEN_TILE, f // FF_TILE),
        in_specs=[
            pl.BlockSpec((TOKEN_TILE, d), lambda i, j: (i, 0)),
            pl.BlockSpec((1, d), lambda i, j: (0, 0)),
            pl.BlockSpec((d, FF_TILE), lambda i, j: (0, j)),
            pl.BlockSpec((d, FF_TILE), lambda i, j: (0, j)),
            pl.BlockSpec((FF_TILE, d), lambda i, j: (j, 0)),
        ],
        out_specs=pl.BlockSpec((TOKEN_TILE, d), lambda i, j: (i, 0)),
        out_shape=jax.ShapeDtypeStruct((t, d), F32),
        scratch_shapes=[pltpu.VMEM((TOKEN_TILE, d), BF16), pltpu.VMEM((TOKEN_TILE, d), F32)],
        compiler_params=pltpu.CompilerParams(
            dimension_semantics=("parallel", "arbitrary"), vmem_limit_bytes=VMEM_LIMIT),
    )(x, g, wg, wu, wd)


def _out_ffn2_kernel(x_ref, yc_ref, ya_ref, ga_ref, wo_ref, g_ref, wg_ref, wu_ref, wd_ref,
                     fin_ref, o_ref, h_ref, acc_ref):
    j = pl.program_id(1)

    @pl.when(j == 0)
    def _():
        ya = (_rms(ya_ref[0].T) * ga_ref[...]).astype(BF16)
        mix = _dot(yc_ref[...], wo_ref[0:CONV_WIDTH, :]) + _dot(ya, wo_ref[CONV_WIDTH:, :])
        x2 = x_ref[...] + mix
        o_ref[...] = x2
        h_ref[...] = (_rms(x2) * g_ref[...]).astype(BF16)
        acc_ref[...] = jnp.zeros_like(acc_ref)

    _ffn_accumulate(h_ref, wg_ref, wu_ref, wd_ref, acc_ref)

    @pl.when(j == pl.num_programs(1) - 1)
    def _():
        o_ref[...] = _rms(o_ref[...] + 0.5 * acc_ref[...]) * fin_ref[...]


def _out_ffn2(x1, yc, ya_t, ga, wo, g, wg, wu, wd, fin):
    t, d = x1.shape
    f = wg.shape[1]
    tm = TOKEN_TILE
    tps = ya_t.shape[2] // tm
    return pl.pallas_call(
        _out_ffn2_kernel,
        name="out_ffn2",
        grid=(t // tm, f // FF_TILE),
        in_specs=[
            pl.BlockSpec((tm, d), lambda i, j: (i, 0)),
            pl.BlockSpec((tm, CONV_WIDTH), lambda i, j: (i, 0)),
            pl.BlockSpec((1, ATTN_WIDTH, tm), lambda i, j: (i // tps, 0, i % tps)),
            pl.BlockSpec((1, ATTN_WIDTH), lambda i, j: (0, 0)),
            _resident(wo.shape),
            pl.BlockSpec((1, d), lambda i, j: (0, 0)),
            pl.BlockSpec((d, FF_TILE), lambda i, j: (0, j)),
            pl.BlockSpec((d, FF_TILE), lambda i, j: (0, j)),
            pl.BlockSpec((FF_TILE, d), lambda i, j: (j, 0)),
            pl.BlockSpec((1, d), lambda i, j: (0, 0)),
        ],
        out_specs=pl.BlockSpec((tm, d), lambda i, j: (i, 0)),
        out_shape=jax.ShapeDtypeStruct((t, d), F32),
        scratch_shapes=[pltpu.VMEM((tm, d), BF16), pltpu.VMEM((tm, d), F32)],
        compiler_params=pltpu.CompilerParams(
            dimension_semantics=("parallel", "arbitrary"), vmem_limit_bytes=VMEM_LIMIT),
    )(x1, yc, ya_t, ga, wo, g, wg, wu, wd, fin)


def _conv_proj_kernel(x_ref, g_ref, w_ref, cw_ref, gc_ref, o_ref, carry_ref, *, tiles_per_seq):
    i = pl.program_id(0)

    @pl.when(i % tiles_per_seq == 0)
    def _():
        carry_ref[...] = jnp.zeros_like(carry_ref)

    h = (_rms(x_ref[...]) * g_ref[...]).astype(BF16)
    c_h = _dot(h, w_ref[:, 0:CONV_WIDTH])
    c_b = _dot(h, w_ref[:, CONV_WIDTH:2 * CONV_WIDTH])
    c_c = _dot(h, w_ref[:, 2 * CONV_WIDTH:3 * CONV_WIDTH])
    u = c_c * c_h
    tm = u.shape[0]
    row = lax.broadcasted_iota(jnp.int32, u.shape, 0)
    prev1 = carry_ref[7:8, :]
    prev2 = carry_ref[6:7, :]
    u1 = jnp.where(row == 0, prev1, pltpu.roll(u, 1, 0))
    u2 = jnp.where(row == 0, prev2, jnp.where(row == 1, prev1, pltpu.roll(u, 2, 0)))
    carry_ref[...] = u[tm - 8:tm, :]
    y = c_b * (cw_ref[0:1, :] * u2 + cw_ref[1:2, :] * u1 + cw_ref[2:3, :] * u)
    o_ref[...] = (_rms(y) * gc_ref[...]).astype(BF16)


def _conv_proj(x1, g, w_conv, conv_w, gc, seq):
    t, d = x1.shape
    tm = TOKEN_TILE
    return pl.pallas_call(
        functools.partial(_conv_proj_kernel, tiles_per_seq=seq // tm),
        name="conv_proj",
        grid=(t // tm,),
        in_specs=[
            pl.BlockSpec((tm, d), lambda i: (i, 0)),
            pl.BlockSpec((1, d), lambda i: (0, 0)),
            _resident(w_conv.shape),
            pl.BlockSpec(conv_w.shape, lambda i: (0, 0)),
            pl.BlockSpec((1, CONV_WIDTH), lambda i: (0, 0)),
        ],
        out_specs=pl.BlockSpec((tm, CONV_WIDTH), lambda i: (i, 0)),
        out_shape=jax.ShapeDtypeStruct((t, CONV_WIDTH), BF16),
        scratch_shapes=[pltpu.VMEM((8, CONV_WIDTH), F32)],
        compiler_params=pltpu.CompilerParams(
            dimension_semantics=("arbitrary",), vmem_limit_bytes=VMEM_LIMIT),
    )(x1, g, w_conv, conv_w, gc)


def _rope(x, cos, sin_signed, first_half):
    outs = []
    for k in range(x.shape[1] // LANES):
        xc = x[:, k * LANES:(k + 1) * LANES]
        partner = jnp.where(first_half, pltpu.roll(xc, LANES - HEAD_DIM // 2, 1),
                            pltpu.roll(xc, HEAD_DIM // 2, 1))
        outs.append(xc * cos + partner * sin_signed)
    return outs


_Q0 = 0
_KC0 = ATTN_WIDTH
_VC0 = _KC0 + KV_WIDTH
_KS0 = _VC0 + KV_WIDTH
_VS0 = _KS0 + KV_WIDTH
_KW0 = _VS0 + KV_WIDTH
_VW0 = _KW0 + KV_WIDTH
_G0 = _VW0 + KV_WIDTH
ATTN_PROJ_WIDTH = _G0 + N_KV * GATE_PAD


def _attn_proj_kernel(x_ref, g_ref, w_ref, cos_ref, sin_ref,
                      q_ref, kc_ref, vc_ref, ks_ref, vs_ref, kw_ref, vw_ref, gate_ref):
    h = (_rms(x_ref[...]) * g_ref[...]).astype(BF16)
    cos = cos_ref[...]
    sin = sin_ref[...]
    lane = lax.broadcasted_iota(jnp.int32, cos.shape, 1)
    first_half = (lane & (HEAD_DIM - 1)) < HEAD_DIM // 2

    def proj(c0, width):
        return _dot(h, w_ref[:, c0:c0 + width])

    def rope_to(ref, c0, width, dtype):
        for k, piece in enumerate(_rope(proj(c0, width), cos, sin, first_half)):
            ref[:, k * LANES:(k + 1) * LANES] = piece.astype(dtype)

    for k, piece in enumerate(_rope(proj(_Q0, ATTN_WIDTH), cos, sin, first_half)):
        q_ref[0, k * LANES:(k + 1) * LANES, :] = (piece * Q_SCALE).T.astype(BF16)
    rope_to(kc_ref, _KC0, KV_WIDTH, F32)
    vc_ref[...] = proj(_VC0, KV_WIDTH)
    rope_to(ks_ref, _KS0, KV_WIDTH, BF16)
    vs_ref[...] = proj(_VS0, KV_WIDTH).astype(BF16)
    rope_to(kw_ref, _KW0, KV_WIDTH, BF16)
    vw_ref[...] = proj(_VW0, KV_WIDTH).astype(BF16)
    gate_ref[0] = jax.nn.sigmoid(proj(_G0, N_KV * GATE_PAD)).T


def _attn_proj(x1, g, w_attn, cos, sin, seq):
    t, d = x1.shape
    tm = TOKEN_TILE
    tps = seq // tm
    bsz = t // seq
    row = lambda w: pl.BlockSpec((tm, w), lambda i: (i, 0))
    col = lambda w: pl.BlockSpec((1, w, tm), lambda i: (i // tps, 0, i % tps))
    kv_dtypes = [F32, F32, BF16, BF16, BF16, BF16]
    out_specs = [col(ATTN_WIDTH)] + [row(KV_WIDTH)] * 6 + [col(N_KV * GATE_PAD)]
    out_shape = ([jax.ShapeDtypeStruct((bsz, ATTN_WIDTH, seq), BF16)]
                 + [jax.ShapeDtypeStruct((t, KV_WIDTH), dt) for dt in kv_dtypes]
                 + [jax.ShapeDtypeStruct((bsz, N_KV * GATE_PAD, seq), F32)])
    return pl.pallas_call(
        _attn_proj_kernel,
        name="attn_proj",
        grid=(t // tm,),
        in_specs=[
            row(d),
            pl.BlockSpec((1, d), lambda i: (0, 0)),
            _resident(w_attn.shape),
            pl.BlockSpec((tm, LANES), lambda i: (i % tps, 0)),
            pl.BlockSpec((tm, LANES), lambda i: (i % tps, 0)),
        ],
        out_specs=out_specs,
        out_shape=out_shape,
        compiler_params=pltpu.CompilerParams(
            dimension_semantics=("parallel",), vmem_limit_bytes=VMEM_LIMIT),
    )(x1, g, w_attn, cos, sin)


def _compress_kernel(c_ref, pe_ref, w1_ref, w2_ref, o_ref, *, transposed):
    c = c_ref[0, 0]
    a = _dot((c + pe_ref[0]).astype(BF16), w1_ref[0])
    b = _dot((c + pe_ref[1]).astype(BF16), w1_ref[1])
    n = a.shape[0]
    hid = jax.nn.gelu(a + pltpu.roll(b, n - 1, 0)).astype(BF16)
    if transposed:
        o_ref[0, 0] = _dot_nt(w2_ref[...], hid).astype(BF16)
    else:
        o_ref[0, 0] = _dot(hid, w2_ref[...]).astype(BF16)


def _compress(chunks, pe, w1, w2, transposed):
    bsz, nkv, n, width = chunks.shape
    out_block = (1, 1, HEAD_DIM, n) if transposed else (1, 1, n, LANES)
    return pl.pallas_call(
        functools.partial(_compress_kernel, transposed=transposed),
        name="compress_v" if transposed else "compress_k",
        grid=(bsz, nkv),
        in_specs=[
            pl.BlockSpec((1, 1, n, width), lambda b, h: (b, h, 0, 0)),
            pl.BlockSpec(pe.shape, lambda b, h: (0, 0, 0)),
            pl.BlockSpec(w1.shape, lambda b, h: (0, 0, 0)),
            pl.BlockSpec(w2.shape, lambda b, h: (0, 0)),
        ],
        out_specs=pl.BlockSpec(out_block, lambda b, h: (b, h, 0, 0)),
        out_shape=jax.ShapeDtypeStruct((bsz, nkv) + out_block[2:], BF16),
        compiler_params=pltpu.CompilerParams(
            dimension_semantics=("parallel", "parallel"), vmem_limit_bytes=VMEM_LIMIT),
    )(chunks, pe, w1, w2)


def _tile4(x):
    return jnp.concatenate([x] * GROUP, axis=1)


def _col_max(x):
    return jnp.max(x, axis=0, keepdims=True)


def _col_sum(x):
    return jnp.sum(x, axis=0, keepdims=True)


def _nsa_kernel(q_ref, gate_ref, kc_ref, vc_ref, ks_ref, vs_ref, kw_ref, vw_ref, o_ref,
                kse_ref, kwe_ref, vst_ref, vsc_ref, vwt_ref, imp_ref, *, seq):
    h = pl.program_id(1)
    c = pl.program_id(2)
    t0 = c * Q_BLOCK
    n_cmp = kc_ref.shape[2]
    n_slc = seq // SLC_BLOCK
    cols = GROUP * Q_BLOCK
    n_chunks = seq // Q_BLOCK
    assert n_slc + HEAD_DIM == LANES and n_slc % 8 == 0

    @pl.when(c == 0)
    def _():
        r = _iota((KV_WIDTH, 2 * LANES), 0)
        ln = _iota((KV_WIDTH, 2 * LANES), 1)
        pick = ((r == h * HEAD_DIM + ln) & (ln < HEAD_DIM)).astype(BF16)
        pick_t = (_iota((V_ROWS, KV_WIDTH), 1)
                  == h * HEAD_DIM + _iota((V_ROWS, KV_WIDTH), 0)).astype(BF16)
        pick_t = jnp.where(_iota((V_ROWS, KV_WIDTH), 0) < HEAD_DIM, pick_t, jnp.zeros_like(pick_t))

        def ones_row(width):
            return jnp.where(_iota((V_ROWS, width), 0) == HEAD_DIM, 1.0, 0.0)

        def build_k(i, _):
            sl = pl.ds(pl.multiple_of(i * KEY_TILE, KEY_TILE), KEY_TILE)
            pos = i * KEY_TILE + _iota((KEY_TILE, 2 * LANES), 0)
            lane = _iota((KEY_TILE, 2 * LANES), 1)
            in_chunk = lane - LANES == (pos & (Q_BLOCK - 1))
            in_block = lane - HEAD_DIM == _div_pow2(pos, SLC_BLOCK)
            kse_ref[sl, :] = (_dot(ks_ref[0, sl, :], pick)
                              + jnp.where(in_chunk | in_block, 1.0, 0.0)).astype(BF16)
            kwe_ref[sl, :] = (_dot(kw_ref[0, sl, :], pick)
                              + jnp.where(in_chunk, 1.0, 0.0)).astype(BF16)
            return 0

        def build_vs(i, _):
            sl = pl.ds(pl.multiple_of(i * KEY_TILE, KEY_TILE), KEY_TILE)
            vst_ref[i] = (_dot_nt(pick_t, vs_ref[0, sl, :]) + ones_row(KEY_TILE)).astype(BF16)
            return 0

        def build_vw(i, _):
            sl = pl.ds(pl.multiple_of(i * Q_BLOCK, Q_BLOCK), Q_BLOCK)
            vwt_ref[i] = (_dot_nt(pick_t, vw_ref[0, sl, :]) + ones_row(Q_BLOCK)).astype(BF16)
            vsc_ref[i] = (_dot_nt(pick_t, vs_ref[0, sl, :]) + ones_row(Q_BLOCK)).astype(BF16)
            return 0

        lax.fori_loop(0, seq // KEY_TILE, build_k, 0)
        lax.fori_loop(0, seq // KEY_TILE, build_vs, 0)
        lax.fori_loop(0, n_chunks, build_vw, 0)
        kwe_ref[pl.ds(seq, Q_BLOCK), :] = jnp.where(
            _iota((Q_BLOCK, 2 * LANES), 1) == HEAD_DIM, 1.0, 0.0).astype(BF16)
        vwt_ref[n_chunks] = jnp.zeros((V_ROWS, Q_BLOCK), BF16)

    qt = q_ref[0]
    wq = jnp.concatenate([qt[g * HEAD_DIM:(g + 1) * HEAD_DIM, :] for g in range(GROUP)], axis=1)
    w_plain = jnp.concatenate([wq, jnp.zeros_like(wq)], axis=0)

    sc = _dot(kc_ref[0, 0], w_plain)
    c_ok = (_iota((n_cmp, Q_BLOCK), 0) * CMP_STRIDE + (CMP_BLOCK - 1)
            <= t0 + _iota((n_cmp, Q_BLOCK), 1))
    sc = sc + _tile4(jnp.where(c_ok, 0.0, NEG))
    pc = jnp.exp2(sc - _col_max(sc))
    any_ok = t0 + (_iota((1, cols), 1) & (Q_BLOCK - 1)) >= CMP_BLOCK - 1
    pc = pc * jnp.where(any_ok, 1.0 / _col_sum(pc), 0.0)
    o_cmp = _dot(vc_ref[0, 0], pc.astype(BF16))

    pcs = pc[:, 0:Q_BLOCK]
    for g in range(1, GROUP):
        pcs = pcs + pc[:, g * Q_BLOCK:(g + 1) * Q_BLOCK]
    pcs_hi = pcs.astype(BF16)
    pcs_lo = (pcs - pcs_hi.astype(F32)).astype(BF16)
    jj = _iota((n_slc, n_cmp), 0)
    ii = _iota((n_slc, n_cmp), 1)
    ov = (jnp.minimum(ii * CMP_STRIDE + CMP_BLOCK, jj * SLC_BLOCK + SLC_BLOCK)
          - jnp.maximum(ii * CMP_STRIDE, jj * SLC_BLOCK))
    ov = (jnp.maximum(ov, 0).astype(F32) * (1.0 / CMP_BLOCK)).astype(BF16)
    imp = _dot(ov, pcs_hi) + _dot(ov, pcs_lo)

    jb = _iota((n_slc, Q_BLOCK), 0)
    tb = _div_pow2(t0 + _iota((n_slc, Q_BLOCK), 1), SLC_BLOCK)
    forced = (jb == 0) | ((jb <= tb) & (jb > tb - N_LOCAL))
    imp = jnp.where(forced, FORCE, imp)
    imp = jnp.where(jb <= tb, imp, NEG)
    imp_ref[...] = imp

    n_groups = n_slc // 8
    mine = [imp[8 * r:8 * r + 8, :] for r in range(n_groups)]
    sub = _iota((8, Q_BLOCK), 0)
    ahead_count = [jnp.zeros((8, Q_BLOCK), F32)] * n_groups
    for j in range(n_slc):
        other = imp_ref[j:j + 1, :]
        for r in range(n_groups):
            if 8 * r > j:
                ahead = other >= mine[r]
            elif 8 * r + 7 <= j:
                ahead = other > mine[r]
            else:
                ahead = (other > mine[r]) | ((other == mine[r]) & (sub > j - 8 * r))
            ahead_count[r] = ahead_count[r] + jnp.where(ahead, 1.0, 0.0)
    sel = (jnp.concatenate(ahead_count, axis=0) < SLC_TOPK) & (imp > 0.5 * NEG)
    sel_bias = jnp.where(sel & (jb * SLC_BLOCK < t0), 0.0, NEG).astype(BF16)
    w_sel = jnp.concatenate([wq, _tile4(sel_bias)], axis=0)

    kpos = _iota((Q_BLOCK, Q_BLOCK), 0)
    tpos = _iota((Q_BLOCK, Q_BLOCK), 1)
    causal = _tile4(jnp.where(kpos <= tpos, 0.0, NEG).astype(BF16))
    recent = _tile4(jnp.where(kpos > tpos, 0.0, NEG).astype(BF16))
    pad = jnp.zeros((HEAD_DIM, cols), BF16)
    w_causal = jnp.concatenate([wq, pad, causal], axis=0)
    dummy_row = jnp.where(_iota((HEAD_DIM, cols), 0) == 0, NEG, 0.0).astype(BF16)
    w_mid = jnp.concatenate([wq, dummy_row], axis=0)
    w_recent = jnp.concatenate([wq, dummy_row, recent], axis=0)

    def soft(scores, m):
        return [jnp.exp2(s - m).astype(BF16) for s in scores]

    n_win = WIN_KEYS // Q_BLOCK
    first = c - (n_win - 1)
    chunk = [jnp.where(first + i >= 0, first + i, n_chunks) for i in range(n_win)]
    rows = [pl.ds(pl.multiple_of(ch * Q_BLOCK, Q_BLOCK), Q_BLOCK) for ch in chunk]
    sw = [_dot(kwe_ref[rows[0], :], w_recent)]
    sw += [_dot(kwe_ref[rows[i], 0:LANES], w_mid) for i in range(1, n_win - 1)]
    sw += [_dot(kwe_ref[rows[n_win - 1], :], w_causal)]
    m_w = _col_max(sw[0])
    for s in sw[1:]:
        m_w = jnp.maximum(m_w, _col_max(s))
    o_win = None
    for ch, p in zip(chunk, soft(sw, m_w)):
        part = _dot(vwt_ref[ch], p)
        o_win = part if o_win is None else o_win + part

    s_d = _dot(kse_ref[pl.ds(pl.multiple_of(t0, Q_BLOCK), Q_BLOCK), :], w_causal)
    m_d = _col_max(s_d)
    acc_d = _dot(vsc_ref[c], soft([s_d], m_d)[0])

    def scores(kt):
        k0 = pl.multiple_of(kt * KEY_TILE, KEY_TILE)
        return _dot(kse_ref[pl.ds(k0, KEY_TILE), 0:LANES], w_sel)

    def update(s, kt, state):
        m, acc = state
        m_new = jnp.maximum(m, _col_max(s))
        p = jnp.exp2(s - m_new).astype(BF16)
        return m_new, jnp.exp2(m - m_new) * acc + _dot(vst_ref[kt], p)

    def tile_pair(i, state):
        s_a, s_b = scores(2 * i), scores(2 * i + 1)
        return update(s_b, 2 * i + 1, update(s_a, 2 * i, state))

    n_tiles = _div_pow2(t0 + KEY_TILE - 1, KEY_TILE)
    _, o_slc = lax.fori_loop(0, _div_pow2(n_tiles + 1, 2), tile_pair, (m_d, acc_d))

    gates = gate_ref[0]
    inv_ls = 1.0 / o_slc[HEAD_DIM:HEAD_DIM + 1, :]
    inv_lw = 1.0 / o_win[HEAD_DIM:HEAD_DIM + 1, :]
    pieces = []
    for g in range(GROUP):
        cs = slice(g * Q_BLOCK, (g + 1) * Q_BLOCK)
        row = g * N_GATES
        pieces.append(gates[row:row + 1, :] * o_cmp[:, cs]
                      + gates[row + 1:row + 2, :] * (o_slc[0:HEAD_DIM, cs] * inv_ls[:, cs])
                      + gates[row + 2:row + 3, :] * (o_win[0:HEAD_DIM, cs] * inv_lw[:, cs]))
    o_ref[0] = jnp.concatenate(pieces, axis=0)


def _nsa(q_t, gates_t, kc, vc_t, ks, vs, kw, vw):
    bsz, _, seq = q_t.shape
    n_cmp = kc.shape[2]
    kv_spec = pl.BlockSpec((1, seq, KV_WIDTH), lambda b, h, c: (b, 0, 0))
    return pl.pallas_call(
        functools.partial(_nsa_kernel, seq=seq),
        name="nsa",
        grid=(bsz, N_KV, seq // Q_BLOCK),
        in_specs=[
            pl.BlockSpec((1, GROUP_WIDTH, Q_BLOCK), lambda b, h, c: (b, h, c)),
            pl.BlockSpec((1, GATE_PAD, Q_BLOCK), lambda b, h, c: (b, h, c)),
            pl.BlockSpec((1, 1, n_cmp, LANES), lambda b, h, c: (b, h, 0, 0)),
            pl.BlockSpec((1, 1, HEAD_DIM, n_cmp), lambda b, h, c: (b, h, 0, 0)),
            kv_spec, kv_spec, kv_spec, kv_spec,
        ],
        out_specs=pl.BlockSpec((1, GROUP_WIDTH, Q_BLOCK), lambda b, h, c: (b, h, c)),
        out_shape=jax.ShapeDtypeStruct((bsz, ATTN_WIDTH, seq), F32),
        scratch_shapes=[
            pltpu.VMEM((seq, 2 * LANES), BF16),
            pltpu.VMEM((seq + Q_BLOCK, 2 * LANES), BF16),
            pltpu.VMEM((seq // KEY_TILE, V_ROWS, KEY_TILE), BF16),
            pltpu.VMEM((seq // Q_BLOCK, V_ROWS, Q_BLOCK), BF16),
            pltpu.VMEM((seq // Q_BLOCK + 1, V_ROWS, Q_BLOCK), BF16),
            pltpu.VMEM((seq // SLC_BLOCK, Q_BLOCK), F32),
        ],
        compiler_params=pltpu.CompilerParams(
            dimension_semantics=("parallel", "parallel", "arbitrary"),
            vmem_limit_bytes=VMEM_LIMIT),
    )(q_t, gates_t, kc, vc_t, ks, vs, kw, vw)


def _rope_tables(seq):
    pos = jnp.arange(seq, dtype=F32)
    inv = ROPE_THETA ** (-jnp.arange(0, HEAD_DIM, 2, dtype=F32) / HEAD_DIM)
    ang = pos[:, None] * inv[None, :]
    cos, sin = jnp.cos(ang), jnp.sin(ang)
    reps = LANES // HEAD_DIM
    return (jnp.tile(jnp.concatenate([cos, cos], axis=1), (1, reps)),
            jnp.tile(jnp.concatenate([-sin, sin], axis=1), (1, reps)))


def _head_chunks(tok, bsz, seq):
    x = tok.reshape(bsz, seq // CMP_STRIDE, CMP_STRIDE, N_KV, HEAD_DIM)
    return x.transpose(0, 3, 1, 2, 4).reshape(bsz, N_KV, seq // CMP_STRIDE, CMP_STRIDE * HEAD_DIM)


def _compress_params(w1, w2, pe):
    halves = CMP_BLOCK // CMP_STRIDE
    w1 = w1.reshape(halves, CMP_STRIDE * HEAD_DIM, CMP_HIDDEN).astype(BF16)
    pe = pe.reshape(halves, 1, CMP_STRIDE * HEAD_DIM)
    return pe, w1, w2.astype(BF16)


def kernel(x, ffn1_norm, ffn1_wg, ffn1_wu, ffn1_wd, mix_norm, w_in, conv_w, cmp_k_w1, cmp_k_w2,
           cmp_k_pe, cmp_v_w1, cmp_v_w2, cmp_v_pe, conv_out_norm, attn_out_norm, w_out, ffn2_norm,
           ffn2_wg, ffn2_wu, ffn2_wd, final_norm):
    bsz, seq, d = x.shape
    assert CMP_BLOCK == 2 * CMP_STRIDE and seq % TOKEN_TILE == 0 and seq % KEY_TILE == 0
    assert x.shape[0] * seq % TOKEN_TILE == 0 and ffn1_wg.shape[0] == 1
    xt = x.reshape(bsz * seq, d)
    cos, sin = _rope_tables(seq)

    w = w_in[0]
    w_conv = w[:, :3 * CONV_WIDTH].astype(BF16)
    g0 = 3 * CONV_WIDTH + ATTN_WIDTH + 6 * KV_WIDTH
    wg = w[:, g0:].reshape(d, N_KV, GROUP * N_GATES)
    wg = jnp.pad(wg, ((0, 0), (0, 0), (0, GATE_PAD - GROUP * N_GATES))).reshape(d, N_KV * GATE_PAD)
    w_attn = jnp.concatenate([w[:, 3 * CONV_WIDTH:g0], wg], axis=1).astype(BF16)

    x1 = _ffn1(xt, ffn1_norm, ffn1_wg[0].astype(BF16), ffn1_wu[0].astype(BF16),
               ffn1_wd[0].astype(BF16))
    yc = _conv_proj(x1, mix_norm, w_conv, conv_w[0], conv_out_norm, seq)
    q_t, k_c, v_c, k_s, v_s, k_w, v_w, gates_t = _attn_proj(x1, mix_norm, w_attn, cos, sin, seq)

    pe_k, w1_k, w2_k = _compress_params(cmp_k_w1[0], cmp_k_w2[0], cmp_k_pe[0])
    pe_v, w1_v, w2_v = _compress_params(cmp_v_w1[0], cmp_v_w2[0], cmp_v_pe[0])
    kc = _compress(_head_chunks(k_c, bsz, seq), pe_k, w1_k,
                   jnp.pad(w2_k, ((0, 0), (0, LANES - HEAD_DIM))), transposed=False)
    vc_t = _compress(_head_chunks(v_c, bsz, seq), pe_v, w1_v, w2_v.T, transposed=True)

    per_batch = lambda a: a.reshape(bsz, seq, a.shape[-1])
    ya_t = _nsa(q_t, gates_t, kc, vc_t,
                per_batch(k_s), per_batch(v_s), per_batch(k_w), per_batch(v_w))

    out = _out_ffn2(x1, yc, ya_t, attn_out_norm,
                    w_out[0].astype(BF16), ffn2_norm, ffn2_wg[0].astype(BF16),
                    ffn2_wu[0].astype(BF16), ffn2_wd[0].astype(BF16), final_norm.reshape(1, d))
    return out.reshape(bsz, seq, d)
```

```python
import functools

import jax
import jax.numpy as jnp
from jax import lax
from jax.experimental import pallas as pl
from jax.experimental.pallas import tpu as pltpu

F32 = jnp.float32
BF16 = jnp.bfloat16

D_MODEL = 2048
D_FF = 5632
CONV_WIDTH = 1024
N_HEADS = 16
N_KV = 4
GROUP = N_HEADS // N_KV
HEAD_DIM = 64
ATTN_WIDTH = N_HEADS * HEAD_DIM
KV_WIDTH = N_KV * HEAD_DIM
CMP_BLOCK = 32
CMP_STRIDE = 16
CMP_HIDDEN = 256
SLC_BLOCK = 64
SLC_TOPK = 16
N_LOCAL = 2
WINDOW = 512
Q_BLOCK = 128
N_GATES = 3
ROPE_THETA = 10000.0
EPS = 1e-6
NEG = -1e30
FORCE = 1e9

LANES = 128
GROUP_WIDTH = GROUP * HEAD_DIM
GATE_PAD = LANES
TOKEN_TILE = 512
FF_TILE = 512
KEY_TILE = 512
WIN_KEYS = WINDOW + Q_BLOCK
ONES_ROWS = 16
HEADS_PER_STEP = 2
VMEM_LIMIT = 56 * 1024 * 1024
Q_SCALE = HEAD_DIM ** -0.5 * 1.4426950408889634


def _rms(x):
    return x * lax.rsqrt(jnp.mean(x * x, axis=-1, keepdims=True) + EPS)


def _dot(a, b):
    return jnp.dot(a, b, preferred_element_type=F32)


def _dot_nt(a, b):
    return lax.dot_general(a, b, (((1,), (1,)), ((), ())), preferred_element_type=F32)


def _div_pow2(x, n):
    assert n & (n - 1) == 0
    return jnp.right_shift(x, n.bit_length() - 1)


def _iota(shape, dim):
    return lax.broadcasted_iota(jnp.int32, shape, dim)


def _resident(shape):
    return pl.BlockSpec(shape, lambda *_: (0,) * len(shape), pipeline_mode=pl.Buffered(1))


def _ffn_accumulate(h_ref, wg_ref, wu_ref, wd_ref, acc_ref):
    h = h_ref[...]
    a = _dot(h, wg_ref[...])
    u = _dot(h, wu_ref[...])
    act = (jax.nn.silu(a) * u).astype(BF16)
    acc_ref[...] += _dot(act, wd_ref[...])


def _ffn1_kernel(x_ref, g_ref, wg_ref, wu_ref, wd_ref, o_ref, h_ref, acc_ref):
    j = pl.program_id(1)

    @pl.when(j == 0)
    def _():
        h_ref[...] = (_rms(x_ref[...]) * g_ref[...]).astype(BF16)
        acc_ref[...] = jnp.zeros_like(acc_ref)

    _ffn_accumulate(h_ref, wg_ref, wu_ref, wd_ref, acc_ref)

    @pl.when(j == pl.num_programs(1) - 1)
    def _():
        o_ref[...] = x_ref[...] + 0.5 * acc_ref[...]


def _ffn1(x, g, wg, wu, wd):
    t, d = x.shape
    f = wg.shape[1]
    return pl.pallas_call(
        _ffn1_kernel,
        name="ffn1",
        grid=(t // TOKEN_TILE, f // FF_TILE),
        in_specs=[
            pl.BlockSpec((TOKEN_TILE, d), lambda i, j: (i, 0)),
            pl.BlockSpec((1, d), lambda i, j: (0, 0)),
            pl.BlockSpec((d, FF_TILE), lambda i, j: (0, j)),
            pl.BlockSpec((d, FF_TILE), lambda i, j: (0, j)),
            pl.BlockSpec((FF_TILE, d), lambda i, j: (j, 0)),
        ],
        out_specs=pl.BlockSpec((TOKEN_TILE, d), lambda i, j: (i, 0)),
        out_shape=jax.ShapeDtypeStruct((t, d), F32),
        scratch_shapes=[pltpu.VMEM((TOKEN_TILE, d), BF16), pltpu.VMEM((TOKEN_TILE, d), F32)],
        compiler_params=pltpu.CompilerParams(
            dimension_semantics=("parallel", "arbitrary"), vmem_limit_bytes=VMEM_LIMIT),
    )(x, g, wg, wu, wd)


def _out_ffn2_kernel(x_ref, yc_ref, ya_ref, ga_ref, wo_ref, g_ref, wg_ref, wu_ref, wd_ref,
                     fin_ref, o_ref, h_ref, acc_ref):
    j = pl.program_id(1)

    @pl.when(j == 0)
    def _():
        ya = (_rms(ya_ref[0].T) * ga_ref[...]).astype(BF16)
        mix = _dot(yc_ref[...], wo_ref[0:CONV_WIDTH, :]) + _dot(ya, wo_ref[CONV_WIDTH:, :])
        x2 = x_ref[...] + mix
        o_ref[...] = x2
        h_ref[...] = (_rms(x2) * g_ref[...]).astype(BF16)
        acc_ref[...] = jnp.zeros_like(acc_ref)

    _ffn_accumulate(h_ref, wg_ref, wu_ref, wd_ref, acc_ref)

    @pl.when(j == pl.num_programs(1) - 1)
    def _():
        o_ref[...] = _rms(o_ref[...] + 0.5 * acc_ref[...]) * fin_ref[...]


def _out_ffn2(x1, yc, ya_t, ga, wo, g, wg, wu, wd, fin):
    t, d = x1.shape
    f = wg.shape[1]
    tm = TOKEN_TILE
    tps = ya_t.shape[2] // tm
    return pl.pallas_call(
        _out_ffn2_kernel,
        name="out_ffn2",
        grid=(t // tm, f // FF_TILE),
        in_specs=[
            pl.BlockSpec((tm, d), lambda i, j: (i, 0)),
            pl.BlockSpec((tm, CONV_WIDTH), lambda i, j: (i, 0)),
            pl.BlockSpec((1, ATTN_WIDTH, tm), lambda i, j: (i // tps, 0, i % tps)),
            pl.BlockSpec((1, ATTN_WIDTH), lambda i, j: (0, 0)),
            _resident(wo.shape),
            pl.BlockSpec((1, d), lambda i, j: (0, 0)),
            pl.BlockSpec((d, FF_TILE), lambda i, j: (0, j)),
            pl.BlockSpec((d, FF_TILE), lambda i, j: (0, j)),
            pl.BlockSpec((FF_TILE, d), lambda i, j: (j, 0)),
            pl.BlockSpec((1, d), lambda i, j: (0, 0)),
        ],
        out_specs=pl.BlockSpec((tm, d), lambda i, j: (i, 0)),
        out_shape=jax.ShapeDtypeStruct((t, d), F32),
        scratch_shapes=[pltpu.VMEM((tm, d), BF16), pltpu.VMEM((tm, d), F32)],
        compiler_params=pltpu.CompilerParams(
            dimension_semantics=("parallel", "arbitrary"), vmem_limit_bytes=VMEM_LIMIT),
    )(x1, yc, ya_t, ga, wo, g, wg, wu, wd, fin)


def _conv_proj_kernel(x_ref, g_ref, w_ref, cw_ref, gc_ref, o_ref, carry_ref, *, tiles_per_seq):
    i = pl.program_id(0)

    @pl.when(i % tiles_per_seq == 0)
    def _():
        carry_ref[...] = jnp.zeros_like(carry_ref)

    h = (_rms(x_ref[...]) * g_ref[...]).astype(BF16)
    c_h = _dot(h, w_ref[:, 0:CONV_WIDTH])
    c_b = _dot(h, w_ref[:, CONV_WIDTH:2 * CONV_WIDTH])
    c_c = _dot(h, w_ref[:, 2 * CONV_WIDTH:3 * CONV_WIDTH])
    u = c_c * c_h
    tm = u.shape[0]
    row = lax.broadcasted_iota(jnp.int32, u.shape, 0)
    prev1 = carry_ref[7:8, :]
    prev2 = carry_ref[6:7, :]
    u1 = jnp.where(row == 0, prev1, pltpu.roll(u, 1, 0))
    u2 = jnp.where(row == 0, prev2, jnp.where(row == 1, prev1, pltpu.roll(u, 2, 0)))
    carry_ref[...] = u[tm - 8:tm, :]
    y = c_b * (cw_ref[0:1, :] * u2 + cw_ref[1:2, :] * u1 + cw_ref[2:3, :] * u)
    o_ref[...] = (_rms(y) * gc_ref[...]).astype(BF16)


def _conv_proj(x1, g, w_conv, conv_w, gc, seq):
    t, d = x1.shape
    tm = TOKEN_TILE
    return pl.pallas_call(
        functools.partial(_conv_proj_kernel, tiles_per_seq=seq // tm),
        name="conv_proj",
        grid=(t // tm,),
        in_specs=[
            pl.BlockSpec((tm, d), lambda i: (i, 0)),
            pl.BlockSpec((1, d), lambda i: (0, 0)),
            _resident(w_conv.shape),
            pl.BlockSpec(conv_w.shape, lambda i: (0, 0)),
            pl.BlockSpec((1, CONV_WIDTH), lambda i: (0, 0)),
        ],
        out_specs=pl.BlockSpec((tm, CONV_WIDTH), lambda i: (i, 0)),
        out_shape=jax.ShapeDtypeStruct((t, CONV_WIDTH), BF16),
        scratch_shapes=[pltpu.VMEM((8, CONV_WIDTH), F32)],
        compiler_params=pltpu.CompilerParams(
            dimension_semantics=("arbitrary",), vmem_limit_bytes=VMEM_LIMIT),
    )(x1, g, w_conv, conv_w, gc)


def _rope(x, cos, sin_signed, first_half):
    outs = []
    for k in range(x.shape[1] // LANES):
        xc = x[:, k * LANES:(k + 1) * LANES]
        partner = jnp.where(first_half, pltpu.roll(xc, LANES - HEAD_DIM // 2, 1),
                            pltpu.roll(xc, HEAD_DIM // 2, 1))
        outs.append(xc * cos + partner * sin_signed)
    return outs


_Q0 = 0
_KC0 = ATTN_WIDTH
_VC0 = _KC0 + KV_WIDTH
_KS0 = _VC0 + KV_WIDTH
_KW0 = _KS0 + KV_WIDTH
_G0 = _KW0 + KV_WIDTH
ATTN_PROJ_WIDTH = _G0 + N_KV * GATE_PAD


def _attn_proj_kernel(x_ref, g_ref, w_ref, wvt_ref, cos_ref, sin_ref,
                      q_ref, kc_ref, vc_ref, ks_ref, kw_ref, vs_ref, vw_ref, gate_ref):
    h = (_rms(x_ref[...]) * g_ref[...]).astype(BF16)
    cos = cos_ref[...]
    sin = sin_ref[...]
    lane = lax.broadcasted_iota(jnp.int32, cos.shape, 1)
    first_half = (lane & (HEAD_DIM - 1)) < HEAD_DIM // 2

    def proj(c0, width):
        return _dot(h, w_ref[:, c0:c0 + width])

    def rope_to(ref, c0, width, dtype):
        for k, piece in enumerate(_rope(proj(c0, width), cos, sin, first_half)):
            ref[:, k * LANES:(k + 1) * LANES] = piece.astype(dtype)

    for k, piece in enumerate(_rope(proj(_Q0, ATTN_WIDTH), cos, sin, first_half)):
        q_ref[0, k * LANES:(k + 1) * LANES, :] = (piece * Q_SCALE).T.astype(BF16)
    rope_to(kc_ref, _KC0, KV_WIDTH, F32)
    vc_ref[...] = proj(_VC0, KV_WIDTH)
    rope_to(ks_ref, _KS0, KV_WIDTH, BF16)
    rope_to(kw_ref, _KW0, KV_WIDTH, BF16)
    gate_ref[0] = jax.nn.sigmoid(proj(_G0, N_KV * GATE_PAD)).T
    vt = _dot_nt(wvt_ref[...], h)
    for k in range(vt.shape[1] // Q_BLOCK):
        cs = slice(k * Q_BLOCK, (k + 1) * Q_BLOCK)
        vs_ref[0, k] = vt[0:KV_WIDTH, cs].astype(BF16)
        vw_ref[0, k] = vt[KV_WIDTH:2 * KV_WIDTH, cs].astype(BF16)


def _attn_proj(x1, g, w_attn, w_vt, cos, sin, seq):
    t, d = x1.shape
    tm = TOKEN_TILE
    tps = seq // tm
    bsz = t // seq
    cpt = tm // Q_BLOCK
    row = lambda w: pl.BlockSpec((tm, w), lambda i: (i, 0))
    col = lambda w: pl.BlockSpec((1, w, tm), lambda i: (i // tps, 0, i % tps))
    chunked = pl.BlockSpec((1, cpt, KV_WIDTH, Q_BLOCK), lambda i: (i // tps, i % tps, 0, 0))
    kv_dtypes = [F32, F32, BF16, BF16]
    out_specs = [col(ATTN_WIDTH)] + [row(KV_WIDTH)] * 4 + [chunked, chunked, col(N_KV * GATE_PAD)]
    out_shape = ([jax.ShapeDtypeStruct((bsz, ATTN_WIDTH, seq), BF16)]
                 + [jax.ShapeDtypeStruct((t, KV_WIDTH), dt) for dt in kv_dtypes]
                 + [jax.ShapeDtypeStruct((bsz, seq // Q_BLOCK, KV_WIDTH, Q_BLOCK), BF16)] * 2
                 + [jax.ShapeDtypeStruct((bsz, N_KV * GATE_PAD, seq), F32)])
    return pl.pallas_call(
        _attn_proj_kernel,
        name="attn_proj",
        grid=(t // tm,),
        in_specs=[
            row(d),
            pl.BlockSpec((1, d), lambda i: (0, 0)),
            _resident(w_attn.shape),
            _resident(w_vt.shape),
            pl.BlockSpec((tm, LANES), lambda i: (i % tps, 0)),
            pl.BlockSpec((tm, LANES), lambda i: (i % tps, 0)),
        ],
        out_specs=out_specs,
        out_shape=out_shape,
        compiler_params=pltpu.CompilerParams(
            dimension_semantics=("parallel",), vmem_limit_bytes=VMEM_LIMIT),
    )(x1, g, w_attn, w_vt, cos, sin)


def _compress_kernel(c_ref, pe_ref, w1_ref, w2_ref, o_ref, *, transposed):
    c = c_ref[0, 0]
    a = _dot((c + pe_ref[0]).astype(BF16), w1_ref[0])
    b = _dot((c + pe_ref[1]).astype(BF16), w1_ref[1])
    n = a.shape[0]
    hid = jax.nn.gelu(a + pltpu.roll(b, n - 1, 0)).astype(BF16)
    if transposed:
        o_ref[0, 0] = _dot_nt(w2_ref[...], hid).astype(BF16)
    else:
        o_ref[0, 0] = _dot(hid, w2_ref[...]).astype(BF16)


def _compress(chunks, pe, w1, w2, transposed):
    bsz, nkv, n, width = chunks.shape
    out_block = (1, 1, HEAD_DIM, n) if transposed else (1, 1, n, LANES)
    return pl.pallas_call(
        functools.partial(_compress_kernel, transposed=transposed),
        name="compress_v" if transposed else "compress_k",
        grid=(bsz, nkv),
        in_specs=[
            pl.BlockSpec((1, 1, n, width), lambda b, h: (b, h, 0, 0)),
            pl.BlockSpec(pe.shape, lambda b, h: (0, 0, 0)),
            pl.BlockSpec(w1.shape, lambda b, h: (0, 0, 0)),
            pl.BlockSpec(w2.shape, lambda b, h: (0, 0)),
        ],
        out_specs=pl.BlockSpec(out_block, lambda b, h: (b, h, 0, 0)),
        out_shape=jax.ShapeDtypeStruct((bsz, nkv) + out_block[2:], BF16),
        compiler_params=pltpu.CompilerParams(
            dimension_semantics=("parallel", "parallel"), vmem_limit_bytes=VMEM_LIMIT),
    )(chunks, pe, w1, w2)


def _tile4(x):
    return jnp.concatenate([x] * GROUP, axis=1)


def _col_max(x):
    return jnp.max(x, axis=0, keepdims=True)


def _col_sum(x):
    return jnp.sum(x, axis=0, keepdims=True)


def _with_ones(v):
    ones = jnp.where(_iota((ONES_ROWS, v.shape[1]), 0) == 0, 1.0, 0.0).astype(v.dtype)
    return jnp.concatenate([v, ones], axis=0)


def _nsa_kernel(q_ref, gate_ref, kc_ref, vc_ref, ks_ref, kw_ref, vs_ref, vw_ref, o_ref,
                kse_ref, kwe_ref, imp_ref, *, seq):
    hp = pl.program_id(1)
    c = pl.program_id(2)
    t0 = c * Q_BLOCK
    n_cmp = kc_ref.shape[2]
    n_slc = seq // SLC_BLOCK
    cols = GROUP * Q_BLOCK
    n_chunks = seq // Q_BLOCK
    chunks_per_tile = KEY_TILE // Q_BLOCK
    heads = range(HEADS_PER_STEP)
    assert n_slc + HEAD_DIM == LANES and n_slc % 8 == 0

    @pl.when(c == 0)
    def _():
        r = _iota((KV_WIDTH, 2 * LANES), 0)
        ln = _iota((KV_WIDTH, 2 * LANES), 1)
        for hh in heads:
            h = hp * HEADS_PER_STEP + hh
            pick = ((r == h * HEAD_DIM + ln) & (ln < HEAD_DIM)).astype(BF16)

            def build_k(i, _, hh=hh, pick=pick):
                sl = pl.ds(pl.multiple_of(i * KEY_TILE, KEY_TILE), KEY_TILE)
                pos = i * KEY_TILE + _iota((KEY_TILE, 2 * LANES), 0)
                lane = _iota((KEY_TILE, 2 * LANES), 1)
                in_chunk = lane - LANES == (pos & (Q_BLOCK - 1))
                in_block = lane - HEAD_DIM == _div_pow2(pos, SLC_BLOCK)
                kse_ref[hh, sl, :] = (_dot(ks_ref[0, sl, :], pick)
                                      + jnp.where(in_chunk | in_block, 1.0, 0.0)).astype(BF16)
                kwe_ref[hh, sl, :] = (_dot(kw_ref[0, sl, :], pick)
                                      + jnp.where(in_chunk, 1.0, 0.0)).astype(BF16)
                return 0

            lax.fori_loop(0, seq // KEY_TILE, build_k, 0)
            kwe_ref[hh, pl.ds(seq, Q_BLOCK), :] = jnp.where(
                _iota((Q_BLOCK, 2 * LANES), 1) == HEAD_DIM, 1.0, 0.0).astype(BF16)

    def v_chunk(ref, hh, idx):
        return _with_ones(ref[0, idx, hh * HEAD_DIM:(hh + 1) * HEAD_DIM, :])

    def v_tile(hh, kt):
        return _with_ones(jnp.concatenate(
            [vs_ref[0, kt * chunks_per_tile + k, hh * HEAD_DIM:(hh + 1) * HEAD_DIM, :]
             for k in range(chunks_per_tile)], axis=1))

    def soft(scores, m):
        return [jnp.exp2(s - m).astype(BF16) for s in scores]

    kpos = _iota((Q_BLOCK, Q_BLOCK), 0)
    tpos = _iota((Q_BLOCK, Q_BLOCK), 1)
    causal = _tile4(jnp.where(kpos <= tpos, 0.0, NEG).astype(BF16))
    recent = _tile4(jnp.where(kpos > tpos, 0.0, NEG).astype(BF16))
    pad = jnp.zeros((HEAD_DIM, cols), BF16)
    dummy_row = jnp.where(_iota((HEAD_DIM, cols), 0) == 0, NEG, 0.0).astype(BF16)
    jj = _iota((n_slc, n_cmp), 0)
    ii = _iota((n_slc, n_cmp), 1)
    ov = (jnp.minimum(ii * CMP_STRIDE + CMP_BLOCK, jj * SLC_BLOCK + SLC_BLOCK)
          - jnp.maximum(ii * CMP_STRIDE, jj * SLC_BLOCK))
    ov = (jnp.maximum(ov, 0).astype(F32) * (1.0 / CMP_BLOCK)).astype(BF16)
    c_ok = (_iota((n_cmp, Q_BLOCK), 0) * CMP_STRIDE + (CMP_BLOCK - 1)
            <= t0 + _iota((n_cmp, Q_BLOCK), 1))
    c_bias = _tile4(jnp.where(c_ok, 0.0, NEG))
    any_ok = t0 + (_iota((1, cols), 1) & (Q_BLOCK - 1)) >= CMP_BLOCK - 1
    jb = _iota((n_slc, Q_BLOCK), 0)
    tb = _div_pow2(t0 + _iota((n_slc, Q_BLOCK), 1), SLC_BLOCK)
    forced = (jb == 0) | ((jb <= tb) & (jb > tb - N_LOCAL))
    sub = _iota((8, Q_BLOCK), 0)
    n_win = WIN_KEYS // Q_BLOCK
    first = c - (n_win - 1)
    k_chunk = [jnp.where(first + i >= 0, first + i, n_chunks) for i in range(n_win)]
    v_idx = [jnp.maximum(first + i, 0) for i in range(n_win)]
    k_rows = [pl.ds(pl.multiple_of(ch * Q_BLOCK, Q_BLOCK), Q_BLOCK) for ch in k_chunk]

    def before_sweep(hh):
        qt = q_ref[0, hh * GROUP_WIDTH:(hh + 1) * GROUP_WIDTH, :]
        wq = jnp.concatenate([qt[g * HEAD_DIM:(g + 1) * HEAD_DIM, :] for g in range(GROUP)], axis=1)

        sc = _dot(kc_ref[0, hh], jnp.concatenate([wq, pad], axis=0)) + c_bias
        pc = jnp.exp2(sc - _col_max(sc))
        pc = pc * jnp.where(any_ok, 1.0 / _col_sum(pc), 0.0)
        o_cmp = _dot(vc_ref[0, hh], pc.astype(BF16))

        pcs = pc[:, 0:Q_BLOCK]
        for g in range(1, GROUP):
            pcs = pcs + pc[:, g * Q_BLOCK:(g + 1) * Q_BLOCK]
        pcs_hi = pcs.astype(BF16)
        pcs_lo = (pcs - pcs_hi.astype(F32)).astype(BF16)
        imp = _dot(ov, pcs_hi) + _dot(ov, pcs_lo)
        imp = jnp.where(forced, FORCE, imp)
        imp = jnp.where(jb <= tb, imp, NEG)
        imp_ref[hh] = imp

        n_groups = n_slc // 8
        mine = [imp[8 * r:8 * r + 8, :] for r in range(n_groups)]
        ahead_count = [jnp.zeros((8, Q_BLOCK), F32)] * n_groups
        for j in range(n_slc):
            other = imp_ref[hh, j:j + 1, :]
            for r in range(n_groups):
                if 8 * r > j:
                    ahead = other >= mine[r]
                elif 8 * r + 7 <= j:
                    ahead = other > mine[r]
                else:
                    ahead = (other > mine[r]) | ((other == mine[r]) & (sub > j - 8 * r))
                ahead_count[r] = ahead_count[r] + jnp.where(ahead, 1.0, 0.0)
        sel = (jnp.concatenate(ahead_count, axis=0) < SLC_TOPK) & (imp > 0.5 * NEG)
        sel_bias = jnp.where(sel & (jb * SLC_BLOCK < t0), 0.0, NEG).astype(BF16)
        w_sel = jnp.concatenate([wq, _tile4(sel_bias)], axis=0)

        w_causal = jnp.concatenate([wq, pad, causal], axis=0)
        w_mid = jnp.concatenate([wq, dummy_row], axis=0)
        w_recent = jnp.concatenate([wq, dummy_row, recent], axis=0)
        sw = [_dot(kwe_ref[hh, k_rows[0], :], w_recent)]
        sw += [_dot(kwe_ref[hh, k_rows[i], 0:LANES], w_mid) for i in range(1, n_win - 1)]
        sw += [_dot(kwe_ref[hh, k_rows[n_win - 1], :], w_causal)]
        m_w = _col_max(sw[0])
        for s in sw[1:]:
            m_w = jnp.maximum(m_w, _col_max(s))
        o_win = None
        for idx, p in zip(v_idx, soft(sw, m_w)):
            part = _dot(v_chunk(vw_ref, hh, idx), p)
            o_win = part if o_win is None else o_win + part

        s_d = _dot(kse_ref[hh, pl.ds(pl.multiple_of(t0, Q_BLOCK), Q_BLOCK), :], w_causal)
        m_d = _col_max(s_d)
        acc_d = _dot(v_chunk(vs_ref, hh, c), soft([s_d], m_d)[0])
        return w_sel, o_cmp, o_win, (m_d, acc_d)

    pre = [before_sweep(hh) for hh in heads]

    def scores(hh, kt):
        k0 = pl.multiple_of(kt * KEY_TILE, KEY_TILE)
        return _dot(kse_ref[hh, pl.ds(k0, KEY_TILE), 0:LANES], pre[hh][0])

    def update(hh, s, kt, state):
        m, acc = state
        m_new = jnp.maximum(m, _col_max(s))
        p = jnp.exp2(s - m_new).astype(BF16)
        return m_new, jnp.exp2(m - m_new) * acc + _dot(v_tile(hh, kt), p)

    def tile_pair(i, states):
        s = [[scores(hh, 2 * i), scores(hh, 2 * i + 1)] for hh in heads]
        return tuple(update(hh, s[hh][1], 2 * i + 1, update(hh, s[hh][0], 2 * i, states[hh]))
                     for hh in heads)

    n_tiles = _div_pow2(t0 + KEY_TILE - 1, KEY_TILE)
    swept = lax.fori_loop(0, _div_pow2(n_tiles + 1, 2), tile_pair, tuple(p[3] for p in pre))

    for hh in heads:
        _, o_cmp, o_win, _ = pre[hh]
        o_slc = swept[hh][1]
        gates = gate_ref[0, hh * GATE_PAD:(hh + 1) * GATE_PAD, :]
        inv_ls = 1.0 / o_slc[HEAD_DIM:HEAD_DIM + 1, :]
        inv_lw = 1.0 / o_win[HEAD_DIM:HEAD_DIM + 1, :]
        pieces = []
        for g in range(GROUP):
            cs = slice(g * Q_BLOCK, (g + 1) * Q_BLOCK)
            row = g * N_GATES
            pieces.append(gates[row:row + 1, :] * o_cmp[:, cs]
                          + gates[row + 1:row + 2, :] * (o_slc[0:HEAD_DIM, cs] * inv_ls[:, cs])
                          + gates[row + 2:row + 3, :] * (o_win[0:HEAD_DIM, cs] * inv_lw[:, cs]))
        o_ref[0, hh * GROUP_WIDTH:(hh + 1) * GROUP_WIDTH, :] = jnp.concatenate(pieces, axis=0)


def _nsa(q_t, gates_t, kc, vc_t, ks, kw, vs_t, vw_t):
    bsz, _, seq = q_t.shape
    n_cmp = kc.shape[2]
    hps = HEADS_PER_STEP
    k_spec = pl.BlockSpec((1, seq, KV_WIDTH), lambda b, h, c: (b, 0, 0))
    v_spec = pl.BlockSpec((1, seq // Q_BLOCK, hps * HEAD_DIM, Q_BLOCK), lambda b, h, c: (b, 0, h, 0))
    return pl.pallas_call(
        functools.partial(_nsa_kernel, seq=seq),
        name="nsa",
        grid=(bsz, N_KV // hps, seq // Q_BLOCK),
        in_specs=[
            pl.BlockSpec((1, hps * GROUP_WIDTH, Q_BLOCK), lambda b, h, c: (b, h, c)),
            pl.BlockSpec((1, hps * GATE_PAD, Q_BLOCK), lambda b, h, c: (b, h, c)),
            pl.BlockSpec((1, hps, n_cmp, LANES), lambda b, h, c: (b, h, 0, 0)),
            pl.BlockSpec((1, hps, HEAD_DIM, n_cmp), lambda b, h, c: (b, h, 0, 0)),
            k_spec, k_spec, v_spec, v_spec,
        ],
        out_specs=pl.BlockSpec((1, hps * GROUP_WIDTH, Q_BLOCK), lambda b, h, c: (b, h, c)),
        out_shape=jax.ShapeDtypeStruct((bsz, ATTN_WIDTH, seq), F32),
        scratch_shapes=[
            pltpu.VMEM((hps, seq, 2 * LANES), BF16),
            pltpu.VMEM((hps, seq + Q_BLOCK, 2 * LANES), BF16),
            pltpu.VMEM((hps, seq // SLC_BLOCK, Q_BLOCK), F32),
        ],
        compiler_params=pltpu.CompilerParams(
            dimension_semantics=("parallel", "parallel", "arbitrary"),
            vmem_limit_bytes=VMEM_LIMIT),
    )(q_t, gates_t, kc, vc_t, ks, kw, vs_t, vw_t)


def _rope_tables(seq):
    pos = jnp.arange(seq, dtype=F32)
    inv = ROPE_THETA ** (-jnp.arange(0, HEAD_DIM, 2, dtype=F32) / HEAD_DIM)
    ang = pos[:, None] * inv[None, :]
    cos, sin = jnp.cos(ang), jnp.sin(ang)
    reps = LANES // HEAD_DIM
    return (jnp.tile(jnp.concatenate([cos, cos], axis=1), (1, reps)),
            jnp.tile(jnp.concatenate([-sin, sin], axis=1), (1, reps)))


def _head_chunks(tok, bsz, seq):
    x = tok.reshape(bsz, seq // CMP_STRIDE, CMP_STRIDE, N_KV, HEAD_DIM)
    return x.transpose(0, 3, 1, 2, 4).reshape(bsz, N_KV, seq // CMP_STRIDE, CMP_STRIDE * HEAD_DIM)


def _compress_params(w1, w2, pe):
    halves = CMP_BLOCK // CMP_STRIDE
    w1 = w1.reshape(halves, CMP_STRIDE * HEAD_DIM, CMP_HIDDEN).astype(BF16)
    pe = pe.reshape(halves, 1, CMP_STRIDE * HEAD_DIM)
    return pe, w1, w2.astype(BF16)


def kernel(x, ffn1_norm, ffn1_wg, ffn1_wu, ffn1_wd, mix_norm, w_in, conv_w, cmp_k_w1, cmp_k_w2,
           cmp_k_pe, cmp_v_w1, cmp_v_w2, cmp_v_pe, conv_out_norm, attn_out_norm, w_out, ffn2_norm,
           ffn2_wg, ffn2_wu, ffn2_wd, final_norm):
    bsz, seq, d = x.shape
    assert CMP_BLOCK == 2 * CMP_STRIDE and seq % TOKEN_TILE == 0 and seq % KEY_TILE == 0
    assert x.shape[0] * seq % TOKEN_TILE == 0 and ffn1_wg.shape[0] == 1
    xt = x.reshape(bsz * seq, d)
    cos, sin = _rope_tables(seq)

    w = w_in[0]
    w_conv = w[:, :3 * CONV_WIDTH].astype(BF16)
    a0 = 3 * CONV_WIDTH
    kv0 = a0 + ATTN_WIDTH
    col = lambda k: w[:, kv0 + k * KV_WIDTH:kv0 + (k + 1) * KV_WIDTH]
    g0 = kv0 + 6 * KV_WIDTH
    wg = w[:, g0:].reshape(d, N_KV, GROUP * N_GATES)
    wg = jnp.pad(wg, ((0, 0), (0, 0), (0, GATE_PAD - GROUP * N_GATES))).reshape(d, N_KV * GATE_PAD)
    w_attn = jnp.concatenate([w[:, a0:kv0], col(0), col(1), col(2), col(4), wg], axis=1).astype(BF16)
    w_vt = jnp.concatenate([col(3), col(5)], axis=1).T.astype(BF16)

    x1 = _ffn1(xt, ffn1_norm, ffn1_wg[0].astype(BF16), ffn1_wu[0].astype(BF16),
               ffn1_wd[0].astype(BF16))
    yc = _conv_proj(x1, mix_norm, w_conv, conv_w[0], conv_out_norm, seq)
    q_t, k_c, v_c, k_s, k_w, vs_t, vw_t, gates_t = _attn_proj(x1, mix_norm, w_attn, w_vt, cos, sin, seq)

    pe_k, w1_k, w2_k = _compress_params(cmp_k_w1[0], cmp_k_w2[0], cmp_k_pe[0])
    pe_v, w1_v, w2_v = _compress_params(cmp_v_w1[0], cmp_v_w2[0], cmp_v_pe[0])
    kc = _compress(_head_chunks(k_c, bsz, seq), pe_k, w1_k,
                   jnp.pad(w2_k, ((0, 0), (0, LANES - HEAD_DIM))), transposed=False)
    vc_t = _compress(_head_chunks(v_c, bsz, seq), pe_v, w1_v, w2_v.T, transposed=True)

    per_batch = lambda a: a.reshape(bsz, seq, a.shape[-1])
    ya_t = _nsa(q_t, gates_t, kc, vc_t, per_batch(k_s), per_batch(k_w), vs_t, vw_t)

    out = _out_ffn2(x1, yc, ya_t, attn_out_norm,
                    w_out[0].astype(BF16), ffn2_norm, ffn2_wg[0].astype(BF16),
                    ffn2_wu[0].astype(BF16), ffn2_wd[0].astype(BF16), final_norm.reshape(1, d))
    return out.reshape(bsz, seq, d)
```

```python
import functools

import jax
import jax.numpy as jnp
from jax import lax
from jax.experimental import pallas as pl
from jax.experimental.pallas import tpu as pltpu

F32 = jnp.float32
BF16 = jnp.bfloat16

D_MODEL = 2048
D_FF = 5632
CONV_WIDTH = 1024
N_HEADS = 16
N_KV = 4
GROUP = N_HEADS // N_KV
HEAD_DIM = 64
ATTN_WIDTH = N_HEADS * HEAD_DIM
KV_WIDTH = N_KV * HEAD_DIM
CMP_BLOCK = 32
CMP_STRIDE = 16
CMP_HIDDEN = 256
SLC_BLOCK = 64
SLC_TOPK = 16
N_LOCAL = 2
WINDOW = 512
Q_BLOCK = 128
N_GATES = 3
ROPE_THETA = 10000.0
EPS = 1e-6
NEG = -1e30
FORCE = 1e9

LANES = 128
GROUP_WIDTH = GROUP * HEAD_DIM
GATE_PAD = LANES
TOKEN_TILE = 512
FF_TILE = 512
KEY_TILE = 512
WIN_KEYS = WINDOW + Q_BLOCK
ONES_ROWS = 16
HEADS_PER_STEP = 2
VMEM_LIMIT = 56 * 1024 * 1024
Q_SCALE = HEAD_DIM ** -0.5 * 1.4426950408889634


def _rms(x):
    return x * lax.rsqrt(jnp.mean(x * x, axis=-1, keepdims=True) + EPS)


def _dot(a, b):
    return jnp.dot(a, b, preferred_element_type=F32)


def _dot_nt(a, b):
    return lax.dot_general(a, b, (((1,), (1,)), ((), ())), preferred_element_type=F32)


def _div_pow2(x, n):
    assert n & (n - 1) == 0
    return jnp.right_shift(x, n.bit_length() - 1)


def _iota(shape, dim):
    return lax.broadcasted_iota(jnp.int32, shape, dim)


def _resident(shape):
    return pl.BlockSpec(shape, lambda *_: (0,) * len(shape), pipeline_mode=pl.Buffered(1))


def _ffn_accumulate(h_ref, wg_ref, wu_ref, wd_ref, acc_ref):
    h = h_ref[...]
    a = _dot(h, wg_ref[...])
    u = _dot(h, wu_ref[...])
    act = (jax.nn.silu(a) * u).astype(BF16)
    acc_ref[...] += _dot(act, wd_ref[...])


def _ffn1_kernel(x_ref, g_ref, wg_ref, wu_ref, wd_ref, o_ref, h_ref, acc_ref):
    j = pl.program_id(1)

    @pl.when(j == 0)
    def _():
        h_ref[...] = (_rms(x_ref[...]) * g_ref[...]).astype(BF16)
        acc_ref[...] = jnp.zeros_like(acc_ref)

    _ffn_accumulate(h_ref, wg_ref, wu_ref, wd_ref, acc_ref)

    @pl.when(j == pl.num_programs(1) - 1)
    def _():
        o_ref[...] = x_ref[...] + 0.5 * acc_ref[...]


def _ffn1(x, g, wg, wu, wd):
    t, d = x.shape
    f = wg.shape[1]
    return pl.pallas_call(
        _ffn1_kernel,
        name="ffn1",
        grid=(t // TOKEN_TILE, f // FF_TILE),
        in_specs=[
            pl.BlockSpec((TOKEN_TILE, d), lambda i, j: (i, 0)),
            pl.BlockSpec((1, d), lambda i, j: (0, 0)),
            pl.BlockSpec((d, FF_TILE), lambda i, j: (0, j)),
            pl.BlockSpec((d, FF_TILE), lambda i, j: (0, j)),
            pl.BlockSpec((FF_TILE, d), lambda i, j: (j, 0)),
        ],
        out_specs=pl.BlockSpec((TOKEN_TILE, d), lambda i, j: (i, 0)),
        out_shape=jax.ShapeDtypeStruct((t, d), F32),
        scratch_shapes=[pltpu.VMEM((TOKEN_TILE, d), BF16), pltpu.VMEM((TOKEN_TILE, d), F32)],
        compiler_params=pltpu.CompilerParams(
            dimension_semantics=("parallel", "arbitrary"), vmem_limit_bytes=VMEM_LIMIT),
    )(x, g, wg, wu, wd)


def _out_ffn2_kernel(x_ref, yc_ref, ya_ref, ga_ref, wo_ref, g_ref, wg_ref, wu_ref, wd_ref,
                     fin_ref, o_ref, h_ref, acc_ref):
    j = pl.program_id(1)

    @pl.when(j == 0)
    def _():
        ya = (_rms(ya_ref[0].T) * ga_ref[...]).astype(BF16)
        mix = _dot(yc_ref[...], wo_ref[0:CONV_WIDTH, :]) + _dot(ya, wo_ref[CONV_WIDTH:, :])
        x2 = x_ref[...] + mix
        o_ref[...] = x2
        h_ref[...] = (_rms(x2) * g_ref[...]).astype(BF16)
        acc_ref[...] = jnp.zeros_like(acc_ref)

    _ffn_accumulate(h_ref, wg_ref, wu_ref, wd_ref, acc_ref)

    @pl.when(j == pl.num_programs(1) - 1)
    def _():
        o_ref[...] = _rms(o_ref[...] + 0.5 * acc_ref[...]) * fin_ref[...]


def _out_ffn2(x1, yc, ya_t, ga, wo, g, wg, wu, wd, fin):
    t, d = x1.shape
    f = wg.shape[1]
    tm = TOKEN_TILE
    tps = ya_t.shape[2] // tm
    return pl.pallas_call(
        _out_ffn2_kernel,
        name="out_ffn2",
        grid=(t // tm, f // FF_TILE),
        in_specs=[
            pl.BlockSpec((tm, d), lambda i, j: (i, 0)),
            pl.BlockSpec((tm, CONV_WIDTH), lambda i, j: (i, 0)),
            pl.BlockSpec((1, ATTN_WIDTH, tm), lambda i, j: (i // tps, 0, i % tps)),
            pl.BlockSpec((1, ATTN_WIDTH), lambda i, j: (0, 0)),
            _resident(wo.shape),
            pl.BlockSpec((1, d), lambda i, j: (0, 0)),
            pl.BlockSpec((d, FF_TILE), lambda i, j: (0, j)),
            pl.BlockSpec((d, FF_TILE), lambda i, j: (0, j)),
            pl.BlockSpec((FF_TILE, d), lambda i, j: (j, 0)),
            pl.BlockSpec((1, d), lambda i, j: (0, 0)),
        ],
        out_specs=pl.BlockSpec((tm, d), lambda i, j: (i, 0)),
        out_shape=jax.ShapeDtypeStruct((t, d), F32),
        scratch_shapes=[pltpu.VMEM((tm, d), BF16), pltpu.VMEM((tm, d), F32)],
        compiler_params=pltpu.CompilerParams(
            dimension_semantics=("parallel", "arbitrary"), vmem_limit_bytes=VMEM_LIMIT),
    )(x1, yc, ya_t, ga, wo, g, wg, wu, wd, fin)


def _conv_proj_kernel(x_ref, g_ref, w_ref, cw_ref, gc_ref, o_ref, carry_ref, *, tiles_per_seq):
    i = pl.program_id(0)

    @pl.when(i % tiles_per_seq == 0)
    def _():
        carry_ref[...] = jnp.zeros_like(carry_ref)

    h = (_rms(x_ref[...]) * g_ref[...]).astype(BF16)
    c_h = _dot(h, w_ref[:, 0:CONV_WIDTH])
    c_b = _dot(h, w_ref[:, CONV_WIDTH:2 * CONV_WIDTH])
    c_c = _dot(h, w_ref[:, 2 * CONV_WIDTH:3 * CONV_WIDTH])
    u = c_c * c_h
    tm = u.shape[0]
    row = lax.broadcasted_iota(jnp.int32, u.shape, 0)
    prev1 = carry_ref[7:8, :]
    prev2 = carry_ref[6:7, :]
    u1 = jnp.where(row == 0, prev1, pltpu.roll(u, 1, 0))
    u2 = jnp.where(row == 0, prev2, jnp.where(row == 1, prev1, pltpu.roll(u, 2, 0)))
    carry_ref[...] = u[tm - 8:tm, :]
    y = c_b * (cw_ref[0:1, :] * u2 + cw_ref[1:2, :] * u1 + cw_ref[2:3, :] * u)
    o_ref[...] = (_rms(y) * gc_ref[...]).astype(BF16)


def _conv_proj(x1, g, w_conv, conv_w, gc, seq):
    t, d = x1.shape
    tm = TOKEN_TILE
    return pl.pallas_call(
        functools.partial(_conv_proj_kernel, tiles_per_seq=seq // tm),
        name="conv_proj",
        grid=(t // tm,),
        in_specs=[
            pl.BlockSpec((tm, d), lambda i: (i, 0)),
            pl.BlockSpec((1, d), lambda i: (0, 0)),
            _resident(w_conv.shape),
            pl.BlockSpec(conv_w.shape, lambda i: (0, 0)),
            pl.BlockSpec((1, CONV_WIDTH), lambda i: (0, 0)),
        ],
        out_specs=pl.BlockSpec((tm, CONV_WIDTH), lambda i: (i, 0)),
        out_shape=jax.ShapeDtypeStruct((t, CONV_WIDTH), BF16),
        scratch_shapes=[pltpu.VMEM((8, CONV_WIDTH), F32)],
        compiler_params=pltpu.CompilerParams(
            dimension_semantics=("arbitrary",), vmem_limit_bytes=VMEM_LIMIT),
    )(x1, g, w_conv, conv_w, gc)


def _rope(x, cos, sin_signed, first_half):
    outs = []
    for k in range(x.shape[1] // LANES):
        xc = x[:, k * LANES:(k + 1) * LANES]
        partner = jnp.where(first_half, pltpu.roll(xc, LANES - HEAD_DIM // 2, 1),
                            pltpu.roll(xc, HEAD_DIM // 2, 1))
        outs.append(xc * cos + partner * sin_signed)
    return outs


_Q0 = 0
_KC0 = ATTN_WIDTH
_VC0 = _KC0 + KV_WIDTH
_KS0 = _VC0 + KV_WIDTH
_KW0 = _KS0 + KV_WIDTH
_G0 = _KW0 + KV_WIDTH
ATTN_PROJ_WIDTH = _G0 + N_KV * GATE_PAD


def _attn_proj_kernel(x_ref, g_ref, w_ref, wvt_ref, cos_ref, sin_ref,
                      q_ref, kc_ref, vc_ref, ks_ref, kw_ref, vs_ref, vw_ref, gate_ref):
    h = (_rms(x_ref[...]) * g_ref[...]).astype(BF16)
    cos = cos_ref[...]
    sin = sin_ref[...]
    lane = lax.broadcasted_iota(jnp.int32, cos.shape, 1)
    first_half = (lane & (HEAD_DIM - 1)) < HEAD_DIM // 2

    def proj(c0, width):
        return _dot(h, w_ref[:, c0:c0 + width])

    def rope_to(ref, c0, width, dtype):
        for k, piece in enumerate(_rope(proj(c0, width), cos, sin, first_half)):
            ref[:, k * LANES:(k + 1) * LANES] = piece.astype(dtype)

    for k, piece in enumerate(_rope(proj(_Q0, ATTN_WIDTH), cos, sin, first_half)):
        q_ref[0, k * LANES:(k + 1) * LANES, :] = (piece * Q_SCALE).T.astype(BF16)
    rope_to(kc_ref, _KC0, KV_WIDTH, F32)
    vc_ref[...] = proj(_VC0, KV_WIDTH)
    rope_to(ks_ref, _KS0, KV_WIDTH, BF16)
    rope_to(kw_ref, _KW0, KV_WIDTH, BF16)
    gate_ref[0] = jax.nn.sigmoid(proj(_G0, N_KV * GATE_PAD)).T
    vt = _dot_nt(wvt_ref[...], h)
    for k in range(vt.shape[1] // Q_BLOCK):
        cs = slice(k * Q_BLOCK, (k + 1) * Q_BLOCK)
        vs_ref[0, k] = vt[0:KV_WIDTH, cs].astype(BF16)
        vw_ref[0, k] = vt[KV_WIDTH:2 * KV_WIDTH, cs].astype(BF16)


def _attn_proj(x1, g, w_attn, w_vt, cos, sin, seq):
    t, d = x1.shape
    tm = TOKEN_TILE
    tps = seq // tm
    bsz = t // seq
    cpt = tm // Q_BLOCK
    row = lambda w: pl.BlockSpec((tm, w), lambda i: (i, 0))
    col = lambda w: pl.BlockSpec((1, w, tm), lambda i: (i // tps, 0, i % tps))
    chunked = pl.BlockSpec((1, cpt, KV_WIDTH, Q_BLOCK), lambda i: (i // tps, i % tps, 0, 0))
    kv_dtypes = [F32, F32, BF16, BF16]
    out_specs = [col(ATTN_WIDTH)] + [row(KV_WIDTH)] * 4 + [chunked, chunked, col(N_KV * GATE_PAD)]
    out_shape = ([jax.ShapeDtypeStruct((bsz, ATTN_WIDTH, seq), BF16)]
                 + [jax.ShapeDtypeStruct((t, KV_WIDTH), dt) for dt in kv_dtypes]
                 + [jax.ShapeDtypeStruct((bsz, seq // Q_BLOCK, KV_WIDTH, Q_BLOCK), BF16)] * 2
                 + [jax.ShapeDtypeStruct((bsz, N_KV * GATE_PAD, seq), F32)])
    return pl.pallas_call(
        _attn_proj_kernel,
        name="attn_proj",
        grid=(t // tm,),
        in_specs=[
            row(d),
            pl.BlockSpec((1, d), lambda i: (0, 0)),
            _resident(w_attn.shape),
            _resident(w_vt.shape),
            pl.BlockSpec((tm, LANES), lambda i: (i % tps, 0)),
            pl.BlockSpec((tm, LANES), lambda i: (i % tps, 0)),
        ],
        out_specs=out_specs,
        out_shape=out_shape,
        compiler_params=pltpu.CompilerParams(
            dimension_semantics=("parallel",), vmem_limit_bytes=VMEM_LIMIT),
    )(x1, g, w_attn, w_vt, cos, sin)


HEADS_PER_TILE = LANES // HEAD_DIM


def _compress_kernel(*refs, transposed):
    tok_refs = refs[:N_KV // HEADS_PER_TILE]
    pe_ref, w1_ref, w2_ref, o_ref = refs[N_KV // HEADS_PER_TILE:]
    n = tok_refs[0].shape[0] // CMP_STRIDE
    for t, tok_ref in enumerate(tok_refs):
        first = jnp.zeros((n, HEADS_PER_TILE * CMP_HIDDEN), F32)
        second = jnp.zeros((n, HEADS_PER_TILE * CMP_HIDDEN), F32)
        for l in range(CMP_STRIDE):
            rows = tok_ref[pl.ds(l, n, stride=CMP_STRIDE), :]
            first = first + _dot((rows + pe_ref[l]).astype(BF16), w1_ref[l])
            second = second + _dot((rows + pe_ref[l + CMP_STRIDE]).astype(BF16),
                                   w1_ref[l + CMP_STRIDE])
        hid = jax.nn.gelu(first + pltpu.roll(second, n - 1, 0)).astype(BF16)
        for k in range(HEADS_PER_TILE):
            hid_h = hid[:, k * CMP_HIDDEN:(k + 1) * CMP_HIDDEN]
            h = t * HEADS_PER_TILE + k
            if transposed:
                o_ref[0, h] = _dot_nt(w2_ref[...], hid_h).astype(BF16)
            else:
                o_ref[0, h] = _dot(hid_h, w2_ref[...]).astype(BF16)


def _compress(tok, pe, w1, w2, seq, transposed):
    bsz = tok.shape[0] // seq
    n = seq // CMP_STRIDE
    out_block = (1, N_KV, HEAD_DIM, n) if transposed else (1, N_KV, n, LANES)
    lane_tile = lambda t: pl.BlockSpec((seq, LANES), lambda b: (b, t))
    return pl.pallas_call(
        functools.partial(_compress_kernel, transposed=transposed),
        name="compress_v" if transposed else "compress_k",
        grid=(bsz,),
        in_specs=[lane_tile(t) for t in range(N_KV // HEADS_PER_TILE)] + [
            _resident(pe.shape),
            _resident(w1.shape),
            _resident(w2.shape),
        ],
        out_specs=pl.BlockSpec(out_block, lambda b: (b, 0, 0, 0)),
        out_shape=jax.ShapeDtypeStruct((bsz,) + out_block[1:], BF16),
        compiler_params=pltpu.CompilerParams(
            dimension_semantics=("parallel",), vmem_limit_bytes=VMEM_LIMIT),
    )(*([tok] * (N_KV // HEADS_PER_TILE)), pe, w1, w2)


def _tile4(x):
    return jnp.concatenate([x] * GROUP, axis=1)


def _col_max(x):
    return jnp.max(x, axis=0, keepdims=True)


def _col_sum(x):
    return jnp.sum(x, axis=0, keepdims=True)


def _with_ones(v):
    ones = jnp.where(_iota((ONES_ROWS, v.shape[1]), 0) == 0, 1.0, 0.0).astype(v.dtype)
    return jnp.concatenate([v, ones], axis=0)


def _nsa_kernel(q_ref, gate_ref, kc_ref, vc_ref, ks_ref, kw_ref, vs_ref, vw_ref, o_ref,
                kse_ref, kwe_ref, imp_ref, *, seq):
    hp = pl.program_id(1)
    c = pl.program_id(2)
    t0 = c * Q_BLOCK
    n_cmp = kc_ref.shape[2]
    n_slc = seq // SLC_BLOCK
    cols = GROUP * Q_BLOCK
    n_chunks = seq // Q_BLOCK
    chunks_per_tile = KEY_TILE // Q_BLOCK
    heads = range(HEADS_PER_STEP)
    assert n_slc + HEAD_DIM == LANES and n_slc % 8 == 0

    @pl.when(c == 0)
    def _():
        r = _iota((KV_WIDTH, 2 * LANES), 0)
        ln = _iota((KV_WIDTH, 2 * LANES), 1)
        for hh in heads:
            h = hp * HEADS_PER_STEP + hh
            pick = ((r == h * HEAD_DIM + ln) & (ln < HEAD_DIM)).astype(BF16)

            def build_k(i, _, hh=hh, pick=pick):
                sl = pl.ds(pl.multiple_of(i * KEY_TILE, KEY_TILE), KEY_TILE)
                pos = i * KEY_TILE + _iota((KEY_TILE, 2 * LANES), 0)
                lane = _iota((KEY_TILE, 2 * LANES), 1)
                in_chunk = lane - LANES == (pos & (Q_BLOCK - 1))
                in_block = lane - HEAD_DIM == _div_pow2(pos, SLC_BLOCK)
                kse_ref[hh, sl, :] = (_dot(ks_ref[0, sl, :], pick)
                                      + jnp.where(in_chunk | in_block, 1.0, 0.0)).astype(BF16)
                kwe_ref[hh, sl, :] = (_dot(kw_ref[0, sl, :], pick)
                                      + jnp.where(in_chunk, 1.0, 0.0)).astype(BF16)
                return 0

            lax.fori_loop(0, seq // KEY_TILE, build_k, 0)
            kwe_ref[hh, pl.ds(seq, Q_BLOCK), :] = jnp.where(
                _iota((Q_BLOCK, 2 * LANES), 1) == HEAD_DIM, 1.0, 0.0).astype(BF16)

    def v_chunk(ref, hh, idx):
        return _with_ones(ref[0, idx, hh * HEAD_DIM:(hh + 1) * HEAD_DIM, :])

    def v_tile(hh, kt):
        return _with_ones(jnp.concatenate(
            [vs_ref[0, kt * chunks_per_tile + k, hh * HEAD_DIM:(hh + 1) * HEAD_DIM, :]
             for k in range(chunks_per_tile)], axis=1))

    def soft(scores, m):
        return [jnp.exp2(s - m).astype(BF16) for s in scores]

    kpos = _iota((Q_BLOCK, Q_BLOCK), 0)
    tpos = _iota((Q_BLOCK, Q_BLOCK), 1)
    causal = _tile4(jnp.where(kpos <= tpos, 0.0, NEG).astype(BF16))
    recent = _tile4(jnp.where(kpos > tpos, 0.0, NEG).astype(BF16))
    pad = jnp.zeros((HEAD_DIM, cols), BF16)
    dummy_row = jnp.where(_iota((HEAD_DIM, cols), 0) == 0, NEG, 0.0).astype(BF16)
    jj = _iota((n_slc, n_cmp), 0)
    ii = _iota((n_slc, n_cmp), 1)
    ov = (jnp.minimum(ii * CMP_STRIDE + CMP_BLOCK, jj * SLC_BLOCK + SLC_BLOCK)
          - jnp.maximum(ii * CMP_STRIDE, jj * SLC_BLOCK))
    ov = (jnp.maximum(ov, 0).astype(F32) * (1.0 / CMP_BLOCK)).astype(BF16)
    c_ok = (_iota((n_cmp, Q_BLOCK), 0) * CMP_STRIDE + (CMP_BLOCK - 1)
            <= t0 + _iota((n_cmp, Q_BLOCK), 1))
    c_bias = _tile4(jnp.where(c_ok, 0.0, NEG))
    any_ok = t0 + (_iota((1, cols), 1) & (Q_BLOCK - 1)) >= CMP_BLOCK - 1
    jb = _iota((n_slc, Q_BLOCK), 0)
    tb = _div_pow2(t0 + _iota((n_slc, Q_BLOCK), 1), SLC_BLOCK)
    forced = (jb == 0) | ((jb <= tb) & (jb > tb - N_LOCAL))
    sub = _iota((8, Q_BLOCK), 0)
    n_win = WIN_KEYS // Q_BLOCK
    first = c - (n_win - 1)
    k_chunk = [jnp.where(first + i >= 0, first + i, n_chunks) for i in range(n_win)]
    v_idx = [jnp.maximum(first + i, 0) for i in range(n_win)]
    k_rows = [pl.ds(pl.multiple_of(ch * Q_BLOCK, Q_BLOCK), Q_BLOCK) for ch in k_chunk]

    def before_sweep(hh):
        qt = q_ref[0, hh * GROUP_WIDTH:(hh + 1) * GROUP_WIDTH, :]
        wq = jnp.concatenate([qt[g * HEAD_DIM:(g + 1) * HEAD_DIM, :] for g in range(GROUP)], axis=1)

        sc = _dot(kc_ref[0, hh], jnp.concatenate([wq, pad], axis=0)) + c_bias
        pc = jnp.exp2(sc - _col_max(sc))
        pc = pc * jnp.where(any_ok, 1.0 / _col_sum(pc), 0.0)
        o_cmp = _dot(vc_ref[0, hh], pc.astype(BF16))

        pcs = pc[:, 0:Q_BLOCK]
        for g in range(1, GROUP):
            pcs = pcs + pc[:, g * Q_BLOCK:(g + 1) * Q_BLOCK]
        pcs_hi = pcs.astype(BF16)
        pcs_lo = (pcs - pcs_hi.astype(F32)).astype(BF16)
        imp = _dot(ov, pcs_hi) + _dot(ov, pcs_lo)
        imp = jnp.where(forced, FORCE, imp)
        imp = jnp.where(jb <= tb, imp, NEG)
        imp_ref[hh] = imp

        n_groups = n_slc // 8
        mine = [imp[8 * r:8 * r + 8, :] for r in range(n_groups)]
        ahead_count = [jnp.zeros((8, Q_BLOCK), F32)] * n_groups
        for j in range(n_slc):
            other = imp_ref[hh, j:j + 1, :]
            for r in range(n_groups):
                if 8 * r > j:
                    ahead = other >= mine[r]
                elif 8 * r + 7 <= j:
                    ahead = other > mine[r]
                else:
                    ahead = (other > mine[r]) | ((other == mine[r]) & (sub > j - 8 * r))
                ahead_count[r] = ahead_count[r] + jnp.where(ahead, 1.0, 0.0)
        sel = (jnp.concatenate(ahead_count, axis=0) < SLC_TOPK) & (imp > 0.5 * NEG)
        sel_bias = jnp.where(sel & (jb * SLC_BLOCK < t0), 0.0, NEG).astype(BF16)
        w_sel = jnp.concatenate([wq, _tile4(sel_bias)], axis=0)

        w_causal = jnp.concatenate([wq, pad, causal], axis=0)
        w_mid = jnp.concatenate([wq, dummy_row], axis=0)
        w_recent = jnp.concatenate([wq, dummy_row, recent], axis=0)
        sw = [_dot(kwe_ref[hh, k_rows[0], :], w_recent)]
        sw += [_dot(kwe_ref[hh, k_rows[i], 0:LANES], w_mid) for i in range(1, n_win - 1)]
        sw += [_dot(kwe_ref[hh, k_rows[n_win - 1], :], w_causal)]
        m_w = _col_max(sw[0])
        for s in sw[1:]:
            m_w = jnp.maximum(m_w, _col_max(s))
        o_win = None
        for idx, p in zip(v_idx, soft(sw, m_w)):
            part = _dot(v_chunk(vw_ref, hh, idx), p)
            o_win = part if o_win is None else o_win + part

        s_d = _dot(kse_ref[hh, pl.ds(pl.multiple_of(t0, Q_BLOCK), Q_BLOCK), :], w_causal)
        m_d = _col_max(s_d)
        acc_d = _dot(v_chunk(vs_ref, hh, c), soft([s_d], m_d)[0])
        return w_sel, o_cmp, o_win, (m_d, acc_d)

    pre = [before_sweep(hh) for hh in heads]

    def scores(hh, kt):
        k0 = pl.multiple_of(kt * KEY_TILE, KEY_TILE)
        return _dot(kse_ref[hh, pl.ds(k0, KEY_TILE), 0:LANES], pre[hh][0])

    def update(hh, s, kt, state):
        m, acc = state
        m_new = jnp.maximum(m, _col_max(s))
        p = jnp.exp2(s - m_new).astype(BF16)
        return m_new, jnp.exp2(m - m_new) * acc + _dot(v_tile(hh, kt), p)

    def tile_pair(i, states):
        s = [[scores(hh, 2 * i), scores(hh, 2 * i + 1)] for hh in heads]
        return tuple(update(hh, s[hh][1], 2 * i + 1, update(hh, s[hh][0], 2 * i, states[hh]))
                     for hh in heads)

    def last_tile(states):
        s = [scores(hh, n_tiles - 1) for hh in heads]
        return tuple(update(hh, s[hh], n_tiles - 1, states[hh]) for hh in heads)

    n_tiles = _div_pow2(t0 + KEY_TILE - 1, KEY_TILE)
    swept = lax.fori_loop(0, _div_pow2(n_tiles, 2), tile_pair, tuple(p[3] for p in pre))
    swept = lax.cond((n_tiles & 1) == 1, last_tile, lambda states: states, swept)

    for hh in heads:
        _, o_cmp, o_win, _ = pre[hh]
        o_slc = swept[hh][1]
        gates = gate_ref[0, hh * GATE_PAD:(hh + 1) * GATE_PAD, :]
        inv_ls = 1.0 / o_slc[HEAD_DIM:HEAD_DIM + 1, :]
        inv_lw = 1.0 / o_win[HEAD_DIM:HEAD_DIM + 1, :]
        pieces = []
        for g in range(GROUP):
            cs = slice(g * Q_BLOCK, (g + 1) * Q_BLOCK)
            row = g * N_GATES
            pieces.append(gates[row:row + 1, :] * o_cmp[:, cs]
                          + gates[row + 1:row + 2, :] * (o_slc[0:HEAD_DIM, cs] * inv_ls[:, cs])
                          + gates[row + 2:row + 3, :] * (o_win[0:HEAD_DIM, cs] * inv_lw[:, cs]))
        o_ref[0, hh * GROUP_WIDTH:(hh + 1) * GROUP_WIDTH, :] = jnp.concatenate(pieces, axis=0)


def _nsa(q_t, gates_t, kc, vc_t, ks, kw, vs_t, vw_t):
    bsz, _, seq = q_t.shape
    n_cmp = kc.shape[2]
    hps = HEADS_PER_STEP
    k_spec = pl.BlockSpec((1, seq, KV_WIDTH), lambda b, h, c: (b, 0, 0))
    v_spec = pl.BlockSpec((1, seq // Q_BLOCK, hps * HEAD_DIM, Q_BLOCK), lambda b, h, c: (b, 0, h, 0))
    return pl.pallas_call(
        functools.partial(_nsa_kernel, seq=seq),
        name="nsa",
        grid=(bsz, N_KV // hps, seq // Q_BLOCK),
        in_specs=[
            pl.BlockSpec((1, hps * GROUP_WIDTH, Q_BLOCK), lambda b, h, c: (b, h, c)),
            pl.BlockSpec((1, hps * GATE_PAD, Q_BLOCK), lambda b, h, c: (b, h, c)),
            pl.BlockSpec((1, hps, n_cmp, LANES), lambda b, h, c: (b, h, 0, 0)),
            pl.BlockSpec((1, hps, HEAD_DIM, n_cmp), lambda b, h, c: (b, h, 0, 0)),
            k_spec, k_spec, v_spec, v_spec,
        ],
        out_specs=pl.BlockSpec((1, hps * GROUP_WIDTH, Q_BLOCK), lambda b, h, c: (b, h, c)),
        out_shape=jax.ShapeDtypeStruct((bsz, ATTN_WIDTH, seq), F32),
        scratch_shapes=[
            pltpu.VMEM((hps, seq, 2 * LANES), BF16),
            pltpu.VMEM((hps, seq + Q_BLOCK, 2 * LANES), BF16),
            pltpu.VMEM((hps, seq // SLC_BLOCK, Q_BLOCK), F32),
        ],
        compiler_params=pltpu.CompilerParams(
            dimension_semantics=("parallel", "parallel", "arbitrary"),
            vmem_limit_bytes=VMEM_LIMIT),
    )(q_t, gates_t, kc, vc_t, ks, kw, vs_t, vw_t)


def _rope_tables(seq):
    pos = jnp.arange(seq, dtype=F32)
    inv = ROPE_THETA ** (-jnp.arange(0, HEAD_DIM, 2, dtype=F32) / HEAD_DIM)
    ang = pos[:, None] * inv[None, :]
    cos, sin = jnp.cos(ang), jnp.sin(ang)
    reps = LANES // HEAD_DIM
    return (jnp.tile(jnp.concatenate([cos, cos], axis=1), (1, reps)),
            jnp.tile(jnp.concatenate([-sin, sin], axis=1), (1, reps)))


def _compress_params(w1, w2, pe):
    eye = jnp.eye(HEADS_PER_TILE, dtype=w1.dtype)
    w1 = jnp.einsum('ldf,hk->lhdkf', w1, eye).reshape(CMP_BLOCK, LANES, HEADS_PER_TILE * CMP_HIDDEN)
    pe = jnp.tile(pe, (1, HEADS_PER_TILE)).reshape(CMP_BLOCK, 1, LANES)
    return pe, w1.astype(BF16), w2.astype(BF16)


def kernel(x, ffn1_norm, ffn1_wg, ffn1_wu, ffn1_wd, mix_norm, w_in, conv_w, cmp_k_w1, cmp_k_w2,
           cmp_k_pe, cmp_v_w1, cmp_v_w2, cmp_v_pe, conv_out_norm, attn_out_norm, w_out, ffn2_norm,
           ffn2_wg, ffn2_wu, ffn2_wd, final_norm):
    bsz, seq, d = x.shape
    assert CMP_BLOCK == 2 * CMP_STRIDE and seq % TOKEN_TILE == 0 and seq % KEY_TILE == 0
    assert x.shape[0] * seq % TOKEN_TILE == 0 and ffn1_wg.shape[0] == 1
    xt = x.reshape(bsz * seq, d)
    cos, sin = _rope_tables(seq)

    w = w_in[0]
    w_conv = w[:, :3 * CONV_WIDTH].astype(BF16)
    a0 = 3 * CONV_WIDTH
    kv0 = a0 + ATTN_WIDTH
    col = lambda k: w[:, kv0 + k * KV_WIDTH:kv0 + (k + 1) * KV_WIDTH]
    g0 = kv0 + 6 * KV_WIDTH
    wg = w[:, g0:].reshape(d, N_KV, GROUP * N_GATES)
    wg = jnp.pad(wg, ((0, 0), (0, 0), (0, GATE_PAD - GROUP * N_GATES))).reshape(d, N_KV * GATE_PAD)
    w_attn = jnp.concatenate([w[:, a0:kv0], col(0), col(1), col(2), col(4), wg], axis=1).astype(BF16)
    w_vt = jnp.concatenate([col(3), col(5)], axis=1).T.astype(BF16)

    x1 = _ffn1(xt, ffn1_norm, ffn1_wg[0].astype(BF16), ffn1_wu[0].astype(BF16),
               ffn1_wd[0].astype(BF16))
    yc = _conv_proj(x1, mix_norm, w_conv, conv_w[0], conv_out_norm, seq)
    q_t, k_c, v_c, k_s, k_w, vs_t, vw_t, gates_t = _attn_proj(x1, mix_norm, w_attn, w_vt, cos, sin, seq)

    pe_k, w1_k, w2_k = _compress_params(cmp_k_w1[0], cmp_k_w2[0], cmp_k_pe[0])
    pe_v, w1_v, w2_v = _compress_params(cmp_v_w1[0], cmp_v_w2[0], cmp_v_pe[0])
    kc = _compress(k_c, pe_k, w1_k, jnp.pad(w2_k, ((0, 0), (0, LANES - HEAD_DIM))), seq,
                   transposed=False)
    vc_t = _compress(v_c, pe_v, w1_v, w2_v.T, seq, transposed=True)

    per_batch = lambda a: a.reshape(bsz, seq, a.shape[-1])
    ya_t = _nsa(q_t, gates_t, kc, vc_t, per_batch(k_s), per_batch(k_w), vs_t, vw_t)

    out = _out_ffn2(x1, yc, ya_t, attn_out_norm,
                    w_out[0].astype(BF16), ffn2_norm, ffn2_wg[0].astype(BF16),
                    ffn2_wu[0].astype(BF16), ffn2_wd[0].astype(BF16), final_norm.reshape(1, d))
    return out.reshape(bsz, seq, d)
```

```python
import functools

import jax
import jax.numpy as jnp
from jax import lax
from jax.experimental import pallas as pl
from jax.experimental.pallas import tpu as pltpu

F32 = jnp.float32
BF16 = jnp.bfloat16

D_MODEL = 2048
D_FF = 5632
CONV_WIDTH = 1024
N_HEADS = 16
N_KV = 4
GROUP = N_HEADS // N_KV
HEAD_DIM = 64
ATTN_WIDTH = N_HEADS * HEAD_DIM
KV_WIDTH = N_KV * HEAD_DIM
CMP_BLOCK = 32
CMP_STRIDE = 16
CMP_HIDDEN = 256
SLC_BLOCK = 64
SLC_TOPK = 16
N_LOCAL = 2
WINDOW = 512
Q_BLOCK = 128
N_GATES = 3
ROPE_THETA = 10000.0
EPS = 1e-6
NEG = -1e30
FORCE = 1e9

LANES = 128
GROUP_WIDTH = GROUP * HEAD_DIM
GATE_PAD = LANES
TOKEN_TILE = 512
FF_TILE = 512
KEY_TILE = 512
WIN_KEYS = WINDOW + Q_BLOCK
ONES_ROWS = 16
HEADS_PER_STEP = 4
VMEM_LIMIT = 56 * 1024 * 1024
Q_SCALE = HEAD_DIM ** -0.5 * 1.4426950408889634


def _rms(x):
    return x * lax.rsqrt(jnp.mean(x * x, axis=-1, keepdims=True) + EPS)


def _dot(a, b):
    return jnp.dot(a, b, preferred_element_type=F32)


def _dot_nt(a, b):
    return lax.dot_general(a, b, (((1,), (1,)), ((), ())), preferred_element_type=F32)


def _div_pow2(x, n):
    assert n & (n - 1) == 0
    return jnp.right_shift(x, n.bit_length() - 1)


def _iota(shape, dim):
    return lax.broadcasted_iota(jnp.int32, shape, dim)


def _resident(shape):
    return pl.BlockSpec(shape, lambda *_: (0,) * len(shape), pipeline_mode=pl.Buffered(1))


def _ffn_accumulate(h_ref, wg_ref, wu_ref, wd_ref, acc_ref):
    h = h_ref[...]
    a = _dot(h, wg_ref[...])
    u = _dot(h, wu_ref[...])
    act = (jax.nn.silu(a) * u).astype(BF16)
    acc_ref[...] += _dot(act, wd_ref[...])


def _ffn1_kernel(x_ref, g_ref, wg_ref, wu_ref, wd_ref, o_ref, h_ref, acc_ref):
    j = pl.program_id(1)

    @pl.when(j == 0)
    def _():
        h_ref[...] = (_rms(x_ref[...]) * g_ref[...]).astype(BF16)
        acc_ref[...] = jnp.zeros_like(acc_ref)

    _ffn_accumulate(h_ref, wg_ref, wu_ref, wd_ref, acc_ref)

    @pl.when(j == pl.num_programs(1) - 1)
    def _():
        o_ref[...] = x_ref[...] + 0.5 * acc_ref[...]


def _ffn1(x, g, wg, wu, wd):
    t, d = x.shape
    f = wg.shape[1]
    return pl.pallas_call(
        _ffn1_kernel,
        name="ffn1",
        grid=(t // TOKEN_TILE, f // FF_TILE),
        in_specs=[
            pl.BlockSpec((TOKEN_TILE, d), lambda i, j: (i, 0)),
            pl.BlockSpec((1, d), lambda i, j: (0, 0)),
            pl.BlockSpec((d, FF_TILE), lambda i, j: (0, j)),
            pl.BlockSpec((d, FF_TILE), lambda i, j: (0, j)),
            pl.BlockSpec((FF_TILE, d), lambda i, j: (j, 0)),
        ],
        out_specs=pl.BlockSpec((TOKEN_TILE, d), lambda i, j: (i, 0)),
        out_shape=jax.ShapeDtypeStruct((t, d), F32),
        scratch_shapes=[pltpu.VMEM((TOKEN_TILE, d), BF16), pltpu.VMEM((TOKEN_TILE, d), F32)],
        compiler_params=pltpu.CompilerParams(
            dimension_semantics=("parallel", "arbitrary"), vmem_limit_bytes=VMEM_LIMIT),
    )(x, g, wg, wu, wd)


def _out_ffn2_kernel(x_ref, yc_ref, ya_ref, ga_ref, wo_ref, g_ref, wg_ref, wu_ref, wd_ref,
                     fin_ref, o_ref, h_ref, acc_ref):
    j = pl.program_id(1)

    @pl.when(j == 0)
    def _():
        ya = (_rms(ya_ref[0].T) * ga_ref[...]).astype(BF16)
        mix = _dot(yc_ref[...], wo_ref[0:CONV_WIDTH, :]) + _dot(ya, wo_ref[CONV_WIDTH:, :])
        x2 = x_ref[...] + mix
        o_ref[...] = x2
        h_ref[...] = (_rms(x2) * g_ref[...]).astype(BF16)
        acc_ref[...] = jnp.zeros_like(acc_ref)

    _ffn_accumulate(h_ref, wg_ref, wu_ref, wd_ref, acc_ref)

    @pl.when(j == pl.num_programs(1) - 1)
    def _():
        o_ref[...] = _rms(o_ref[...] + 0.5 * acc_ref[...]) * fin_ref[...]


def _out_ffn2(x1, yc, ya_t, ga, wo, g, wg, wu, wd, fin):
    t, d = x1.shape
    f = wg.shape[1]
    tm = TOKEN_TILE
    tps = ya_t.shape[2] // tm
    return pl.pallas_call(
        _out_ffn2_kernel,
        name="out_ffn2",
        grid=(t // tm, f // FF_TILE),
        in_specs=[
            pl.BlockSpec((tm, d), lambda i, j: (i, 0)),
            pl.BlockSpec((tm, CONV_WIDTH), lambda i, j: (i, 0)),
            pl.BlockSpec((1, ATTN_WIDTH, tm), lambda i, j: (i // tps, 0, i % tps)),
            pl.BlockSpec((1, ATTN_WIDTH), lambda i, j: (0, 0)),
            _resident(wo.shape),
            pl.BlockSpec((1, d), lambda i, j: (0, 0)),
            pl.BlockSpec((d, FF_TILE), lambda i, j: (0, j)),
            pl.BlockSpec((d, FF_TILE), lambda i, j: (0, j)),
            pl.BlockSpec((FF_TILE, d), lambda i, j: (j, 0)),
            pl.BlockSpec((1, d), lambda i, j: (0, 0)),
        ],
        out_specs=pl.BlockSpec((tm, d), lambda i, j: (i, 0)),
        out_shape=jax.ShapeDtypeStruct((t, d), F32),
        scratch_shapes=[pltpu.VMEM((tm, d), BF16), pltpu.VMEM((tm, d), F32)],
        compiler_params=pltpu.CompilerParams(
            dimension_semantics=("parallel", "arbitrary"), vmem_limit_bytes=VMEM_LIMIT),
    )(x1, yc, ya_t, ga, wo, g, wg, wu, wd, fin)


def _conv_proj_kernel(x_ref, g_ref, w_ref, cw_ref, gc_ref, o_ref, carry_ref, *, tiles_per_seq):
    i = pl.program_id(0)

    @pl.when(i % tiles_per_seq == 0)
    def _():
        carry_ref[...] = jnp.zeros_like(carry_ref)

    h = (_rms(x_ref[...]) * g_ref[...]).astype(BF16)
    c_h = _dot(h, w_ref[:, 0:CONV_WIDTH])
    c_b = _dot(h, w_ref[:, CONV_WIDTH:2 * CONV_WIDTH])
    c_c = _dot(h, w_ref[:, 2 * CONV_WIDTH:3 * CONV_WIDTH])
    u = c_c * c_h
    tm = u.shape[0]
    row = lax.broadcasted_iota(jnp.int32, u.shape, 0)
    prev1 = carry_ref[7:8, :]
    prev2 = carry_ref[6:7, :]
    u1 = jnp.where(row == 0, prev1, pltpu.roll(u, 1, 0))
    u2 = jnp.where(row == 0, prev2, jnp.where(row == 1, prev1, pltpu.roll(u, 2, 0)))
    carry_ref[...] = u[tm - 8:tm, :]
    y = c_b * (cw_ref[0:1, :] * u2 + cw_ref[1:2, :] * u1 + cw_ref[2:3, :] * u)
    o_ref[...] = (_rms(y) * gc_ref[...]).astype(BF16)


def _conv_proj(x1, g, w_conv, conv_w, gc, seq):
    t, d = x1.shape
    tm = TOKEN_TILE
    return pl.pallas_call(
        functools.partial(_conv_proj_kernel, tiles_per_seq=seq // tm),
        name="conv_proj",
        grid=(t // tm,),
        in_specs=[
            pl.BlockSpec((tm, d), lambda i: (i, 0)),
            pl.BlockSpec((1, d), lambda i: (0, 0)),
            _resident(w_conv.shape),
            pl.BlockSpec(conv_w.shape, lambda i: (0, 0)),
            pl.BlockSpec((1, CONV_WIDTH), lambda i: (0, 0)),
        ],
        out_specs=pl.BlockSpec((tm, CONV_WIDTH), lambda i: (i, 0)),
        out_shape=jax.ShapeDtypeStruct((t, CONV_WIDTH), BF16),
        scratch_shapes=[pltpu.VMEM((8, CONV_WIDTH), F32)],
        compiler_params=pltpu.CompilerParams(
            dimension_semantics=("arbitrary",), vmem_limit_bytes=VMEM_LIMIT),
    )(x1, g, w_conv, conv_w, gc)


def _rope(x, cos, sin_signed, first_half):
    outs = []
    for k in range(x.shape[1] // LANES):
        xc = x[:, k * LANES:(k + 1) * LANES]
        partner = jnp.where(first_half, pltpu.roll(xc, LANES - HEAD_DIM // 2, 1),
                            pltpu.roll(xc, HEAD_DIM // 2, 1))
        outs.append(xc * cos + partner * sin_signed)
    return outs


_Q0 = 0
_KC0 = ATTN_WIDTH
_VC0 = _KC0 + KV_WIDTH
_KS0 = _VC0 + KV_WIDTH
_KW0 = _KS0 + KV_WIDTH
_G0 = _KW0 + KV_WIDTH
ATTN_PROJ_WIDTH = _G0 + N_KV * GATE_PAD


def _attn_proj_kernel(x_ref, g_ref, w_ref, wvt_ref, cos_ref, sin_ref,
                      q_ref, kc_ref, vc_ref, ks_ref, kw_ref, vs_ref, vw_ref, gate_ref):
    h = (_rms(x_ref[...]) * g_ref[...]).astype(BF16)
    cos = cos_ref[...]
    sin = sin_ref[...]
    lane = lax.broadcasted_iota(jnp.int32, cos.shape, 1)
    first_half = (lane & (HEAD_DIM - 1)) < HEAD_DIM // 2

    def proj(c0, width):
        return _dot(h, w_ref[:, c0:c0 + width])

    def rope_to(ref, c0, width, dtype):
        for k, piece in enumerate(_rope(proj(c0, width), cos, sin, first_half)):
            ref[:, k * LANES:(k + 1) * LANES] = piece.astype(dtype)

    for k, piece in enumerate(_rope(proj(_Q0, ATTN_WIDTH), cos, sin, first_half)):
        q_ref[0, k * LANES:(k + 1) * LANES, :] = (piece * Q_SCALE).T.astype(BF16)
    rope_to(kc_ref, _KC0, KV_WIDTH, F32)
    vc_ref[...] = proj(_VC0, KV_WIDTH)
    rope_to(ks_ref, _KS0, KV_WIDTH, BF16)
    rope_to(kw_ref, _KW0, KV_WIDTH, BF16)
    gate_ref[0] = jax.nn.sigmoid(proj(_G0, N_KV * GATE_PAD)).T
    vt = _dot_nt(wvt_ref[...], h)
    for k in range(vt.shape[1] // Q_BLOCK):
        cs = slice(k * Q_BLOCK, (k + 1) * Q_BLOCK)
        vs_ref[0, k] = vt[0:KV_WIDTH, cs].astype(BF16)
        vw_ref[0, k] = vt[KV_WIDTH:2 * KV_WIDTH, cs].astype(BF16)


def _attn_proj(x1, g, w_attn, w_vt, cos, sin, seq):
    t, d = x1.shape
    tm = TOKEN_TILE
    tps = seq // tm
    bsz = t // seq
    cpt = tm // Q_BLOCK
    row = lambda w: pl.BlockSpec((tm, w), lambda i: (i, 0))
    col = lambda w: pl.BlockSpec((1, w, tm), lambda i: (i // tps, 0, i % tps))
    chunked = pl.BlockSpec((1, cpt, KV_WIDTH, Q_BLOCK), lambda i: (i // tps, i % tps, 0, 0))
    kv_dtypes = [F32, F32, BF16, BF16]
    out_specs = [col(ATTN_WIDTH)] + [row(KV_WIDTH)] * 4 + [chunked, chunked, col(N_KV * GATE_PAD)]
    out_shape = ([jax.ShapeDtypeStruct((bsz, ATTN_WIDTH, seq), BF16)]
                 + [jax.ShapeDtypeStruct((t, KV_WIDTH), dt) for dt in kv_dtypes]
                 + [jax.ShapeDtypeStruct((bsz, seq // Q_BLOCK, KV_WIDTH, Q_BLOCK), BF16)] * 2
                 + [jax.ShapeDtypeStruct((bsz, N_KV * GATE_PAD, seq), F32)])
    return pl.pallas_call(
        _attn_proj_kernel,
        name="attn_proj",
        grid=(t // tm,),
        in_specs=[
            row(d),
            pl.BlockSpec((1, d), lambda i: (0, 0)),
            _resident(w_attn.shape),
            _resident(w_vt.shape),
            pl.BlockSpec((tm, LANES), lambda i: (i % tps, 0)),
            pl.BlockSpec((tm, LANES), lambda i: (i % tps, 0)),
        ],
        out_specs=out_specs,
        out_shape=out_shape,
        compiler_params=pltpu.CompilerParams(
            dimension_semantics=("parallel",), vmem_limit_bytes=VMEM_LIMIT),
    )(x1, g, w_attn, w_vt, cos, sin)


HEADS_PER_TILE = LANES // HEAD_DIM


def _compress_kernel(*refs, transposed):
    tok_refs = refs[:N_KV // HEADS_PER_TILE]
    pe_ref, w1_ref, w2_ref, o_ref = refs[N_KV // HEADS_PER_TILE:]
    n = tok_refs[0].shape[0] // CMP_STRIDE
    for t, tok_ref in enumerate(tok_refs):
        first = jnp.zeros((n, HEADS_PER_TILE * CMP_HIDDEN), F32)
        second = jnp.zeros((n, HEADS_PER_TILE * CMP_HIDDEN), F32)
        for l in range(CMP_STRIDE):
            rows = tok_ref[pl.ds(l, n, stride=CMP_STRIDE), :]
            first = first + _dot((rows + pe_ref[l]).astype(BF16), w1_ref[l])
            second = second + _dot((rows + pe_ref[l + CMP_STRIDE]).astype(BF16),
                                   w1_ref[l + CMP_STRIDE])
        hid = jax.nn.gelu(first + pltpu.roll(second, n - 1, 0)).astype(BF16)
        for k in range(HEADS_PER_TILE):
            hid_h = hid[:, k * CMP_HIDDEN:(k + 1) * CMP_HIDDEN]
            h = t * HEADS_PER_TILE + k
            if transposed:
                o_ref[0, h] = _dot_nt(w2_ref[...], hid_h).astype(BF16)
            else:
                o_ref[0, h] = _dot(hid_h, w2_ref[...]).astype(BF16)


def _compress(tok, pe, w1, w2, seq, transposed):
    bsz = tok.shape[0] // seq
    n = seq // CMP_STRIDE
    out_block = (1, N_KV, HEAD_DIM, n) if transposed else (1, N_KV, n, LANES)
    lane_tile = lambda t: pl.BlockSpec((seq, LANES), lambda b: (b, t))
    return pl.pallas_call(
        functools.partial(_compress_kernel, transposed=transposed),
        name="compress_v" if transposed else "compress_k",
        grid=(bsz,),
        in_specs=[lane_tile(t) for t in range(N_KV // HEADS_PER_TILE)] + [
            _resident(pe.shape),
            _resident(w1.shape),
            _resident(w2.shape),
        ],
        out_specs=pl.BlockSpec(out_block, lambda b: (b, 0, 0, 0)),
        out_shape=jax.ShapeDtypeStruct((bsz,) + out_block[1:], BF16),
        compiler_params=pltpu.CompilerParams(
            dimension_semantics=("parallel",), vmem_limit_bytes=VMEM_LIMIT),
    )(*([tok] * (N_KV // HEADS_PER_TILE)), pe, w1, w2)


def _tile4(x):
    return jnp.concatenate([x] * GROUP, axis=1)


def _col_max(x):
    return jnp.max(x, axis=0, keepdims=True)


def _col_sum(x):
    return jnp.sum(x, axis=0, keepdims=True)


def _round_robin(gens, skew=1):
    results = [None] * len(gens)
    live = {}
    waiting = dict(enumerate(gens))
    rnd = 0
    while waiting or live:
        for k in [k for k in waiting if rnd >= k * skew]:
            live[k] = waiting.pop(k)
        for k in sorted(live):
            try:
                next(live[k])
            except StopIteration as done:
                results[k] = done.value
                del live[k]
        rnd += 1
    return results


def _with_ones(v):
    ones = jnp.where(_iota((ONES_ROWS, v.shape[1]), 0) == 0, 1.0, 0.0).astype(v.dtype)
    return jnp.concatenate([v, ones], axis=0)


def _nsa_kernel(q_ref, gate_ref, kc_ref, vc_ref, ks_ref, kw_ref, vs_ref, vw_ref, o_ref,
                kse_ref, kwe_ref, imp_ref, *, seq):
    hp = pl.program_id(1)
    c = pl.program_id(2)
    t0 = c * Q_BLOCK
    n_cmp = kc_ref.shape[2]
    n_slc = seq // SLC_BLOCK
    cols = GROUP * Q_BLOCK
    n_chunks = seq // Q_BLOCK
    chunks_per_tile = KEY_TILE // Q_BLOCK
    heads = range(HEADS_PER_STEP)
    assert n_slc + HEAD_DIM == LANES and n_slc % 8 == 0

    @pl.when(c == 0)
    def _():
        r = _iota((KV_WIDTH, 2 * LANES), 0)
        ln = _iota((KV_WIDTH, 2 * LANES), 1)
        for hh in heads:
            h = hp * HEADS_PER_STEP + hh
            pick = ((r == h * HEAD_DIM + ln) & (ln < HEAD_DIM)).astype(BF16)

            def build_k(i, _, hh=hh, pick=pick):
                sl = pl.ds(pl.multiple_of(i * KEY_TILE, KEY_TILE), KEY_TILE)
                pos = i * KEY_TILE + _iota((KEY_TILE, 2 * LANES), 0)
                lane = _iota((KEY_TILE, 2 * LANES), 1)
                in_chunk = lane - LANES == (pos & (Q_BLOCK - 1))
                in_block = lane - HEAD_DIM == _div_pow2(pos, SLC_BLOCK)
                kse_ref[hh, sl, :] = (_dot(ks_ref[0, sl, :], pick)
                                      + jnp.where(in_chunk | in_block, 1.0, 0.0)).astype(BF16)
                kwe_ref[hh, sl, :] = (_dot(kw_ref[0, sl, :], pick)
                                      + jnp.where(in_chunk, 1.0, 0.0)).astype(BF16)
                return 0

            lax.fori_loop(0, seq // KEY_TILE, build_k, 0)
            kwe_ref[hh, pl.ds(seq, Q_BLOCK), :] = jnp.where(
                _iota((Q_BLOCK, 2 * LANES), 1) == HEAD_DIM, 1.0, 0.0).astype(BF16)

    def v_chunk(ref, hh, idx):
        return _with_ones(ref[0, idx, hh * HEAD_DIM:(hh + 1) * HEAD_DIM, :])

    def v_tile(hh, kt):
        return _with_ones(jnp.concatenate(
            [vs_ref[0, kt * chunks_per_tile + k, hh * HEAD_DIM:(hh + 1) * HEAD_DIM, :]
             for k in range(chunks_per_tile)], axis=1))

    def soft(scores, m):
        return [jnp.exp2(s - m).astype(BF16) for s in scores]

    kpos = _iota((Q_BLOCK, Q_BLOCK), 0)
    tpos = _iota((Q_BLOCK, Q_BLOCK), 1)
    causal = _tile4(jnp.where(kpos <= tpos, 0.0, NEG).astype(BF16))
    recent = _tile4(jnp.where(kpos > tpos, 0.0, NEG).astype(BF16))
    pad = jnp.zeros((HEAD_DIM, cols), BF16)
    dummy_row = jnp.where(_iota((HEAD_DIM, cols), 0) == 0, NEG, 0.0).astype(BF16)
    jj = _iota((n_slc, n_cmp), 0)
    ii = _iota((n_slc, n_cmp), 1)
    ov = (jnp.minimum(ii * CMP_STRIDE + CMP_BLOCK, jj * SLC_BLOCK + SLC_BLOCK)
          - jnp.maximum(ii * CMP_STRIDE, jj * SLC_BLOCK))
    ov = (jnp.maximum(ov, 0).astype(F32) * (1.0 / CMP_BLOCK)).astype(BF16)
    c_ok = (_iota((n_cmp, Q_BLOCK), 0) * CMP_STRIDE + (CMP_BLOCK - 1)
            <= t0 + _iota((n_cmp, Q_BLOCK), 1))
    c_bias = _tile4(jnp.where(c_ok, 0.0, NEG))
    any_ok = t0 + (_iota((1, cols), 1) & (Q_BLOCK - 1)) >= CMP_BLOCK - 1
    jb = _iota((n_slc, Q_BLOCK), 0)
    tb = _div_pow2(t0 + _iota((n_slc, Q_BLOCK), 1), SLC_BLOCK)
    forced = (jb == 0) | ((jb <= tb) & (jb > tb - N_LOCAL))
    sub = _iota((8, Q_BLOCK), 0)
    n_win = WIN_KEYS // Q_BLOCK
    first = c - (n_win - 1)
    k_chunk = [jnp.where(first + i >= 0, first + i, n_chunks) for i in range(n_win)]
    v_idx = [jnp.maximum(first + i, 0) for i in range(n_win)]
    k_rows = [pl.ds(pl.multiple_of(ch * Q_BLOCK, Q_BLOCK), Q_BLOCK) for ch in k_chunk]

    def before_sweep(hh):
        qt = q_ref[0, hh * GROUP_WIDTH:(hh + 1) * GROUP_WIDTH, :]
        wq = jnp.concatenate([qt[g * HEAD_DIM:(g + 1) * HEAD_DIM, :] for g in range(GROUP)], axis=1)

        sc = _dot(kc_ref[0, hh], jnp.concatenate([wq, pad], axis=0)) + c_bias
        yield
        pc = jnp.exp2(sc - _col_max(sc))
        pc = pc * jnp.where(any_ok, 1.0 / _col_sum(pc), 0.0)
        yield
        o_cmp = _dot(vc_ref[0, hh], pc.astype(BF16))

        pcs = pc[:, 0:Q_BLOCK]
        for g in range(1, GROUP):
            pcs = pcs + pc[:, g * Q_BLOCK:(g + 1) * Q_BLOCK]
        pcs_hi = pcs.astype(BF16)
        pcs_lo = (pcs - pcs_hi.astype(F32)).astype(BF16)
        imp = _dot(ov, pcs_hi) + _dot(ov, pcs_lo)
        yield
        imp = jnp.where(forced, FORCE, imp)
        imp = jnp.where(jb <= tb, imp, NEG)
        imp_ref[hh] = imp

        w_causal = jnp.concatenate([wq, pad, causal], axis=0)
        w_mid = jnp.concatenate([wq, dummy_row], axis=0)
        w_recent = jnp.concatenate([wq, dummy_row, recent], axis=0)
        sw = [_dot(kwe_ref[hh, k_rows[0], :], w_recent)]
        sw += [_dot(kwe_ref[hh, k_rows[i], 0:LANES], w_mid) for i in range(1, n_win - 1)]
        sw += [_dot(kwe_ref[hh, k_rows[n_win - 1], :], w_causal)]
        s_d = _dot(kse_ref[hh, pl.ds(pl.multiple_of(t0, Q_BLOCK), Q_BLOCK), :], w_causal)
        yield

        n_groups = n_slc // 8
        mine = [imp[8 * r:8 * r + 8, :] for r in range(n_groups)]
        ahead_count = [jnp.zeros((8, Q_BLOCK), F32)] * n_groups
        for j in range(n_slc):
            other = imp_ref[hh, j:j + 1, :]
            for r in range(n_groups):
                if 8 * r > j:
                    ahead = other >= mine[r]
                elif 8 * r + 7 <= j:
                    ahead = other > mine[r]
                else:
                    ahead = (other > mine[r]) | ((other == mine[r]) & (sub > j - 8 * r))
                ahead_count[r] = ahead_count[r] + jnp.where(ahead, 1.0, 0.0)
            if j % 8 == 7:
                yield
        sel = (jnp.concatenate(ahead_count, axis=0) < SLC_TOPK) & (imp > 0.5 * NEG)
        sel_bias = jnp.where(sel & (jb * SLC_BLOCK < t0), 0.0, NEG).astype(BF16)
        w_sel = jnp.concatenate([wq, _tile4(sel_bias)], axis=0)

        m_w = _col_max(sw[0])
        for s in sw[1:]:
            m_w = jnp.maximum(m_w, _col_max(s))
        yield
        p_w = soft(sw, m_w)
        yield
        o_win = None
        for idx, p in zip(v_idx, p_w):
            part = _dot(v_chunk(vw_ref, hh, idx), p)
            o_win = part if o_win is None else o_win + part
        m_d = _col_max(s_d)
        acc_d = _dot(v_chunk(vs_ref, hh, c), soft([s_d], m_d)[0])
        return w_sel, o_cmp, o_win, (m_d, acc_d)

    pre = _round_robin([before_sweep(hh) for hh in heads])

    def scores(hh, kt):
        k0 = pl.multiple_of(kt * KEY_TILE, KEY_TILE)
        return _dot(kse_ref[hh, pl.ds(k0, KEY_TILE), 0:LANES], pre[hh][0])

    def sweep(items, states, lookahead=2):
        states = list(states)
        s, m_new, p = {}, {}, {}

        def issue(n):
            s[n] = scores(*items[n])

        def value_matmul(n):
            hh, kt = items[n]
            m, acc = states[hh]
            states[hh] = m_new[n], jnp.exp2(m - m_new[n]) * acc + _dot(v_tile(hh, kt), p[n])

        for n in range(min(lookahead, len(items))):
            issue(n)
        for n, (hh, _) in enumerate(items):
            m_new[n] = jnp.maximum(states[hh][0], _col_max(s[n]))
            if n + lookahead < len(items):
                issue(n + lookahead)
            p[n] = jnp.exp2(s[n] - m_new[n]).astype(BF16)
            if n >= 1:
                value_matmul(n - 1)
        value_matmul(len(items) - 1)
        return tuple(states)

    def tile_pair(i, states):
        return sweep([(hh, 2 * i + k) for k in range(2) for hh in heads], states)

    def last_tile(states):
        return sweep([(hh, n_tiles - 1) for hh in heads], states)

    n_tiles = _div_pow2(t0 + KEY_TILE - 1, KEY_TILE)
    swept = lax.fori_loop(0, _div_pow2(n_tiles, 2), tile_pair, tuple(p[3] for p in pre))
    swept = lax.cond((n_tiles & 1) == 1, last_tile, lambda states: states, swept)

    for hh in heads:
        _, o_cmp, o_win, _ = pre[hh]
        o_slc = swept[hh][1]
        gates = gate_ref[0, hh * GATE_PAD:(hh + 1) * GATE_PAD, :]
        inv_ls = 1.0 / o_slc[HEAD_DIM:HEAD_DIM + 1, :]
        inv_lw = 1.0 / o_win[HEAD_DIM:HEAD_DIM + 1, :]
        pieces = []
        for g in range(GROUP):
            cs = slice(g * Q_BLOCK, (g + 1) * Q_BLOCK)
            row = g * N_GATES
            pieces.append(gates[row:row + 1, :] * o_cmp[:, cs]
                          + gates[row + 1:row + 2, :] * (o_slc[0:HEAD_DIM, cs] * inv_ls[:, cs])
                          + gates[row + 2:row + 3, :] * (o_win[0:HEAD_DIM, cs] * inv_lw[:, cs]))
        o_ref[0, hh * GROUP_WIDTH:(hh + 1) * GROUP_WIDTH, :] = jnp.concatenate(pieces, axis=0)


def _nsa(q_t, gates_t, kc, vc_t, ks, kw, vs_t, vw_t):
    bsz, _, seq = q_t.shape
    n_cmp = kc.shape[2]
    hps = HEADS_PER_STEP
    k_spec = pl.BlockSpec((1, seq, KV_WIDTH), lambda b, h, c: (b, 0, 0))
    v_spec = pl.BlockSpec((1, seq // Q_BLOCK, hps * HEAD_DIM, Q_BLOCK), lambda b, h, c: (b, 0, h, 0))
    return pl.pallas_call(
        functools.partial(_nsa_kernel, seq=seq),
        name="nsa",
        grid=(bsz, N_KV // hps, seq // Q_BLOCK),
        in_specs=[
            pl.BlockSpec((1, hps * GROUP_WIDTH, Q_BLOCK), lambda b, h, c: (b, h, c)),
            pl.BlockSpec((1, hps * GATE_PAD, Q_BLOCK), lambda b, h, c: (b, h, c)),
            pl.BlockSpec((1, hps, n_cmp, LANES), lambda b, h, c: (b, h, 0, 0)),
            pl.BlockSpec((1, hps, HEAD_DIM, n_cmp), lambda b, h, c: (b, h, 0, 0)),
            k_spec, k_spec, v_spec, v_spec,
        ],
        out_specs=pl.BlockSpec((1, hps * GROUP_WIDTH, Q_BLOCK), lambda b, h, c: (b, h, c)),
        out_shape=jax.ShapeDtypeStruct((bsz, ATTN_WIDTH, seq), F32),
        scratch_shapes=[
            pltpu.VMEM((hps, seq, 2 * LANES), BF16),
            pltpu.VMEM((hps, seq + Q_BLOCK, 2 * LANES), BF16),
            pltpu.VMEM((hps, seq // SLC_BLOCK, Q_BLOCK), F32),
        ],
        compiler_params=pltpu.CompilerParams(
            dimension_semantics=("parallel", "parallel", "arbitrary"),
            vmem_limit_bytes=VMEM_LIMIT),
    )(q_t, gates_t, kc, vc_t, ks, kw, vs_t, vw_t)


def _rope_tables(seq):
    pos = jnp.arange(seq, dtype=F32)
    inv = ROPE_THETA ** (-jnp.arange(0, HEAD_DIM, 2, dtype=F32) / HEAD_DIM)
    ang = pos[:, None] * inv[None, :]
    cos, sin = jnp.cos(ang), jnp.sin(ang)
    reps = LANES // HEAD_DIM
    return (jnp.tile(jnp.concatenate([cos, cos], axis=1), (1, reps)),
            jnp.tile(jnp.concatenate([-sin, sin], axis=1), (1, reps)))


def _compress_params(w1, w2, pe):
    eye = jnp.eye(HEADS_PER_TILE, dtype=w1.dtype)
    w1 = jnp.einsum('ldf,hk->lhdkf', w1, eye).reshape(CMP_BLOCK, LANES, HEADS_PER_TILE * CMP_HIDDEN)
    pe = jnp.tile(pe, (1, HEADS_PER_TILE)).reshape(CMP_BLOCK, 1, LANES)
    return pe, w1.astype(BF16), w2.astype(BF16)


def kernel(x, ffn1_norm, ffn1_wg, ffn1_wu, ffn1_wd, mix_norm, w_in, conv_w, cmp_k_w1, cmp_k_w2,
           cmp_k_pe, cmp_v_w1, cmp_v_w2, cmp_v_pe, conv_out_norm, attn_out_norm, w_out, ffn2_norm,
           ffn2_wg, ffn2_wu, ffn2_wd, final_norm):
    bsz, seq, d = x.shape
    assert CMP_BLOCK == 2 * CMP_STRIDE and seq % TOKEN_TILE == 0 and seq % KEY_TILE == 0
    assert x.shape[0] * seq % TOKEN_TILE == 0 and ffn1_wg.shape[0] == 1
    xt = x.reshape(bsz * seq, d)
    cos, sin = _rope_tables(seq)

    w = w_in[0]
    w_conv = w[:, :3 * CONV_WIDTH].astype(BF16)
    a0 = 3 * CONV_WIDTH
    kv0 = a0 + ATTN_WIDTH
    col = lambda k: w[:, kv0 + k * KV_WIDTH:kv0 + (k + 1) * KV_WIDTH]
    g0 = kv0 + 6 * KV_WIDTH
    wg = w[:, g0:].reshape(d, N_KV, GROUP * N_GATES)
    wg = jnp.pad(wg, ((0, 0), (0, 0), (0, GATE_PAD - GROUP * N_GATES))).reshape(d, N_KV * GATE_PAD)
    w_attn = jnp.concatenate([w[:, a0:kv0], col(0), col(1), col(2), col(4), wg], axis=1).astype(BF16)
    w_vt = jnp.concatenate([col(3), col(5)], axis=1).T.astype(BF16)

    x1 = _ffn1(xt, ffn1_norm, ffn1_wg[0].astype(BF16), ffn1_wu[0].astype(BF16),
               ffn1_wd[0].astype(BF16))
    yc = _conv_proj(x1, mix_norm, w_conv, conv_w[0], conv_out_norm, seq)
    q_t, k_c, v_c, k_s, k_w, vs_t, vw_t, gates_t = _attn_proj(x1, mix_norm, w_attn, w_vt, cos, sin, seq)

    pe_k, w1_k, w2_k = _compress_params(cmp_k_w1[0], cmp_k_w2[0], cmp_k_pe[0])
    pe_v, w1_v, w2_v = _compress_params(cmp_v_w1[0], cmp_v_w2[0], cmp_v_pe[0])
    kc = _compress(k_c, pe_k, w1_k, jnp.pad(w2_k, ((0, 0), (0, LANES - HEAD_DIM))), seq,
                   transposed=False)
    vc_t = _compress(v_c, pe_v, w1_v, w2_v.T, seq, transposed=True)

    per_batch = lambda a: a.reshape(bsz, seq, a.shape[-1])
    ya_t = _nsa(q_t, gates_t, kc, vc_t, per_batch(k_s), per_batch(k_w), vs_t, vw_t)

    out = _out_ffn2(x1, yc, ya_t, attn_out_norm,
                    w_out[0].astype(BF16), ffn2_norm, ffn2_wg[0].astype(BF16),
                    ffn2_wu[0].astype(BF16), ffn2_wd[0].astype(BF16), final_norm.reshape(1, d))
    return out.reshape(bsz, seq, d)
```

```python
import functools

import jax
import jax.numpy as jnp
from jax import lax
from jax.experimental import pallas as pl
from jax.experimental.pallas import tpu as pltpu

F32 = jnp.float32
BF16 = jnp.bfloat16

D_MODEL = 2048
D_FF = 5632
CONV_WIDTH = 1024
N_HEADS = 16
N_KV = 4
GROUP = N_HEADS // N_KV
HEAD_DIM = 64
ATTN_WIDTH = N_HEADS * HEAD_DIM
KV_WIDTH = N_KV * HEAD_DIM
CMP_BLOCK = 32
CMP_STRIDE = 16
CMP_HIDDEN = 256
SLC_BLOCK = 64
SLC_TOPK = 16
N_LOCAL = 2
WINDOW = 512
Q_BLOCK = 128
N_GATES = 3
ROPE_THETA = 10000.0
EPS = 1e-6
NEG = -1e30
FORCE = 1e9

LANES = 128
GROUP_WIDTH = GROUP * HEAD_DIM
GATE_PAD = LANES
TOKEN_TILE = 512
FF_TILE = 512
KEY_TILE = 512
WIN_KEYS = WINDOW + Q_BLOCK
ONES_ROWS = 16
HEADS_PER_STEP = 4
VMEM_LIMIT = 56 * 1024 * 1024
Q_SCALE = HEAD_DIM ** -0.5 * 1.4426950408889634


def _rms(x):
    return x * lax.rsqrt(jnp.mean(x * x, axis=-1, keepdims=True) + EPS)


def _dot(a, b):
    return jnp.dot(a, b, preferred_element_type=F32)


def _dot_nt(a, b):
    return lax.dot_general(a, b, (((1,), (1,)), ((), ())), preferred_element_type=F32)


def _div_pow2(x, n):
    assert n & (n - 1) == 0
    return jnp.right_shift(x, n.bit_length() - 1)


def _iota(shape, dim):
    return lax.broadcasted_iota(jnp.int32, shape, dim)


def _resident(shape):
    return pl.BlockSpec(shape, lambda *_: (0,) * len(shape), pipeline_mode=pl.Buffered(1))


def _round_robin(gens, skew=1):
    results = [None] * len(gens)
    live = {}
    waiting = dict(enumerate(gens))
    rnd = 0
    while waiting or live:
        for k in [k for k in waiting if rnd >= k * skew]:
            live[k] = waiting.pop(k)
        for k in sorted(live):
            try:
                next(live[k])
            except StopIteration as done:
                results[k] = done.value
                del live[k]
        rnd += 1
    return results


def _ffn_sweep(make_h, finish, wg_ref, wu_ref, wd_ref, h_ref, acc_ref):
    j = pl.program_id(1)
    last = pl.num_programs(1) - 1
    tm = h_ref.shape[0]
    halves = [slice(0, tm // 2), slice(tm // 2, tm)]

    def hidden(h):
        a = _dot(h, wg_ref[...])
        u = _dot(h, wu_ref[...])
        return (jax.nn.silu(a) * u).astype(BF16)

    @pl.when(j == 0)
    def _():
        def first(rows):
            h = yield from make_h(rows)
            h_ref[rows, :] = h
            yield
            act = hidden(h)
            yield
            acc_ref[rows, :] = _dot(act, wd_ref[...])

        _round_robin([first(rows) for rows in halves])

    @pl.when((j > 0) & (j < last))
    def _():
        acc_ref[...] += _dot(hidden(h_ref[...]), wd_ref[...])

    @pl.when(j == last)
    def _():
        def final(rows):
            act = hidden(h_ref[rows, :])
            yield
            finish(rows, acc_ref[rows, :] + _dot(act, wd_ref[...]))

        _round_robin([final(rows) for rows in halves])


def _ffn1_kernel(x_ref, g_ref, wg_ref, wu_ref, wd_ref, o_ref, h_ref, acc_ref):
    def make_h(rows):
        return (_rms(x_ref[rows, :]) * g_ref[...]).astype(BF16)
        yield

    def finish(rows, acc):
        o_ref[rows, :] = x_ref[rows, :] + 0.5 * acc

    _ffn_sweep(make_h, finish, wg_ref, wu_ref, wd_ref, h_ref, acc_ref)


def _ffn1(x, g, wg, wu, wd):
    t, d = x.shape
    f = wg.shape[1]
    return pl.pallas_call(
        _ffn1_kernel,
        name="ffn1",
        grid=(t // TOKEN_TILE, f // FF_TILE),
        in_specs=[
            pl.BlockSpec((TOKEN_TILE, d), lambda i, j: (i, 0)),
            pl.BlockSpec((1, d), lambda i, j: (0, 0)),
            pl.BlockSpec((d, FF_TILE), lambda i, j: (0, j)),
            pl.BlockSpec((d, FF_TILE), lambda i, j: (0, j)),
            pl.BlockSpec((FF_TILE, d), lambda i, j: (j, 0)),
        ],
        out_specs=pl.BlockSpec((TOKEN_TILE, d), lambda i, j: (i, 0)),
        out_shape=jax.ShapeDtypeStruct((t, d), F32),
        scratch_shapes=[pltpu.VMEM((TOKEN_TILE, d), BF16), pltpu.VMEM((TOKEN_TILE, d), F32)],
        compiler_params=pltpu.CompilerParams(
            dimension_semantics=("parallel", "arbitrary"), vmem_limit_bytes=VMEM_LIMIT),
    )(x, g, wg, wu, wd)


def _out_ffn2_kernel(x_ref, yc_ref, ya_ref, ga_ref, wo_ref, g_ref, wg_ref, wu_ref, wd_ref,
                     fin_ref, o_ref, h_ref, acc_ref):
    def make_h(rows):
        ya = (_rms(ya_ref[0, :, rows].T) * ga_ref[...]).astype(BF16)
        yield
        mix = _dot(yc_ref[rows, :], wo_ref[0:CONV_WIDTH, :]) + _dot(ya, wo_ref[CONV_WIDTH:, :])
        yield
        x2 = x_ref[rows, :] + mix
        o_ref[rows, :] = x2
        return (_rms(x2) * g_ref[...]).astype(BF16)

    def finish(rows, acc):
        o_ref[rows, :] = _rms(o_ref[rows, :] + 0.5 * acc) * fin_ref[...]

    _ffn_sweep(make_h, finish, wg_ref, wu_ref, wd_ref, h_ref, acc_ref)


def _out_ffn2(x1, yc, ya_t, ga, wo, g, wg, wu, wd, fin):
    t, d = x1.shape
    f = wg.shape[1]
    tm = TOKEN_TILE
    tps = ya_t.shape[2] // tm
    return pl.pallas_call(
        _out_ffn2_kernel,
        name="out_ffn2",
        grid=(t // tm, f // FF_TILE),
        in_specs=[
            pl.BlockSpec((tm, d), lambda i, j: (i, 0)),
            pl.BlockSpec((tm, CONV_WIDTH), lambda i, j: (i, 0)),
            pl.BlockSpec((1, ATTN_WIDTH, tm), lambda i, j: (i // tps, 0, i % tps)),
            pl.BlockSpec((1, ATTN_WIDTH), lambda i, j: (0, 0)),
            _resident(wo.shape),
            pl.BlockSpec((1, d), lambda i, j: (0, 0)),
            pl.BlockSpec((d, FF_TILE), lambda i, j: (0, j)),
            pl.BlockSpec((d, FF_TILE), lambda i, j: (0, j)),
            pl.BlockSpec((FF_TILE, d), lambda i, j: (j, 0)),
            pl.BlockSpec((1, d), lambda i, j: (0, 0)),
        ],
        out_specs=pl.BlockSpec((tm, d), lambda i, j: (i, 0)),
        out_shape=jax.ShapeDtypeStruct((t, d), F32),
        scratch_shapes=[pltpu.VMEM((tm, d), BF16), pltpu.VMEM((tm, d), F32)],
        compiler_params=pltpu.CompilerParams(
            dimension_semantics=("parallel", "arbitrary"), vmem_limit_bytes=VMEM_LIMIT),
    )(x1, yc, ya_t, ga, wo, g, wg, wu, wd, fin)


def _conv_proj_kernel(x_ref, g_ref, w_ref, cw_ref, gc_ref, o_ref, carry_ref, *, tiles_per_seq):
    i = pl.program_id(0)

    @pl.when(i % tiles_per_seq == 0)
    def _():
        carry_ref[...] = jnp.zeros_like(carry_ref)

    tm = x_ref.shape[0]
    half = tm // 2
    tails = {-1: carry_ref[...]}

    def one_half(k):
        rows = slice(k * half, (k + 1) * half)
        h = (_rms(x_ref[rows, :]) * g_ref[...]).astype(BF16)
        yield
        c_h = _dot(h, w_ref[:, 0:CONV_WIDTH])
        c_c = _dot(h, w_ref[:, 2 * CONV_WIDTH:3 * CONV_WIDTH])
        u = c_c * c_h
        tails[k] = u[half - 8:half, :]
        c_b = _dot(h, w_ref[:, CONV_WIDTH:2 * CONV_WIDTH])
        yield
        row = lax.broadcasted_iota(jnp.int32, u.shape, 0)
        prev1 = tails[k - 1][7:8, :]
        prev2 = tails[k - 1][6:7, :]
        u1 = jnp.where(row == 0, prev1, pltpu.roll(u, 1, 0))
        u2 = jnp.where(row == 0, prev2, jnp.where(row == 1, prev1, pltpu.roll(u, 2, 0)))
        y = c_b * (cw_ref[0:1, :] * u2 + cw_ref[1:2, :] * u1 + cw_ref[2:3, :] * u)
        o_ref[rows, :] = (_rms(y) * gc_ref[...]).astype(BF16)

    _round_robin([one_half(0), one_half(1)])
    carry_ref[...] = tails[1]


def _conv_proj(x1, g, w_conv, conv_w, gc, seq):
    t, d = x1.shape
    tm = TOKEN_TILE
    return pl.pallas_call(
        functools.partial(_conv_proj_kernel, tiles_per_seq=seq // tm),
        name="conv_proj",
        grid=(t // tm,),
        in_specs=[
            pl.BlockSpec((tm, d), lambda i: (i, 0)),
            pl.BlockSpec((1, d), lambda i: (0, 0)),
            _resident(w_conv.shape),
            pl.BlockSpec(conv_w.shape, lambda i: (0, 0)),
            pl.BlockSpec((1, CONV_WIDTH), lambda i: (0, 0)),
        ],
        out_specs=pl.BlockSpec((tm, CONV_WIDTH), lambda i: (i, 0)),
        out_shape=jax.ShapeDtypeStruct((t, CONV_WIDTH), BF16),
        scratch_shapes=[pltpu.VMEM((8, CONV_WIDTH), F32)],
        compiler_params=pltpu.CompilerParams(
            dimension_semantics=("arbitrary",), vmem_limit_bytes=VMEM_LIMIT),
    )(x1, g, w_conv, conv_w, gc)


def _rope(x, cos, sin_signed, first_half):
    outs = []
    for k in range(x.shape[1] // LANES):
        xc = x[:, k * LANES:(k + 1) * LANES]
        partner = jnp.where(first_half, pltpu.roll(xc, LANES - HEAD_DIM // 2, 1),
                            pltpu.roll(xc, HEAD_DIM // 2, 1))
        outs.append(xc * cos + partner * sin_signed)
    return outs


_Q0 = 0
_KC0 = ATTN_WIDTH
_VC0 = _KC0 + KV_WIDTH
_KS0 = _VC0 + KV_WIDTH
_KW0 = _KS0 + KV_WIDTH
_G0 = _KW0 + KV_WIDTH
ATTN_PROJ_WIDTH = _G0 + N_KV * GATE_PAD


def _attn_proj_kernel(x_ref, g_ref, w_ref, wvt_ref, cos_ref, sin_ref,
                      q_ref, kc_ref, vc_ref, ks_ref, kw_ref, vs_ref, vw_ref, gate_ref):
    tm = x_ref.shape[0]
    half = tm // 2
    lane = lax.broadcasted_iota(jnp.int32, (half, LANES), 1)
    first_half = (lane & (HEAD_DIM - 1)) < HEAD_DIM // 2

    def one_half(k):
        rows = slice(k * half, (k + 1) * half)
        h = (_rms(x_ref[rows, :]) * g_ref[...]).astype(BF16)
        cos = cos_ref[rows, :]
        sin = sin_ref[rows, :]

        def proj(c0, width):
            return _dot(h, w_ref[:, c0:c0 + width])

        def rope_to(ref, c0, width, dtype):
            for n, piece in enumerate(_rope(proj(c0, width), cos, sin, first_half)):
                ref[rows, n * LANES:(n + 1) * LANES] = piece.astype(dtype)

        yield
        for n, piece in enumerate(_rope(proj(_Q0, ATTN_WIDTH), cos, sin, first_half)):
            q_ref[0, n * LANES:(n + 1) * LANES, rows] = (piece * Q_SCALE).T.astype(BF16)
        yield
        rope_to(kc_ref, _KC0, KV_WIDTH, F32)
        vc_ref[rows, :] = proj(_VC0, KV_WIDTH)
        rope_to(ks_ref, _KS0, KV_WIDTH, BF16)
        rope_to(kw_ref, _KW0, KV_WIDTH, BF16)
        yield
        gate_ref[0, :, rows] = jax.nn.sigmoid(proj(_G0, N_KV * GATE_PAD)).T
        vt = _dot_nt(wvt_ref[...], h)
        chunks = half // Q_BLOCK
        for n in range(chunks):
            cs = slice(n * Q_BLOCK, (n + 1) * Q_BLOCK)
            vs_ref[0, k * chunks + n] = vt[0:KV_WIDTH, cs].astype(BF16)
            vw_ref[0, k * chunks + n] = vt[KV_WIDTH:2 * KV_WIDTH, cs].astype(BF16)

    _round_robin([one_half(0), one_half(1)])


def _attn_proj(x1, g, w_attn, w_vt, cos, sin, seq):
    t, d = x1.shape
    tm = TOKEN_TILE
    tps = seq // tm
    bsz = t // seq
    cpt = tm // Q_BLOCK
    row = lambda w: pl.BlockSpec((tm, w), lambda i: (i, 0))
    col = lambda w: pl.BlockSpec((1, w, tm), lambda i: (i // tps, 0, i % tps))
    chunked = pl.BlockSpec((1, cpt, KV_WIDTH, Q_BLOCK), lambda i: (i // tps, i % tps, 0, 0))
    kv_dtypes = [F32, F32, BF16, BF16]
    out_specs = [col(ATTN_WIDTH)] + [row(KV_WIDTH)] * 4 + [chunked, chunked, col(N_KV * GATE_PAD)]
    out_shape = ([jax.ShapeDtypeStruct((bsz, ATTN_WIDTH, seq), BF16)]
                 + [jax.ShapeDtypeStruct((t, KV_WIDTH), dt) for dt in kv_dtypes]
                 + [jax.ShapeDtypeStruct((bsz, seq // Q_BLOCK, KV_WIDTH, Q_BLOCK), BF16)] * 2
                 + [jax.ShapeDtypeStruct((bsz, N_KV * GATE_PAD, seq), F32)])
    return pl.pallas_call(
        _attn_proj_kernel,
        name="attn_proj",
        grid=(t // tm,),
        in_specs=[
            row(d),
            pl.BlockSpec((1, d), lambda i: (0, 0)),
            _resident(w_attn.shape),
            _resident(w_vt.shape),
            pl.BlockSpec((tm, LANES), lambda i: (i % tps, 0)),
            pl.BlockSpec((tm, LANES), lambda i: (i % tps, 0)),
        ],
        out_specs=out_specs,
        out_shape=out_shape,
        compiler_params=pltpu.CompilerParams(
            dimension_semantics=("parallel",), vmem_limit_bytes=VMEM_LIMIT),
    )(x1, g, w_attn, w_vt, cos, sin)


HEADS_PER_TILE = LANES // HEAD_DIM


def _compress_kernel(*refs, transposed):
    tok_refs = refs[:N_KV // HEADS_PER_TILE]
    pe_ref, w1_ref, w2_ref, o_ref = refs[N_KV // HEADS_PER_TILE:]
    n = tok_refs[0].shape[0] // CMP_STRIDE
    for t, tok_ref in enumerate(tok_refs):
        first = jnp.zeros((n, HEADS_PER_TILE * CMP_HIDDEN), F32)
        second = jnp.zeros((n, HEADS_PER_TILE * CMP_HIDDEN), F32)
        for l in range(CMP_STRIDE):
            rows = tok_ref[pl.ds(l, n, stride=CMP_STRIDE), :]
            first = first + _dot((rows + pe_ref[l]).astype(BF16), w1_ref[l])
            second = second + _dot((rows + pe_ref[l + CMP_STRIDE]).astype(BF16),
                                   w1_ref[l + CMP_STRIDE])
        hid = jax.nn.gelu(first + pltpu.roll(second, n - 1, 0)).astype(BF16)
        for k in range(HEADS_PER_TILE):
            hid_h = hid[:, k * CMP_HIDDEN:(k + 1) * CMP_HIDDEN]
            h = t * HEADS_PER_TILE + k
            if transposed:
                o_ref[0, h] = _dot_nt(w2_ref[...], hid_h).astype(BF16)
            else:
                o_ref[0, h] = _dot(hid_h, w2_ref[...]).astype(BF16)


def _compress(tok, pe, w1, w2, seq, transposed):
    bsz = tok.shape[0] // seq
    n = seq // CMP_STRIDE
    out_block = (1, N_KV, HEAD_DIM, n) if transposed else (1, N_KV, n, LANES)
    lane_tile = lambda t: pl.BlockSpec((seq, LANES), lambda b: (b, t))
    return pl.pallas_call(
        functools.partial(_compress_kernel, transposed=transposed),
        name="compress_v" if transposed else "compress_k",
        grid=(bsz,),
        in_specs=[lane_tile(t) for t in range(N_KV // HEADS_PER_TILE)] + [
            _resident(pe.shape),
            _resident(w1.shape),
            _resident(w2.shape),
        ],
        out_specs=pl.BlockSpec(out_block, lambda b: (b, 0, 0, 0)),
        out_shape=jax.ShapeDtypeStruct((bsz,) + out_block[1:], BF16),
        compiler_params=pltpu.CompilerParams(
            dimension_semantics=("parallel",), vmem_limit_bytes=VMEM_LIMIT),
    )(*([tok] * (N_KV // HEADS_PER_TILE)), pe, w1, w2)


def _tile4(x):
    return jnp.concatenate([x] * GROUP, axis=1)


def _col_max(x):
    return jnp.max(x, axis=0, keepdims=True)


def _col_sum(x):
    return jnp.sum(x, axis=0, keepdims=True)


def _with_ones(v):
    ones = jnp.where(_iota((ONES_ROWS, v.shape[1]), 0) == 0, 1.0, 0.0).astype(v.dtype)
    return jnp.concatenate([v, ones], axis=0)


def _nsa_kernel(q_ref, gate_ref, kc_ref, vc_ref, ks_ref, kw_ref, vs_ref, vw_ref, o_ref,
                kse_ref, kwe_ref, imp_ref, *, seq):
    hp = pl.program_id(1)
    c = pl.program_id(2)
    t0 = c * Q_BLOCK
    n_cmp = kc_ref.shape[2]
    n_slc = seq // SLC_BLOCK
    cols = GROUP * Q_BLOCK
    n_chunks = seq // Q_BLOCK
    chunks_per_tile = KEY_TILE // Q_BLOCK
    heads = range(HEADS_PER_STEP)
    assert n_slc + HEAD_DIM == LANES and n_slc % 8 == 0

    @pl.when(c == 0)
    def _():
        r = _iota((KV_WIDTH, 2 * LANES), 0)
        ln = _iota((KV_WIDTH, 2 * LANES), 1)
        for hh in heads:
            h = hp * HEADS_PER_STEP + hh
            pick = ((r == h * HEAD_DIM + ln) & (ln < HEAD_DIM)).astype(BF16)

            def build_k(i, _, hh=hh, pick=pick):
                sl = pl.ds(pl.multiple_of(i * KEY_TILE, KEY_TILE), KEY_TILE)
                pos = i * KEY_TILE + _iota((KEY_TILE, 2 * LANES), 0)
                lane = _iota((KEY_TILE, 2 * LANES), 1)
                in_chunk = lane - LANES == (pos & (Q_BLOCK - 1))
                in_block = lane - HEAD_DIM == _div_pow2(pos, SLC_BLOCK)
                kse_ref[hh, sl, :] = (_dot(ks_ref[0, sl, :], pick)
                                      + jnp.where(in_chunk | in_block, 1.0, 0.0)).astype(BF16)
                kwe_ref[hh, sl, :] = (_dot(kw_ref[0, sl, :], pick)
                                      + jnp.where(in_chunk, 1.0, 0.0)).astype(BF16)
                return 0

            lax.fori_loop(0, seq // KEY_TILE, build_k, 0)
            kwe_ref[hh, pl.ds(seq, Q_BLOCK), :] = jnp.where(
                _iota((Q_BLOCK, 2 * LANES), 1) == HEAD_DIM, 1.0, 0.0).astype(BF16)

    def v_chunk(ref, hh, idx):
        return _with_ones(ref[0, idx, hh * HEAD_DIM:(hh + 1) * HEAD_DIM, :])

    def v_tile(hh, kt):
        return _with_ones(jnp.concatenate(
            [vs_ref[0, kt * chunks_per_tile + k, hh * HEAD_DIM:(hh + 1) * HEAD_DIM, :]
             for k in range(chunks_per_tile)], axis=1))

    def soft(scores, m):
        return [jnp.exp2(s - m).astype(BF16) for s in scores]

    kpos = _iota((Q_BLOCK, Q_BLOCK), 0)
    tpos = _iota((Q_BLOCK, Q_BLOCK), 1)
    causal = _tile4(jnp.where(kpos <= tpos, 0.0, NEG).astype(BF16))
    recent = _tile4(jnp.where(kpos > tpos, 0.0, NEG).astype(BF16))
    pad = jnp.zeros((HEAD_DIM, cols), BF16)
    dummy_row = jnp.where(_iota((HEAD_DIM, cols), 0) == 0, NEG, 0.0).astype(BF16)
    jj = _iota((n_slc, n_cmp), 0)
    ii = _iota((n_slc, n_cmp), 1)
    ov = (jnp.minimum(ii * CMP_STRIDE + CMP_BLOCK, jj * SLC_BLOCK + SLC_BLOCK)
          - jnp.maximum(ii * CMP_STRIDE, jj * SLC_BLOCK))
    ov = (jnp.maximum(ov, 0).astype(F32) * (1.0 / CMP_BLOCK)).astype(BF16)
    c_ok = (_iota((n_cmp, Q_BLOCK), 0) * CMP_STRIDE + (CMP_BLOCK - 1)
            <= t0 + _iota((n_cmp, Q_BLOCK), 1))
    c_bias = _tile4(jnp.where(c_ok, 0.0, NEG))
    any_ok = t0 + (_iota((1, cols), 1) & (Q_BLOCK - 1)) >= CMP_BLOCK - 1
    jb = _iota((n_slc, Q_BLOCK), 0)
    tb = _div_pow2(t0 + _iota((n_slc, Q_BLOCK), 1), SLC_BLOCK)
    forced = (jb == 0) | ((jb <= tb) & (jb > tb - N_LOCAL))
    sub = _iota((8, Q_BLOCK), 0)
    n_win = WIN_KEYS // Q_BLOCK
    first = c - (n_win - 1)
    k_chunk = [jnp.where(first + i >= 0, first + i, n_chunks) for i in range(n_win)]
    v_idx = [jnp.maximum(first + i, 0) for i in range(n_win)]
    k_rows = [pl.ds(pl.multiple_of(ch * Q_BLOCK, Q_BLOCK), Q_BLOCK) for ch in k_chunk]

    def before_sweep(hh):
        qt = q_ref[0, hh * GROUP_WIDTH:(hh + 1) * GROUP_WIDTH, :]
        wq = jnp.concatenate([qt[g * HEAD_DIM:(g + 1) * HEAD_DIM, :] for g in range(GROUP)], axis=1)

        sc = _dot(kc_ref[0, hh], jnp.concatenate([wq, pad], axis=0)) + c_bias
        yield
        pc = jnp.exp2(sc - _col_max(sc))
        pc = pc * jnp.where(any_ok, 1.0 / _col_sum(pc), 0.0)
        yield
        o_cmp = _dot(vc_ref[0, hh], pc.astype(BF16))

        pcs = pc[:, 0:Q_BLOCK]
        for g in range(1, GROUP):
            pcs = pcs + pc[:, g * Q_BLOCK:(g + 1) * Q_BLOCK]
        pcs_hi = pcs.astype(BF16)
        pcs_lo = (pcs - pcs_hi.astype(F32)).astype(BF16)
        imp = _dot(ov, pcs_hi) + _dot(ov, pcs_lo)
        yield
        imp = jnp.where(forced, FORCE, imp)
        imp = jnp.where(jb <= tb, imp, NEG)
        imp_ref[hh] = imp

        w_causal = jnp.concatenate([wq, pad, causal], axis=0)
        w_mid = jnp.concatenate([wq, dummy_row], axis=0)
        w_recent = jnp.concatenate([wq, dummy_row, recent], axis=0)
        sw = [_dot(kwe_ref[hh, k_rows[0], :], w_recent)]
        sw += [_dot(kwe_ref[hh, k_rows[i], 0:LANES], w_mid) for i in range(1, n_win - 1)]
        sw += [_dot(kwe_ref[hh, k_rows[n_win - 1], :], w_causal)]
        s_d = _dot(kse_ref[hh, pl.ds(pl.multiple_of(t0, Q_BLOCK), Q_BLOCK), :], w_causal)
        yield

        n_groups = n_slc // 8
        mine = [imp[8 * r:8 * r + 8, :] for r in range(n_groups)]
        ahead_count = [jnp.zeros((8, Q_BLOCK), F32)] * n_groups
        for j in range(n_slc):
            other = imp_ref[hh, j:j + 1, :]
            for r in range(n_groups):
                if 8 * r > j:
                    ahead = other >= mine[r]
                elif 8 * r + 7 <= j:
                    ahead = other > mine[r]
                else:
                    ahead = (other > mine[r]) | ((other == mine[r]) & (sub > j - 8 * r))
                ahead_count[r] = ahead_count[r] + jnp.where(ahead, 1.0, 0.0)
            if j % 8 == 7:
                yield
        sel = (jnp.concatenate(ahead_count, axis=0) < SLC_TOPK) & (imp > 0.5 * NEG)
        sel_bias = jnp.where(sel & (jb * SLC_BLOCK < t0), 0.0, NEG).astype(BF16)
        w_sel = jnp.concatenate([wq, _tile4(sel_bias)], axis=0)

        m_w = _col_max(sw[0])
        for s in sw[1:]:
            m_w = jnp.maximum(m_w, _col_max(s))
        yield
        p_w = soft(sw, m_w)
        yield
        o_win = None
        for idx, p in zip(v_idx, p_w):
            part = _dot(v_chunk(vw_ref, hh, idx), p)
            o_win = part if o_win is None else o_win + part
        m_d = _col_max(s_d)
        acc_d = _dot(v_chunk(vs_ref, hh, c), soft([s_d], m_d)[0])
        return w_sel, o_cmp, o_win, (m_d, acc_d)

    pre = _round_robin([before_sweep(hh) for hh in heads])

    def scores(hh, kt):
        k0 = pl.multiple_of(kt * KEY_TILE, KEY_TILE)
        return _dot(kse_ref[hh, pl.ds(k0, KEY_TILE), 0:LANES], pre[hh][0])

    def sweep(items, states, lookahead=2):
        states = list(states)
        s, m_new, p = {}, {}, {}

        def issue(n):
            s[n] = scores(*items[n])

        def value_matmul(n):
            hh, kt = items[n]
            m, acc = states[hh]
            states[hh] = m_new[n], jnp.exp2(m - m_new[n]) * acc + _dot(v_tile(hh, kt), p[n])

        for n in range(min(lookahead, len(items))):
            issue(n)
        for n, (hh, _) in enumerate(items):
            m_new[n] = jnp.maximum(states[hh][0], _col_max(s[n]))
            if n + lookahead < len(items):
                issue(n + lookahead)
            p[n] = jnp.exp2(s[n] - m_new[n]).astype(BF16)
            if n >= 1:
                value_matmul(n - 1)
        value_matmul(len(items) - 1)
        return tuple(states)

    def tile_pair(i, states):
        return sweep([(hh, 2 * i + k) for k in range(2) for hh in heads], states)

    def last_tile(states):
        return sweep([(hh, n_tiles - 1) for hh in heads], states)

    n_tiles = _div_pow2(t0 + KEY_TILE - 1, KEY_TILE)
    swept = lax.fori_loop(0, _div_pow2(n_tiles, 2), tile_pair, tuple(p[3] for p in pre))
    swept = lax.cond((n_tiles & 1) == 1, last_tile, lambda states: states, swept)

    for hh in heads:
        _, o_cmp, o_win, _ = pre[hh]
        o_slc = swept[hh][1]
        gates = gate_ref[0, hh * GATE_PAD:(hh + 1) * GATE_PAD, :]
        inv_ls = 1.0 / o_slc[HEAD_DIM:HEAD_DIM + 1, :]
        inv_lw = 1.0 / o_win[HEAD_DIM:HEAD_DIM + 1, :]
        pieces = []
        for g in range(GROUP):
            cs = slice(g * Q_BLOCK, (g + 1) * Q_BLOCK)
            row = g * N_GATES
            pieces.append(gates[row:row + 1, :] * o_cmp[:, cs]
                          + gates[row + 1:row + 2, :] * (o_slc[0:HEAD_DIM, cs] * inv_ls[:, cs])
                          + gates[row + 2:row + 3, :] * (o_win[0:HEAD_DIM, cs] * inv_lw[:, cs]))
        o_ref[0, hh * GROUP_WIDTH:(hh + 1) * GROUP_WIDTH, :] = jnp.concatenate(pieces, axis=0)


def _nsa(q_t, gates_t, kc, vc_t, ks, kw, vs_t, vw_t):
    bsz, _, seq = q_t.shape
    n_cmp = kc.shape[2]
    hps = HEADS_PER_STEP
    k_spec = pl.BlockSpec((1, seq, KV_WIDTH), lambda b, h, c: (b, 0, 0))
    v_spec = pl.BlockSpec((1, seq // Q_BLOCK, hps * HEAD_DIM, Q_BLOCK), lambda b, h, c: (b, 0, h, 0))
    return pl.pallas_call(
        functools.partial(_nsa_kernel, seq=seq),
        name="nsa",
        grid=(bsz, N_KV // hps, seq // Q_BLOCK),
        in_specs=[
            pl.BlockSpec((1, hps * GROUP_WIDTH, Q_BLOCK), lambda b, h, c: (b, h, c)),
            pl.BlockSpec((1, hps * GATE_PAD, Q_BLOCK), lambda b, h, c: (b, h, c)),
            pl.BlockSpec((1, hps, n_cmp, LANES), lambda b, h, c: (b, h, 0, 0)),
            pl.BlockSpec((1, hps, HEAD_DIM, n_cmp), lambda b, h, c: (b, h, 0, 0)),
            k_spec, k_spec, v_spec, v_spec,
        ],
        out_specs=pl.BlockSpec((1, hps * GROUP_WIDTH, Q_BLOCK), lambda b, h, c: (b, h, c)),
        out_shape=jax.ShapeDtypeStruct((bsz, ATTN_WIDTH, seq), F32),
        scratch_shapes=[
            pltpu.VMEM((hps, seq, 2 * LANES), BF16),
            pltpu.VMEM((hps, seq + Q_BLOCK, 2 * LANES), BF16),
            pltpu.VMEM((hps, seq // SLC_BLOCK, Q_BLOCK), F32),
        ],
        compiler_params=pltpu.CompilerParams(
            dimension_semantics=("parallel", "parallel", "arbitrary"),
            vmem_limit_bytes=VMEM_LIMIT),
    )(q_t, gates_t, kc, vc_t, ks, kw, vs_t, vw_t)


def _rope_tables(seq):
    pos = jnp.arange(seq, dtype=F32)
    inv = ROPE_THETA ** (-jnp.arange(0, HEAD_DIM, 2, dtype=F32) / HEAD_DIM)
    ang = pos[:, None] * inv[None, :]
    cos, sin = jnp.cos(ang), jnp.sin(ang)
    reps = LANES // HEAD_DIM
    return (jnp.tile(jnp.concatenate([cos, cos], axis=1), (1, reps)),
            jnp.tile(jnp.concatenate([-sin, sin], axis=1), (1, reps)))


def _compress_params(w1, w2, pe):
    eye = jnp.eye(HEADS_PER_TILE, dtype=w1.dtype)
    w1 = jnp.einsum('ldf,hk->lhdkf', w1, eye).reshape(CMP_BLOCK, LANES, HEADS_PER_TILE * CMP_HIDDEN)
    pe = jnp.tile(pe, (1, HEADS_PER_TILE)).reshape(CMP_BLOCK, 1, LANES)
    return pe, w1.astype(BF16), w2.astype(BF16)


def kernel(x, ffn1_norm, ffn1_wg, ffn1_wu, ffn1_wd, mix_norm, w_in, conv_w, cmp_k_w1, cmp_k_w2,
           cmp_k_pe, cmp_v_w1, cmp_v_w2, cmp_v_pe, conv_out_norm, attn_out_norm, w_out, ffn2_norm,
           ffn2_wg, ffn2_wu, ffn2_wd, final_norm):
    bsz, seq, d = x.shape
    assert CMP_BLOCK == 2 * CMP_STRIDE and seq % TOKEN_TILE == 0 and seq % KEY_TILE == 0
    assert x.shape[0] * seq % TOKEN_TILE == 0 and ffn1_wg.shape[0] == 1
    xt = x.reshape(bsz * seq, d)
    cos, sin = _rope_tables(seq)

    w = w_in[0]
    w_conv = w[:, :3 * CONV_WIDTH].astype(BF16)
    a0 = 3 * CONV_WIDTH
    kv0 = a0 + ATTN_WIDTH
    col = lambda k: w[:, kv0 + k * KV_WIDTH:kv0 + (k + 1) * KV_WIDTH]
    g0 = kv0 + 6 * KV_WIDTH
    wg = w[:, g0:].reshape(d, N_KV, GROUP * N_GATES)
    wg = jnp.pad(wg, ((0, 0), (0, 0), (0, GATE_PAD - GROUP * N_GATES))).reshape(d, N_KV * GATE_PAD)
    w_attn = jnp.concatenate([w[:, a0:kv0], col(0), col(1), col(2), col(4), wg], axis=1).astype(BF16)
    w_vt = jnp.concatenate([col(3), col(5)], axis=1).T.astype(BF16)

    x1 = _ffn1(xt, ffn1_norm, ffn1_wg[0].astype(BF16), ffn1_wu[0].astype(BF16),
               ffn1_wd[0].astype(BF16))
    yc = _conv_proj(x1, mix_norm, w_conv, conv_w[0], conv_out_norm, seq)
    q_t, k_c, v_c, k_s, k_w, vs_t, vw_t, gates_t = _attn_proj(x1, mix_norm, w_attn, w_vt, cos, sin, seq)

    pe_k, w1_k, w2_k = _compress_params(cmp_k_w1[0], cmp_k_w2[0], cmp_k_pe[0])
    pe_v, w1_v, w2_v = _compress_params(cmp_v_w1[0], cmp_v_w2[0], cmp_v_pe[0])
    kc = _compress(k_c, pe_k, w1_k, jnp.pad(w2_k, ((0, 0), (0, LANES - HEAD_DIM))), seq,
                   transposed=False)
    vc_t = _compress(v_c, pe_v, w1_v, w2_v.T, seq, transposed=True)

    per_batch = lambda a: a.reshape(bsz, seq, a.shape[-1])
    ya_t = _nsa(q_t, gates_t, kc, vc_t, per_batch(k_s), per_batch(k_w), vs_t, vw_t)

    out = _out_ffn2(x1, yc, ya_t, attn_out_norm,
                    w_out[0].astype(BF16), ffn2_norm, ffn2_wg[0].astype(BF16),
                    ffn2_wu[0].astype(BF16), ffn2_wd[0].astype(BF16), final_norm.reshape(1, d))
    return out.reshape(bsz, seq, d)
```

```python
import functools

import jax
import jax.numpy as jnp
from jax import lax
from jax.experimental import pallas as pl
from jax.experimental.pallas import tpu as pltpu

F32 = jnp.float32
BF16 = jnp.bfloat16

D_MODEL = 2048
D_FF = 5632
CONV_WIDTH = 1024
N_HEADS = 16
N_KV = 4
GROUP = N_HEADS // N_KV
HEAD_DIM = 64
ATTN_WIDTH = N_HEADS * HEAD_DIM
KV_WIDTH = N_KV * HEAD_DIM
CMP_BLOCK = 32
CMP_STRIDE = 16
CMP_HIDDEN = 256
SLC_BLOCK = 64
SLC_TOPK = 16
N_LOCAL = 2
WINDOW = 512
Q_BLOCK = 128
N_GATES = 3
ROPE_THETA = 10000.0
EPS = 1e-6
NEG = -1e30
FORCE = 1e9

LANES = 128
GROUP_WIDTH = GROUP * HEAD_DIM
GATE_ROWS = LANES
TOKEN_TILE = 512
FF_TILE = 512
KEY_TILE = 512
WIN_KEYS = WINDOW + Q_BLOCK
ONES_ROWS = 16
HEADS_PER_STEP = 4
VMEM_LIMIT = 56 * 1024 * 1024
Q_SCALE = HEAD_DIM ** -0.5 * 1.4426950408889634


def _rms(x):
    return x * lax.rsqrt(jnp.mean(x * x, axis=-1, keepdims=True) + EPS)


def _dot(a, b):
    return jnp.dot(a, b, preferred_element_type=F32)


def _dot_nt(a, b):
    return lax.dot_general(a, b, (((1,), (1,)), ((), ())), preferred_element_type=F32)


def _div_pow2(x, n):
    assert n & (n - 1) == 0
    return jnp.right_shift(x, n.bit_length() - 1)


def _iota(shape, dim):
    return lax.broadcasted_iota(jnp.int32, shape, dim)


def _resident(shape):
    return pl.BlockSpec(shape, lambda *_: (0,) * len(shape), pipeline_mode=pl.Buffered(1))


def _round_robin(gens, skew=1):
    results = [None] * len(gens)
    live = {}
    waiting = dict(enumerate(gens))
    rnd = 0
    while waiting or live:
        for k in [k for k in waiting if rnd >= k * skew]:
            live[k] = waiting.pop(k)
        for k in sorted(live):
            try:
                next(live[k])
            except StopIteration as done:
                results[k] = done.value
                del live[k]
        rnd += 1
    return results


def _ffn_sweep(make_h, finish, wg_ref, wu_ref, wd_ref, h_ref, acc_ref):
    j = pl.program_id(1)
    last = pl.num_programs(1) - 1
    tm = h_ref.shape[0]
    halves = [slice(0, tm // 2), slice(tm // 2, tm)]

    def hidden(h):
        a = _dot(h, wg_ref[...])
        u = _dot(h, wu_ref[...])
        return (jax.nn.silu(a) * u).astype(BF16)

    @pl.when(j == 0)
    def _():
        def first(rows):
            h = yield from make_h(rows)
            h_ref[rows, :] = h
            yield
            act = hidden(h)
            yield
            acc_ref[rows, :] = _dot(act, wd_ref[...])

        _round_robin([first(rows) for rows in halves])

    @pl.when((j > 0) & (j < last))
    def _():
        acc_ref[...] += _dot(hidden(h_ref[...]), wd_ref[...])

    @pl.when(j == last)
    def _():
        def final(rows):
            act = hidden(h_ref[rows, :])
            yield
            finish(rows, acc_ref[rows, :] + _dot(act, wd_ref[...]))

        _round_robin([final(rows) for rows in halves])


def _ffn1_kernel(x_ref, g_ref, wg_ref, wu_ref, wd_ref, *refs, n_cast):
    cast_in = refs[:n_cast]
    o_ref = refs[n_cast]
    cast_out = refs[n_cast + 1:2 * n_cast + 1]
    h_ref, acc_ref = refs[2 * n_cast + 1:]
    for src, dst in zip(cast_in, cast_out):
        dst[...] = src[...].astype(BF16)

    def make_h(rows):
        return (_rms(x_ref[rows, :]) * g_ref[...]).astype(BF16)
        yield

    def finish(rows, acc):
        o_ref[rows, :] = x_ref[rows, :] + 0.5 * acc

    _ffn_sweep(make_h, finish, wg_ref, wu_ref, wd_ref, h_ref, acc_ref)


def _cast_block(shape, ni, nj):
    rows, cols = shape
    sub, lane = 16, LANES
    if rows % ni == 0 and cols % nj == 0 and (rows // ni) % sub == 0 and (cols // nj) % lane == 0:
        return pl.BlockSpec((rows // ni, cols // nj), lambda i, j: (i, j))
    if rows % (ni * nj) == 0 and (rows // (ni * nj)) % sub == 0:
        return pl.BlockSpec((rows // (ni * nj), cols), lambda i, j: (i * nj + j, 0))
    assert rows % ni == 0 and (rows // ni) % sub == 0
    return pl.BlockSpec((rows // ni, cols), lambda i, j: (i, 0))


def _ffn1(x, g, wg, wu, wd, later_weights):
    t, d = x.shape
    f = wg.shape[1]
    grid = (t // TOKEN_TILE, f // FF_TILE)
    cast_specs = [_cast_block(w.shape, *grid) for w in later_weights]
    return pl.pallas_call(
        functools.partial(_ffn1_kernel, n_cast=len(later_weights)),
        name="ffn1",
        grid=grid,
        in_specs=[
            pl.BlockSpec((TOKEN_TILE, d), lambda i, j: (i, 0)),
            pl.BlockSpec((1, d), lambda i, j: (0, 0)),
            pl.BlockSpec((d, FF_TILE), lambda i, j: (0, j)),
            pl.BlockSpec((d, FF_TILE), lambda i, j: (0, j)),
            pl.BlockSpec((FF_TILE, d), lambda i, j: (j, 0)),
        ] + cast_specs,
        out_specs=[pl.BlockSpec((TOKEN_TILE, d), lambda i, j: (i, 0))] + cast_specs,
        out_shape=[jax.ShapeDtypeStruct((t, d), F32)] + [
            jax.ShapeDtypeStruct(w.shape, BF16) for w in later_weights],
        scratch_shapes=[pltpu.VMEM((TOKEN_TILE, d), BF16), pltpu.VMEM((TOKEN_TILE, d), F32)],
        compiler_params=pltpu.CompilerParams(
            dimension_semantics=("arbitrary", "arbitrary"), vmem_limit_bytes=VMEM_LIMIT),
    )(x, g, wg, wu, wd, *later_weights)


def _out_ffn2_kernel(x_ref, yc_ref, ya_ref, ga_ref, wo_ref, g_ref, wg_ref, wu_ref, wd_ref,
                     fin_ref, o_ref, h_ref, acc_ref):
    def make_h(rows):
        ya = (_rms(ya_ref[0, :, rows].T) * ga_ref[...]).astype(BF16)
        yield
        mix = _dot(yc_ref[rows, :], wo_ref[0:CONV_WIDTH, :]) + _dot(ya, wo_ref[CONV_WIDTH:, :])
        yield
        x2 = x_ref[rows, :] + mix
        o_ref[rows, :] = x2
        return (_rms(x2) * g_ref[...]).astype(BF16)

    def finish(rows, acc):
        o_ref[rows, :] = _rms(o_ref[rows, :] + 0.5 * acc) * fin_ref[...]

    _ffn_sweep(make_h, finish, wg_ref, wu_ref, wd_ref, h_ref, acc_ref)


def _out_ffn2(x1, yc, ya_t, ga, wo, g, wg, wu, wd, fin):
    t, d = x1.shape
    f = wg.shape[1]
    tm = TOKEN_TILE
    tps = ya_t.shape[2] // tm
    return pl.pallas_call(
        _out_ffn2_kernel,
        name="out_ffn2",
        grid=(t // tm, f // FF_TILE),
        in_specs=[
            pl.BlockSpec((tm, d), lambda i, j: (i, 0)),
            pl.BlockSpec((tm, CONV_WIDTH), lambda i, j: (i, 0)),
            pl.BlockSpec((1, ATTN_WIDTH, tm), lambda i, j: (i // tps, 0, i % tps)),
            pl.BlockSpec((1, ATTN_WIDTH), lambda i, j: (0, 0)),
            _resident(wo.shape),
            pl.BlockSpec((1, d), lambda i, j: (0, 0)),
            pl.BlockSpec((d, FF_TILE), lambda i, j: (0, j)),
            pl.BlockSpec((d, FF_TILE), lambda i, j: (0, j)),
            pl.BlockSpec((FF_TILE, d), lambda i, j: (j, 0)),
            pl.BlockSpec((1, d), lambda i, j: (0, 0)),
        ],
        out_specs=pl.BlockSpec((tm, d), lambda i, j: (i, 0)),
        out_shape=jax.ShapeDtypeStruct((t, d), F32),
        scratch_shapes=[pltpu.VMEM((tm, d), BF16), pltpu.VMEM((tm, d), F32)],
        compiler_params=pltpu.CompilerParams(
            dimension_semantics=("parallel", "arbitrary"), vmem_limit_bytes=VMEM_LIMIT),
    )(x1, yc, ya_t, ga, wo, g, wg, wu, wd, fin)


def _conv_proj_kernel(x_ref, g_ref, w_ref, cw_ref, gc_ref, o_ref, carry_ref, *, tiles_per_seq):
    i = pl.program_id(0)

    @pl.when(i % tiles_per_seq == 0)
    def _():
        carry_ref[...] = jnp.zeros_like(carry_ref)

    tm = x_ref.shape[0]
    half = tm // 2
    tails = {-1: carry_ref[...]}

    def one_half(k):
        rows = slice(k * half, (k + 1) * half)
        h = (_rms(x_ref[rows, :]) * g_ref[...]).astype(BF16)
        yield
        c_h = _dot(h, w_ref[:, 0:CONV_WIDTH])
        c_c = _dot(h, w_ref[:, 2 * CONV_WIDTH:3 * CONV_WIDTH])
        u = c_c * c_h
        tails[k] = u[half - 8:half, :]
        c_b = _dot(h, w_ref[:, CONV_WIDTH:2 * CONV_WIDTH])
        yield
        row = lax.broadcasted_iota(jnp.int32, u.shape, 0)
        prev1 = tails[k - 1][7:8, :]
        prev2 = tails[k - 1][6:7, :]
        u1 = jnp.where(row == 0, prev1, pltpu.roll(u, 1, 0))
        u2 = jnp.where(row == 0, prev2, jnp.where(row == 1, prev1, pltpu.roll(u, 2, 0)))
        y = c_b * (cw_ref[0:1, :] * u2 + cw_ref[1:2, :] * u1 + cw_ref[2:3, :] * u)
        o_ref[rows, :] = (_rms(y) * gc_ref[...]).astype(BF16)

    _round_robin([one_half(0), one_half(1)])
    carry_ref[...] = tails[1]


def _conv_proj(x1, g, w_in, conv_w, gc, seq):
    t, d = x1.shape
    tm = TOKEN_TILE
    return pl.pallas_call(
        functools.partial(_conv_proj_kernel, tiles_per_seq=seq // tm),
        name="conv_proj",
        grid=(t // tm,),
        in_specs=[
            pl.BlockSpec((tm, d), lambda i: (i, 0)),
            pl.BlockSpec((1, d), lambda i: (0, 0)),
            _w_in_window(_CONV0, 3 * CONV_WIDTH),
            pl.BlockSpec(conv_w.shape, lambda i: (0, 0)),
            pl.BlockSpec((1, CONV_WIDTH), lambda i: (0, 0)),
        ],
        out_specs=pl.BlockSpec((tm, CONV_WIDTH), lambda i: (i, 0)),
        out_shape=jax.ShapeDtypeStruct((t, CONV_WIDTH), BF16),
        scratch_shapes=[pltpu.VMEM((8, CONV_WIDTH), F32)],
        compiler_params=pltpu.CompilerParams(
            dimension_semantics=("arbitrary",), vmem_limit_bytes=VMEM_LIMIT),
    )(x1, g, w_in, conv_w, gc)


def _rope(x, cos, sin_signed, first_half):
    outs = []
    for k in range(x.shape[1] // LANES):
        xc = x[:, k * LANES:(k + 1) * LANES]
        partner = jnp.where(first_half, pltpu.roll(xc, LANES - HEAD_DIM // 2, 1),
                            pltpu.roll(xc, HEAD_DIM // 2, 1))
        outs.append(xc * cos + partner * sin_signed)
    return outs


_CONV0 = 0
_Q0 = 3 * CONV_WIDTH
_KC0 = _Q0 + ATTN_WIDTH
_VC0 = _KC0 + KV_WIDTH
_KS0 = _VC0 + KV_WIDTH
_VS0 = _KS0 + KV_WIDTH
_KW0 = _VS0 + KV_WIDTH
_VW0 = _KW0 + KV_WIDTH
_G0 = _VW0 + KV_WIDTH
IN_WIDTH = _G0 + N_HEADS * N_GATES
IN_WIDTH_PADDED = -(-IN_WIDTH // LANES) * LANES


def _w_in_window(c0, width):
    assert c0 % width == 0
    return pl.BlockSpec((D_MODEL, width), lambda *_: (0, c0 // width), pipeline_mode=pl.Buffered(1))


def _attn_proj_kernel(x_ref, g_ref, wq_ref, wkc_ref, wvc_ref, wks_ref, wkw_ref, wgate_ref, wvt_ref,
                      cos_ref, sin_ref,
                      q_ref, kc_ref, vc_ref, ks_ref, kw_ref, vs_ref, vw_ref, gate_ref):
    tm = x_ref.shape[0]
    half = tm // 2
    lane = lax.broadcasted_iota(jnp.int32, (half, LANES), 1)
    first_half = (lane & (HEAD_DIM - 1)) < HEAD_DIM // 2

    def one_half(k):
        rows = slice(k * half, (k + 1) * half)
        h = (_rms(x_ref[rows, :]) * g_ref[...]).astype(BF16)
        cos = cos_ref[rows, :]
        sin = sin_ref[rows, :]

        def proj(w_ref):
            return _dot(h, w_ref[...])

        def rope_to(ref, w_ref, dtype):
            for n, piece in enumerate(_rope(proj(w_ref), cos, sin, first_half)):
                ref[rows, n * LANES:(n + 1) * LANES] = piece.astype(dtype)

        yield
        for n, piece in enumerate(_rope(proj(wq_ref), cos, sin, first_half)):
            q_ref[0, n * LANES:(n + 1) * LANES, rows] = (piece * Q_SCALE).T.astype(BF16)
        yield
        rope_to(kc_ref, wkc_ref, F32)
        vc_ref[rows, :] = proj(wvc_ref)
        rope_to(ks_ref, wks_ref, BF16)
        rope_to(kw_ref, wkw_ref, BF16)
        yield
        gate_ref[0, :, rows] = jax.nn.sigmoid(proj(wgate_ref)).T
        vt = _dot_nt(wvt_ref[...], h)
        chunks = half // Q_BLOCK
        for n in range(chunks):
            cs = slice(n * Q_BLOCK, (n + 1) * Q_BLOCK)
            vs_ref[0, k * chunks + n] = vt[0:KV_WIDTH, cs].astype(BF16)
            vw_ref[0, k * chunks + n] = vt[KV_WIDTH:2 * KV_WIDTH, cs].astype(BF16)

    _round_robin([one_half(0), one_half(1)])


def _attn_proj(x1, g, w_in, w_vt, cos, sin, seq):
    t, d = x1.shape
    tm = TOKEN_TILE
    tps = seq // tm
    bsz = t // seq
    cpt = tm // Q_BLOCK
    row = lambda w: pl.BlockSpec((tm, w), lambda i: (i, 0))
    col = lambda w: pl.BlockSpec((1, w, tm), lambda i: (i // tps, 0, i % tps))
    chunked = pl.BlockSpec((1, cpt, KV_WIDTH, Q_BLOCK), lambda i: (i // tps, i % tps, 0, 0))
    kv_dtypes = [F32, F32, BF16, BF16]
    out_specs = [col(ATTN_WIDTH)] + [row(KV_WIDTH)] * 4 + [chunked, chunked, col(GATE_ROWS)]
    out_shape = ([jax.ShapeDtypeStruct((bsz, ATTN_WIDTH, seq), BF16)]
                 + [jax.ShapeDtypeStruct((t, KV_WIDTH), dt) for dt in kv_dtypes]
                 + [jax.ShapeDtypeStruct((bsz, seq // Q_BLOCK, KV_WIDTH, Q_BLOCK), BF16)] * 2
                 + [jax.ShapeDtypeStruct((bsz, GATE_ROWS, seq), F32)])
    windows = [_w_in_window(_Q0, ATTN_WIDTH)] + [
        _w_in_window(c0, KV_WIDTH) for c0 in (_KC0, _VC0, _KS0, _KW0)] + [
        _w_in_window(_G0, GATE_ROWS)]
    return pl.pallas_call(
        _attn_proj_kernel,
        name="attn_proj",
        grid=(t // tm,),
        in_specs=[row(d), pl.BlockSpec((1, d), lambda i: (0, 0))] + windows + [
            _resident(w_vt.shape),
            pl.BlockSpec((tm, LANES), lambda i: (i % tps, 0)),
            pl.BlockSpec((tm, LANES), lambda i: (i % tps, 0)),
        ],
        out_specs=out_specs,
        out_shape=out_shape,
        compiler_params=pltpu.CompilerParams(
            dimension_semantics=("parallel",), vmem_limit_bytes=VMEM_LIMIT),
    )(x1, g, *([w_in] * len(windows)), w_vt, cos, sin)


HEADS_PER_TILE = LANES // HEAD_DIM


def _compress_kernel(*refs, transposed):
    tok_refs = refs[:N_KV // HEADS_PER_TILE]
    pe_ref, w1_ref, w2_ref, o_ref = refs[N_KV // HEADS_PER_TILE:]
    n = tok_refs[0].shape[0] // CMP_STRIDE
    for t, tok_ref in enumerate(tok_refs):
        first = jnp.zeros((n, HEADS_PER_TILE * CMP_HIDDEN), F32)
        second = jnp.zeros((n, HEADS_PER_TILE * CMP_HIDDEN), F32)
        for l in range(CMP_STRIDE):
            rows = tok_ref[pl.ds(l, n, stride=CMP_STRIDE), :]
            first = first + _dot((rows + pe_ref[l]).astype(BF16), w1_ref[l])
            second = second + _dot((rows + pe_ref[l + CMP_STRIDE]).astype(BF16),
                                   w1_ref[l + CMP_STRIDE])
        hid = jax.nn.gelu(first + pltpu.roll(second, n - 1, 0)).astype(BF16)
        for k in range(HEADS_PER_TILE):
            hid_h = hid[:, k * CMP_HIDDEN:(k + 1) * CMP_HIDDEN]
            h = t * HEADS_PER_TILE + k
            if transposed:
                o_ref[0, h] = _dot_nt(w2_ref[...], hid_h).astype(BF16)
            else:
                o_ref[0, h] = _dot(hid_h, w2_ref[...]).astype(BF16)


def _compress(tok, pe, w1, w2, seq, transposed):
    bsz = tok.shape[0] // seq
    n = seq // CMP_STRIDE
    out_block = (1, N_KV, HEAD_DIM, n) if transposed else (1, N_KV, n, LANES)
    lane_tile = lambda t: pl.BlockSpec((seq, LANES), lambda b: (b, t))
    return pl.pallas_call(
        functools.partial(_compress_kernel, transposed=transposed),
        name="compress_v" if transposed else "compress_k",
        grid=(bsz,),
        in_specs=[lane_tile(t) for t in range(N_KV // HEADS_PER_TILE)] + [
            _resident(pe.shape),
            _resident(w1.shape),
            _resident(w2.shape),
        ],
        out_specs=pl.BlockSpec(out_block, lambda b: (b, 0, 0, 0)),
        out_shape=jax.ShapeDtypeStruct((bsz,) + out_block[1:], BF16),
        compiler_params=pltpu.CompilerParams(
            dimension_semantics=("parallel",), vmem_limit_bytes=VMEM_LIMIT),
    )(*([tok] * (N_KV // HEADS_PER_TILE)), pe, w1, w2)


def _tile4(x):
    return jnp.concatenate([x] * GROUP, axis=1)


def _col_max(x):
    return jnp.max(x, axis=0, keepdims=True)


def _col_sum(x):
    return jnp.sum(x, axis=0, keepdims=True)


def _with_ones(v):
    ones = jnp.where(_iota((ONES_ROWS, v.shape[1]), 0) == 0, 1.0, 0.0).astype(v.dtype)
    return jnp.concatenate([v, ones], axis=0)


def _nsa_kernel(q_ref, gate_ref, kc_ref, vc_ref, ks_ref, kw_ref, vs_ref, vw_ref, o_ref,
                kse_ref, kwe_ref, imp_ref, *, seq):
    hp = pl.program_id(1)
    c = pl.program_id(2)
    t0 = c * Q_BLOCK
    n_cmp = kc_ref.shape[2]
    n_slc = seq // SLC_BLOCK
    cols = GROUP * Q_BLOCK
    n_chunks = seq // Q_BLOCK
    chunks_per_tile = KEY_TILE // Q_BLOCK
    heads = range(HEADS_PER_STEP)
    assert n_slc + HEAD_DIM == LANES and n_slc % 8 == 0

    @pl.when(c == 0)
    def _():
        r = _iota((KV_WIDTH, 2 * LANES), 0)
        ln = _iota((KV_WIDTH, 2 * LANES), 1)
        for hh in heads:
            h = hp * HEADS_PER_STEP + hh
            pick = ((r == h * HEAD_DIM + ln) & (ln < HEAD_DIM)).astype(BF16)

            def build_k(i, _, hh=hh, pick=pick):
                sl = pl.ds(pl.multiple_of(i * KEY_TILE, KEY_TILE), KEY_TILE)
                pos = i * KEY_TILE + _iota((KEY_TILE, 2 * LANES), 0)
                lane = _iota((KEY_TILE, 2 * LANES), 1)
                in_chunk = lane - LANES == (pos & (Q_BLOCK - 1))
                in_block = lane - HEAD_DIM == _div_pow2(pos, SLC_BLOCK)
                kse_ref[hh, sl, :] = (_dot(ks_ref[0, sl, :], pick)
                                      + jnp.where(in_chunk | in_block, 1.0, 0.0)).astype(BF16)
                kwe_ref[hh, sl, :] = (_dot(kw_ref[0, sl, :], pick)
                                      + jnp.where(in_chunk, 1.0, 0.0)).astype(BF16)
                return 0

            lax.fori_loop(0, seq // KEY_TILE, build_k, 0)
            kwe_ref[hh, pl.ds(seq, Q_BLOCK), :] = jnp.where(
                _iota((Q_BLOCK, 2 * LANES), 1) == HEAD_DIM, 1.0, 0.0).astype(BF16)

    def v_chunk(ref, hh, idx):
        return _with_ones(ref[0, idx, hh * HEAD_DIM:(hh + 1) * HEAD_DIM, :])

    def v_tile(hh, kt):
        return _with_ones(jnp.concatenate(
            [vs_ref[0, kt * chunks_per_tile + k, hh * HEAD_DIM:(hh + 1) * HEAD_DIM, :]
             for k in range(chunks_per_tile)], axis=1))

    def soft(scores, m):
        return [jnp.exp2(s - m).astype(BF16) for s in scores]

    kpos = _iota((Q_BLOCK, Q_BLOCK), 0)
    tpos = _iota((Q_BLOCK, Q_BLOCK), 1)
    causal = _tile4(jnp.where(kpos <= tpos, 0.0, NEG).astype(BF16))
    recent = _tile4(jnp.where(kpos > tpos, 0.0, NEG).astype(BF16))
    pad = jnp.zeros((HEAD_DIM, cols), BF16)
    dummy_row = jnp.where(_iota((HEAD_DIM, cols), 0) == 0, NEG, 0.0).astype(BF16)
    jj = _iota((n_slc, n_cmp), 0)
    ii = _iota((n_slc, n_cmp), 1)
    ov = (jnp.minimum(ii * CMP_STRIDE + CMP_BLOCK, jj * SLC_BLOCK + SLC_BLOCK)
          - jnp.maximum(ii * CMP_STRIDE, jj * SLC_BLOCK))
    ov = (jnp.maximum(ov, 0).astype(F32) * (1.0 / CMP_BLOCK)).astype(BF16)
    c_ok = (_iota((n_cmp, Q_BLOCK), 0) * CMP_STRIDE + (CMP_BLOCK - 1)
            <= t0 + _iota((n_cmp, Q_BLOCK), 1))
    c_bias = _tile4(jnp.where(c_ok, 0.0, NEG))
    any_ok = t0 + (_iota((1, cols), 1) & (Q_BLOCK - 1)) >= CMP_BLOCK - 1
    jb = _iota((n_slc, Q_BLOCK), 0)
    tb = _div_pow2(t0 + _iota((n_slc, Q_BLOCK), 1), SLC_BLOCK)
    forced = (jb == 0) | ((jb <= tb) & (jb > tb - N_LOCAL))
    sub = _iota((8, Q_BLOCK), 0)
    n_win = WIN_KEYS // Q_BLOCK
    first = c - (n_win - 1)
    k_chunk = [jnp.where(first + i >= 0, first + i, n_chunks) for i in range(n_win)]
    v_idx = [jnp.maximum(first + i, 0) for i in range(n_win)]
    k_rows = [pl.ds(pl.multiple_of(ch * Q_BLOCK, Q_BLOCK), Q_BLOCK) for ch in k_chunk]

    def before_sweep(hh):
        qt = q_ref[0, hh * GROUP_WIDTH:(hh + 1) * GROUP_WIDTH, :]
        wq = jnp.concatenate([qt[g * HEAD_DIM:(g + 1) * HEAD_DIM, :] for g in range(GROUP)], axis=1)

        sc = _dot(kc_ref[0, hh], jnp.concatenate([wq, pad], axis=0)) + c_bias
        yield
        pc = jnp.exp2(sc - _col_max(sc))
        pc = pc * jnp.where(any_ok, 1.0 / _col_sum(pc), 0.0)
        yield
        o_cmp = _dot(vc_ref[0, hh], pc.astype(BF16))

        pcs = pc[:, 0:Q_BLOCK]
        for g in range(1, GROUP):
            pcs = pcs + pc[:, g * Q_BLOCK:(g + 1) * Q_BLOCK]
        pcs_hi = pcs.astype(BF16)
        pcs_lo = (pcs - pcs_hi.astype(F32)).astype(BF16)
        imp = _dot(ov, pcs_hi) + _dot(ov, pcs_lo)
        yield
        imp = jnp.where(forced, FORCE, imp)
        imp = jnp.where(jb <= tb, imp, NEG)
        imp_ref[hh] = imp

        w_causal = jnp.concatenate([wq, pad, causal], axis=0)
        w_mid = jnp.concatenate([wq, dummy_row], axis=0)
        w_recent = jnp.concatenate([wq, dummy_row, recent], axis=0)
        sw = [_dot(kwe_ref[hh, k_rows[0], :], w_recent)]
        sw += [_dot(kwe_ref[hh, k_rows[i], 0:LANES], w_mid) for i in range(1, n_win - 1)]
        sw += [_dot(kwe_ref[hh, k_rows[n_win - 1], :], w_causal)]
        s_d = _dot(kse_ref[hh, pl.ds(pl.multiple_of(t0, Q_BLOCK), Q_BLOCK), :], w_causal)
        yield

        n_groups = n_slc // 8
        mine = [imp[8 * r:8 * r + 8, :] for r in range(n_groups)]
        ahead_count = [jnp.zeros((8, Q_BLOCK), F32)] * n_groups
        for j in range(n_slc):
            other = imp_ref[hh, j:j + 1, :]
            for r in range(n_groups):
                if 8 * r > j:
                    ahead = other >= mine[r]
                elif 8 * r + 7 <= j:
                    ahead = other > mine[r]
                else:
                    ahead = (other > mine[r]) | ((other == mine[r]) & (sub > j - 8 * r))
                ahead_count[r] = ahead_count[r] + jnp.where(ahead, 1.0, 0.0)
            if j % 8 == 7:
                yield
        sel = (jnp.concatenate(ahead_count, axis=0) < SLC_TOPK) & (imp > 0.5 * NEG)
        sel_bias = jnp.where(sel & (jb * SLC_BLOCK < t0), 0.0, NEG).astype(BF16)
        w_sel = jnp.concatenate([wq, _tile4(sel_bias)], axis=0)

        m_w = _col_max(sw[0])
        for s in sw[1:]:
            m_w = jnp.maximum(m_w, _col_max(s))
        yield
        p_w = soft(sw, m_w)
        yield
        o_win = None
        for idx, p in zip(v_idx, p_w):
            part = _dot(v_chunk(vw_ref, hh, idx), p)
            o_win = part if o_win is None else o_win + part
        m_d = _col_max(s_d)
        acc_d = _dot(v_chunk(vs_ref, hh, c), soft([s_d], m_d)[0])
        return w_sel, o_cmp, o_win, (m_d, acc_d)

    pre = _round_robin([before_sweep(hh) for hh in heads])

    def scores(hh, kt):
        k0 = pl.multiple_of(kt * KEY_TILE, KEY_TILE)
        return _dot(kse_ref[hh, pl.ds(k0, KEY_TILE), 0:LANES], pre[hh][0])

    def sweep(items, states, lookahead=2):
        states = list(states)
        s, m_new, p = {}, {}, {}

        def issue(n):
            s[n] = scores(*items[n])

        def value_matmul(n):
            hh, kt = items[n]
            m, acc = states[hh]
            states[hh] = m_new[n], jnp.exp2(m - m_new[n]) * acc + _dot(v_tile(hh, kt), p[n])

        for n in range(min(lookahead, len(items))):
            issue(n)
        for n, (hh, _) in enumerate(items):
            m_new[n] = jnp.maximum(states[hh][0], _col_max(s[n]))
            if n + lookahead < len(items):
                issue(n + lookahead)
            p[n] = jnp.exp2(s[n] - m_new[n]).astype(BF16)
            if n >= 1:
                value_matmul(n - 1)
        value_matmul(len(items) - 1)
        return tuple(states)

    def tile_pair(i, states):
        return sweep([(hh, 2 * i + k) for k in range(2) for hh in heads], states)

    def last_tile(states):
        return sweep([(hh, n_tiles - 1) for hh in heads], states)

    n_tiles = _div_pow2(t0 + KEY_TILE - 1, KEY_TILE)
    swept = lax.fori_loop(0, _div_pow2(n_tiles, 2), tile_pair, tuple(p[3] for p in pre))
    swept = lax.cond((n_tiles & 1) == 1, last_tile, lambda states: states, swept)

    for hh in heads:
        _, o_cmp, o_win, _ = pre[hh]
        o_slc = swept[hh][1]
        gates = gate_ref[0, hh * GROUP * N_GATES:(hh + 1) * GROUP * N_GATES, :]
        inv_ls = 1.0 / o_slc[HEAD_DIM:HEAD_DIM + 1, :]
        inv_lw = 1.0 / o_win[HEAD_DIM:HEAD_DIM + 1, :]
        pieces = []
        for g in range(GROUP):
            cs = slice(g * Q_BLOCK, (g + 1) * Q_BLOCK)
            row = g * N_GATES
            pieces.append(gates[row:row + 1, :] * o_cmp[:, cs]
                          + gates[row + 1:row + 2, :] * (o_slc[0:HEAD_DIM, cs] * inv_ls[:, cs])
                          + gates[row + 2:row + 3, :] * (o_win[0:HEAD_DIM, cs] * inv_lw[:, cs]))
        o_ref[0, hh * GROUP_WIDTH:(hh + 1) * GROUP_WIDTH, :] = jnp.concatenate(pieces, axis=0)


def _nsa(q_t, gates_t, kc, vc_t, ks, kw, vs_t, vw_t):
    bsz, _, seq = q_t.shape
    n_cmp = kc.shape[2]
    hps = HEADS_PER_STEP
    assert hps == N_KV
    k_spec = pl.BlockSpec((1, seq, KV_WIDTH), lambda b, h, c: (b, 0, 0))
    v_spec = pl.BlockSpec((1, seq // Q_BLOCK, hps * HEAD_DIM, Q_BLOCK), lambda b, h, c: (b, 0, h, 0))
    return pl.pallas_call(
        functools.partial(_nsa_kernel, seq=seq),
        name="nsa",
        grid=(bsz, N_KV // hps, seq // Q_BLOCK),
        in_specs=[
            pl.BlockSpec((1, hps * GROUP_WIDTH, Q_BLOCK), lambda b, h, c: (b, h, c)),
            pl.BlockSpec((1, GATE_ROWS, Q_BLOCK), lambda b, h, c: (b, 0, c)),
            pl.BlockSpec((1, hps, n_cmp, LANES), lambda b, h, c: (b, h, 0, 0)),
            pl.BlockSpec((1, hps, HEAD_DIM, n_cmp), lambda b, h, c: (b, h, 0, 0)),
            k_spec, k_spec, v_spec, v_spec,
        ],
        out_specs=pl.BlockSpec((1, hps * GROUP_WIDTH, Q_BLOCK), lambda b, h, c: (b, h, c)),
        out_shape=jax.ShapeDtypeStruct((bsz, ATTN_WIDTH, seq), F32),
        scratch_shapes=[
            pltpu.VMEM((hps, seq, 2 * LANES), BF16),
            pltpu.VMEM((hps, seq + Q_BLOCK, 2 * LANES), BF16),
            pltpu.VMEM((hps, seq // SLC_BLOCK, Q_BLOCK), F32),
        ],
        compiler_params=pltpu.CompilerParams(
            dimension_semantics=("parallel", "parallel", "arbitrary"),
            vmem_limit_bytes=VMEM_LIMIT),
    )(q_t, gates_t, kc, vc_t, ks, kw, vs_t, vw_t)


def _rope_tables(seq):
    pos = jnp.arange(seq, dtype=F32)
    inv = ROPE_THETA ** (-jnp.arange(0, HEAD_DIM, 2, dtype=F32) / HEAD_DIM)
    ang = pos[:, None] * inv[None, :]
    cos, sin = jnp.cos(ang), jnp.sin(ang)
    reps = LANES // HEAD_DIM
    return (jnp.tile(jnp.concatenate([cos, cos], axis=1), (1, reps)),
            jnp.tile(jnp.concatenate([-sin, sin], axis=1), (1, reps)))


def _compress_params(w1, w2, pe):
    eye = jnp.eye(HEADS_PER_TILE, dtype=w1.dtype)
    w1 = jnp.einsum('ldf,hk->lhdkf', w1, eye).reshape(CMP_BLOCK, LANES, HEADS_PER_TILE * CMP_HIDDEN)
    pe = jnp.tile(pe, (1, HEADS_PER_TILE)).reshape(CMP_BLOCK, 1, LANES)
    return pe, w1.astype(BF16), w2.astype(BF16)


def kernel(x, ffn1_norm, ffn1_wg, ffn1_wu, ffn1_wd, mix_norm, w_in, conv_w, cmp_k_w1, cmp_k_w2,
           cmp_k_pe, cmp_v_w1, cmp_v_w2, cmp_v_pe, conv_out_norm, attn_out_norm, w_out, ffn2_norm,
           ffn2_wg, ffn2_wu, ffn2_wd, final_norm):
    bsz, seq, d = x.shape
    assert CMP_BLOCK == 2 * CMP_STRIDE and seq % TOKEN_TILE == 0 and seq % KEY_TILE == 0
    assert x.shape[0] * seq % TOKEN_TILE == 0 and ffn1_wg.shape[0] == 1
    xt = x.reshape(bsz * seq, d)
    cos, sin = _rope_tables(seq)

    assert w_in.shape[1:] == (D_MODEL, IN_WIDTH)
    w = jnp.pad(w_in[0].astype(BF16), ((0, 0), (0, IN_WIDTH_PADDED - IN_WIDTH)))
    w_vt = jnp.concatenate([w[:, _VS0:_VS0 + KV_WIDTH], w[:, _VW0:_VW0 + KV_WIDTH]], axis=1).T

    x1, wg2, wu2, wd2, wo = _ffn1(xt, ffn1_norm, ffn1_wg[0].astype(BF16), ffn1_wu[0].astype(BF16),
                                  ffn1_wd[0].astype(BF16),
                                  [ffn2_wg[0], ffn2_wu[0], ffn2_wd[0], w_out[0]])
    yc = _conv_proj(x1, mix_norm, w, conv_w[0], conv_out_norm, seq)
    q_t, k_c, v_c, k_s, k_w, vs_t, vw_t, gates_t = _attn_proj(x1, mix_norm, w, w_vt, cos, sin, seq)

    pe_k, w1_k, w2_k = _compress_params(cmp_k_w1[0], cmp_k_w2[0], cmp_k_pe[0])
    pe_v, w1_v, w2_v = _compress_params(cmp_v_w1[0], cmp_v_w2[0], cmp_v_pe[0])
    kc = _compress(k_c, pe_k, w1_k, jnp.pad(w2_k, ((0, 0), (0, LANES - HEAD_DIM))), seq,
                   transposed=False)
    vc_t = _compress(v_c, pe_v, w1_v, w2_v.T, seq, transposed=True)

    per_batch = lambda a: a.reshape(bsz, seq, a.shape[-1])
    ya_t = _nsa(q_t, gates_t, kc, vc_t, per_batch(k_s), per_batch(k_w), vs_t, vw_t)

    out = _out_ffn2(x1, yc, ya_t, attn_out_norm, wo, ffn2_norm, wg2, wu2, wd2,
                    final_norm.reshape(1, d))
    return out.reshape(bsz, seq, d)
```

```python
import functools

import jax
import jax.numpy as jnp
from jax import lax
from jax.experimental import pallas as pl
from jax.experimental.pallas import tpu as pltpu

F32 = jnp.float32
BF16 = jnp.bfloat16

D_MODEL = 2048
D_FF = 5632
CONV_WIDTH = 1024
N_HEADS = 16
N_KV = 4
GROUP = N_HEADS // N_KV
HEAD_DIM = 64
ATTN_WIDTH = N_HEADS * HEAD_DIM
KV_WIDTH = N_KV * HEAD_DIM
CMP_BLOCK = 32
CMP_STRIDE = 16
CMP_HIDDEN = 256
SLC_BLOCK = 64
SLC_TOPK = 16
N_LOCAL = 2
WINDOW = 512
Q_BLOCK = 128
N_GATES = 3
ROPE_THETA = 10000.0
EPS = 1e-6
NEG = -1e30
FORCE = 1e9

LANES = 128
GROUP_WIDTH = GROUP * HEAD_DIM
GATE_ROWS = LANES
TOKEN_TILE = 512
FFN1_TOKEN_TILE = 1024
FF_TILE = 512
KEY_TILE = 512
WIN_KEYS = WINDOW + Q_BLOCK
ONES_ROWS = 16
HEADS_PER_STEP = 4
VMEM_LIMIT = 56 * 1024 * 1024
FFN1_VMEM_LIMIT = 62 * 1024 * 1024
Q_SCALE = HEAD_DIM ** -0.5 * 1.4426950408889634


def _rms(x):
    return x * lax.rsqrt(jnp.mean(x * x, axis=-1, keepdims=True) + EPS)


def _dot(a, b):
    return jnp.dot(a, b, preferred_element_type=F32)


def _dot_nt(a, b):
    return lax.dot_general(a, b, (((1,), (1,)), ((), ())), preferred_element_type=F32)


def _div_pow2(x, n):
    assert n & (n - 1) == 0
    return jnp.right_shift(x, n.bit_length() - 1)


def _iota(shape, dim):
    return lax.broadcasted_iota(jnp.int32, shape, dim)


def _resident(shape):
    return pl.BlockSpec(shape, lambda *_: (0,) * len(shape), pipeline_mode=pl.Buffered(1))


def _round_robin(gens, skew=1):
    results = [None] * len(gens)
    live = {}
    waiting = dict(enumerate(gens))
    rnd = 0
    while waiting or live:
        for k in [k for k in waiting if rnd >= k * skew]:
            live[k] = waiting.pop(k)
        for k in sorted(live):
            try:
                next(live[k])
            except StopIteration as done:
                results[k] = done.value
                del live[k]
        rnd += 1
    return results


def _ffn_sweep(make_h, finish, wg_ref, wu_ref, wd_ref, h_ref, acc_ref):
    j = pl.program_id(1)
    last = pl.num_programs(1) - 1
    tm = h_ref.shape[0]
    halves = [slice(0, tm // 2), slice(tm // 2, tm)]

    def hidden(h):
        a = _dot(h, wg_ref[...])
        u = _dot(h, wu_ref[...])
        return (jax.nn.silu(a) * u).astype(BF16)

    @pl.when(j == 0)
    def _():
        def first(rows):
            h = yield from make_h(rows)
            h_ref[rows, :] = h
            yield
            act = hidden(h)
            yield
            acc_ref[rows, :] = _dot(act, wd_ref[...])

        _round_robin([first(rows) for rows in halves])

    @pl.when((j > 0) & (j < last))
    def _():
        acc_ref[...] += _dot(hidden(h_ref[...]), wd_ref[...])

    @pl.when(j == last)
    def _():
        def final(rows):
            act = hidden(h_ref[rows, :])
            yield
            finish(rows, acc_ref[rows, :] + _dot(act, wd_ref[...]))

        _round_robin([final(rows) for rows in halves])


def _ffn1_kernel(x_ref, g_ref, wg_ref, wu_ref, wd_ref, *refs, n_cast):
    cast_in = refs[:n_cast]
    o_ref = refs[n_cast]
    cast_out = refs[n_cast + 1:2 * n_cast + 1]
    (h_ref,) = refs[2 * n_cast + 1:]
    acc_ref = o_ref
    for src, dst in zip(cast_in, cast_out):
        dst[...] = src[...].astype(BF16)

    def make_h(rows):
        return (_rms(x_ref[rows, :]) * g_ref[...]).astype(BF16)
        yield

    def finish(rows, acc):
        o_ref[rows, :] = x_ref[rows, :] + 0.5 * acc

    _ffn_sweep(make_h, finish, wg_ref, wu_ref, wd_ref, h_ref, acc_ref)


def _cast_block(shape, ni, nj):
    rows, cols = shape
    sub, lane = 16, LANES
    if rows % ni == 0 and cols % nj == 0 and (rows // ni) % sub == 0 and (cols // nj) % lane == 0:
        return pl.BlockSpec((rows // ni, cols // nj), lambda i, j: (i, j))
    if rows % (ni * nj) == 0 and (rows // (ni * nj)) % sub == 0:
        return pl.BlockSpec((rows // (ni * nj), cols), lambda i, j: (i * nj + j, 0))
    assert rows % ni == 0 and (rows // ni) % sub == 0
    return pl.BlockSpec((rows // ni, cols), lambda i, j: (i, 0))


def _ffn1(x, g, wg, wu, wd, later_weights):
    t, d = x.shape
    f = wg.shape[1]
    tm = FFN1_TOKEN_TILE
    grid = (t // tm, f // FF_TILE)
    cast_specs = [_cast_block(w.shape, *grid) for w in later_weights]
    return pl.pallas_call(
        functools.partial(_ffn1_kernel, n_cast=len(later_weights)),
        name="ffn1",
        grid=grid,
        in_specs=[
            pl.BlockSpec((tm, d), lambda i, j: (i, 0)),
            pl.BlockSpec((1, d), lambda i, j: (0, 0)),
            pl.BlockSpec((d, FF_TILE), lambda i, j: (0, j)),
            pl.BlockSpec((d, FF_TILE), lambda i, j: (0, j)),
            pl.BlockSpec((FF_TILE, d), lambda i, j: (j, 0)),
        ] + cast_specs,
        out_specs=[pl.BlockSpec((tm, d), lambda i, j: (i, 0))] + cast_specs,
        out_shape=[jax.ShapeDtypeStruct((t, d), F32)] + [
            jax.ShapeDtypeStruct(w.shape, BF16) for w in later_weights],
        scratch_shapes=[pltpu.VMEM((tm, d), BF16)],
        compiler_params=pltpu.CompilerParams(
            dimension_semantics=("arbitrary", "arbitrary"), vmem_limit_bytes=FFN1_VMEM_LIMIT),
    )(x, g, wg, wu, wd, *later_weights)


def _out_ffn2_kernel(x_ref, yc_ref, ya_ref, ga_ref, wo_ref, g_ref, wg_ref, wu_ref, wd_ref,
                     fin_ref, o_ref, h_ref, acc_ref):
    def make_h(rows):
        ya = (_rms(ya_ref[0, :, rows].T) * ga_ref[...]).astype(BF16)
        yield
        mix = _dot(yc_ref[rows, :], wo_ref[0:CONV_WIDTH, :]) + _dot(ya, wo_ref[CONV_WIDTH:, :])
        yield
        x2 = x_ref[rows, :] + mix
        o_ref[rows, :] = x2
        return (_rms(x2) * g_ref[...]).astype(BF16)

    def finish(rows, acc):
        o_ref[rows, :] = _rms(o_ref[rows, :] + 0.5 * acc) * fin_ref[...]

    _ffn_sweep(make_h, finish, wg_ref, wu_ref, wd_ref, h_ref, acc_ref)


def _out_ffn2(x1, yc, ya_t, ga, wo, g, wg, wu, wd, fin):
    t, d = x1.shape
    f = wg.shape[1]
    tm = TOKEN_TILE
    tps = ya_t.shape[2] // tm
    return pl.pallas_call(
        _out_ffn2_kernel,
        name="out_ffn2",
        grid=(t // tm, f // FF_TILE),
        in_specs=[
            pl.BlockSpec((tm, d), lambda i, j: (i, 0)),
            pl.BlockSpec((tm, CONV_WIDTH), lambda i, j: (i, 0)),
            pl.BlockSpec((1, ATTN_WIDTH, tm), lambda i, j: (i // tps, 0, i % tps)),
            pl.BlockSpec((1, ATTN_WIDTH), lambda i, j: (0, 0)),
            _resident(wo.shape),
            pl.BlockSpec((1, d), lambda i, j: (0, 0)),
            pl.BlockSpec((d, FF_TILE), lambda i, j: (0, j)),
            pl.BlockSpec((d, FF_TILE), lambda i, j: (0, j)),
            pl.BlockSpec((FF_TILE, d), lambda i, j: (j, 0)),
            pl.BlockSpec((1, d), lambda i, j: (0, 0)),
        ],
        out_specs=pl.BlockSpec((tm, d), lambda i, j: (i, 0)),
        out_shape=jax.ShapeDtypeStruct((t, d), F32),
        scratch_shapes=[pltpu.VMEM((tm, d), BF16), pltpu.VMEM((tm, d), F32)],
        compiler_params=pltpu.CompilerParams(
            dimension_semantics=("parallel", "arbitrary"), vmem_limit_bytes=VMEM_LIMIT),
    )(x1, yc, ya_t, ga, wo, g, wg, wu, wd, fin)


def _conv_proj_kernel(x_ref, g_ref, w_ref, cw_ref, gc_ref, o_ref, carry_ref, *, tiles_per_seq):
    i = pl.program_id(0)

    @pl.when(i % tiles_per_seq == 0)
    def _():
        carry_ref[...] = jnp.zeros_like(carry_ref)

    tm = x_ref.shape[0]
    half = tm // 2
    tails = {-1: carry_ref[...]}

    def one_half(k):
        rows = slice(k * half, (k + 1) * half)
        h = (_rms(x_ref[rows, :]) * g_ref[...]).astype(BF16)
        yield
        c_h = _dot(h, w_ref[:, 0:CONV_WIDTH])
        c_c = _dot(h, w_ref[:, 2 * CONV_WIDTH:3 * CONV_WIDTH])
        u = c_c * c_h
        tails[k] = u[half - 8:half, :]
        c_b = _dot(h, w_ref[:, CONV_WIDTH:2 * CONV_WIDTH])
        yield
        row = lax.broadcasted_iota(jnp.int32, u.shape, 0)
        prev1 = tails[k - 1][7:8, :]
        prev2 = tails[k - 1][6:7, :]
        u1 = jnp.where(row == 0, prev1, pltpu.roll(u, 1, 0))
        u2 = jnp.where(row == 0, prev2, jnp.where(row == 1, prev1, pltpu.roll(u, 2, 0)))
        y = c_b * (cw_ref[0:1, :] * u2 + cw_ref[1:2, :] * u1 + cw_ref[2:3, :] * u)
        o_ref[rows, :] = (_rms(y) * gc_ref[...]).astype(BF16)

    _round_robin([one_half(0), one_half(1)])
    carry_ref[...] = tails[1]


def _conv_proj(x1, g, w_in, conv_w, gc, seq):
    t, d = x1.shape
    tm = TOKEN_TILE
    return pl.pallas_call(
        functools.partial(_conv_proj_kernel, tiles_per_seq=seq // tm),
        name="conv_proj",
        grid=(t // tm,),
        in_specs=[
            pl.BlockSpec((tm, d), lambda i: (i, 0)),
            pl.BlockSpec((1, d), lambda i: (0, 0)),
            _w_in_window(_CONV0, 3 * CONV_WIDTH),
            pl.BlockSpec(conv_w.shape, lambda i: (0, 0)),
            pl.BlockSpec((1, CONV_WIDTH), lambda i: (0, 0)),
        ],
        out_specs=pl.BlockSpec((tm, CONV_WIDTH), lambda i: (i, 0)),
        out_shape=jax.ShapeDtypeStruct((t, CONV_WIDTH), BF16),
        scratch_shapes=[pltpu.VMEM((8, CONV_WIDTH), F32)],
        compiler_params=pltpu.CompilerParams(
            dimension_semantics=("arbitrary",), vmem_limit_bytes=VMEM_LIMIT),
    )(x1, g, w_in, conv_w, gc)


def _rope(x, cos, sin_signed, first_half):
    outs = []
    for k in range(x.shape[1] // LANES):
        xc = x[:, k * LANES:(k + 1) * LANES]
        partner = jnp.where(first_half, pltpu.roll(xc, LANES - HEAD_DIM // 2, 1),
                            pltpu.roll(xc, HEAD_DIM // 2, 1))
        outs.append(xc * cos + partner * sin_signed)
    return outs


_CONV0 = 0
_Q0 = 3 * CONV_WIDTH
_KC0 = _Q0 + ATTN_WIDTH
_VC0 = _KC0 + KV_WIDTH
_KS0 = _VC0 + KV_WIDTH
_VS0 = _KS0 + KV_WIDTH
_KW0 = _VS0 + KV_WIDTH
_VW0 = _KW0 + KV_WIDTH
_G0 = _VW0 + KV_WIDTH
IN_WIDTH = _G0 + N_HEADS * N_GATES
IN_WIDTH_PADDED = -(-IN_WIDTH // LANES) * LANES


def _w_in_window(c0, width):
    assert c0 % width == 0
    return pl.BlockSpec((D_MODEL, width), lambda *_: (0, c0 // width), pipeline_mode=pl.Buffered(1))


def _attn_proj_kernel(x_ref, g_ref, wq_ref, wkc_ref, wvc_ref, wks_ref, wkw_ref, wgate_ref, wvt_ref,
                      cos_ref, sin_ref,
                      q_ref, kc_ref, vc_ref, ks_ref, kw_ref, vs_ref, vw_ref, gate_ref):
    tm = x_ref.shape[0]
    half = tm // 2
    lane = lax.broadcasted_iota(jnp.int32, (half, LANES), 1)
    first_half = (lane & (HEAD_DIM - 1)) < HEAD_DIM // 2

    def one_half(k):
        rows = slice(k * half, (k + 1) * half)
        h = (_rms(x_ref[rows, :]) * g_ref[...]).astype(BF16)
        cos = cos_ref[rows, :]
        sin = sin_ref[rows, :]

        def proj(w_ref):
            return _dot(h, w_ref[...])

        def rope_to(ref, w_ref, dtype):
            for n, piece in enumerate(_rope(proj(w_ref), cos, sin, first_half)):
                ref[rows, n * LANES:(n + 1) * LANES] = piece.astype(dtype)

        yield
        for n, piece in enumerate(_rope(proj(wq_ref), cos, sin, first_half)):
            q_ref[0, n * LANES:(n + 1) * LANES, rows] = (piece * Q_SCALE).T.astype(BF16)
        yield
        rope_to(kc_ref, wkc_ref, F32)
        vc_ref[rows, :] = proj(wvc_ref)
        rope_to(ks_ref, wks_ref, BF16)
        rope_to(kw_ref, wkw_ref, BF16)
        yield
        gate_ref[0, :, rows] = jax.nn.sigmoid(proj(wgate_ref)).T
        vt = _dot_nt(wvt_ref[...], h)
        chunks = half // Q_BLOCK
        for n in range(chunks):
            cs = slice(n * Q_BLOCK, (n + 1) * Q_BLOCK)
            vs_ref[0, k * chunks + n] = vt[0:KV_WIDTH, cs].astype(BF16)
            vw_ref[0, k * chunks + n] = vt[KV_WIDTH:2 * KV_WIDTH, cs].astype(BF16)

    _round_robin([one_half(0), one_half(1)])


def _attn_proj(x1, g, w_in, w_vt, cos, sin, seq):
    t, d = x1.shape
    tm = TOKEN_TILE
    tps = seq // tm
    bsz = t // seq
    cpt = tm // Q_BLOCK
    row = lambda w: pl.BlockSpec((tm, w), lambda i: (i, 0))
    col = lambda w: pl.BlockSpec((1, w, tm), lambda i: (i // tps, 0, i % tps))
    chunked = pl.BlockSpec((1, cpt, KV_WIDTH, Q_BLOCK), lambda i: (i // tps, i % tps, 0, 0))
    kv_dtypes = [F32, F32, BF16, BF16]
    out_specs = [col(ATTN_WIDTH)] + [row(KV_WIDTH)] * 4 + [chunked, chunked, col(GATE_ROWS)]
    out_shape = ([jax.ShapeDtypeStruct((bsz, ATTN_WIDTH, seq), BF16)]
                 + [jax.ShapeDtypeStruct((t, KV_WIDTH), dt) for dt in kv_dtypes]
                 + [jax.ShapeDtypeStruct((bsz, seq // Q_BLOCK, KV_WIDTH, Q_BLOCK), BF16)] * 2
                 + [jax.ShapeDtypeStruct((bsz, GATE_ROWS, seq), F32)])
    windows = [_w_in_window(_Q0, ATTN_WIDTH)] + [
        _w_in_window(c0, KV_WIDTH) for c0 in (_KC0, _VC0, _KS0, _KW0)] + [
        _w_in_window(_G0, GATE_ROWS)]
    return pl.pallas_call(
        _attn_proj_kernel,
        name="attn_proj",
        grid=(t // tm,),
        in_specs=[row(d), pl.BlockSpec((1, d), lambda i: (0, 0))] + windows + [
            _resident(w_vt.shape),
            pl.BlockSpec((tm, LANES), lambda i: (i % tps, 0)),
            pl.BlockSpec((tm, LANES), lambda i: (i % tps, 0)),
        ],
        out_specs=out_specs,
        out_shape=out_shape,
        compiler_params=pltpu.CompilerParams(
            dimension_semantics=("parallel",), vmem_limit_bytes=VMEM_LIMIT),
    )(x1, g, *([w_in] * len(windows)), w_vt, cos, sin)


HEADS_PER_TILE = LANES // HEAD_DIM


def _compress_kernel(*refs, transposed):
    tok_refs = refs[:N_KV // HEADS_PER_TILE]
    pe_ref, w1_ref, w2_ref, o_ref = refs[N_KV // HEADS_PER_TILE:]
    n = tok_refs[0].shape[0] // CMP_STRIDE
    for t, tok_ref in enumerate(tok_refs):
        first = jnp.zeros((n, HEADS_PER_TILE * CMP_HIDDEN), F32)
        second = jnp.zeros((n, HEADS_PER_TILE * CMP_HIDDEN), F32)
        for l in range(CMP_STRIDE):
            rows = tok_ref[pl.ds(l, n, stride=CMP_STRIDE), :]
            first = first + _dot((rows + pe_ref[l]).astype(BF16), w1_ref[l])
            second = second + _dot((rows + pe_ref[l + CMP_STRIDE]).astype(BF16),
                                   w1_ref[l + CMP_STRIDE])
        hid = jax.nn.gelu(first + pltpu.roll(second, n - 1, 0)).astype(BF16)
        for k in range(HEADS_PER_TILE):
            hid_h = hid[:, k * CMP_HIDDEN:(k + 1) * CMP_HIDDEN]
            h = t * HEADS_PER_TILE + k
            if transposed:
                o_ref[0, h] = _dot_nt(w2_ref[...], hid_h).astype(BF16)
            else:
                o_ref[0, h] = _dot(hid_h, w2_ref[...]).astype(BF16)


def _compress(tok, pe, w1, w2, seq, transposed):
    bsz = tok.shape[0] // seq
    n = seq // CMP_STRIDE
    out_block = (1, N_KV, HEAD_DIM, n) if transposed else (1, N_KV, n, LANES)
    lane_tile = lambda t: pl.BlockSpec((seq, LANES), lambda b: (b, t))
    return pl.pallas_call(
        functools.partial(_compress_kernel, transposed=transposed),
        name="compress_v" if transposed else "compress_k",
        grid=(bsz,),
        in_specs=[lane_tile(t) for t in range(N_KV // HEADS_PER_TILE)] + [
            _resident(pe.shape),
            _resident(w1.shape),
            _resident(w2.shape),
        ],
        out_specs=pl.BlockSpec(out_block, lambda b: (b, 0, 0, 0)),
        out_shape=jax.ShapeDtypeStruct((bsz,) + out_block[1:], BF16),
        compiler_params=pltpu.CompilerParams(
            dimension_semantics=("parallel",), vmem_limit_bytes=VMEM_LIMIT),
    )(*([tok] * (N_KV // HEADS_PER_TILE)), pe, w1, w2)


def _tile4(x):
    return jnp.concatenate([x] * GROUP, axis=1)


def _col_max(x):
    return jnp.max(x, axis=0, keepdims=True)


def _col_sum(x):
    return jnp.sum(x, axis=0, keepdims=True)


def _with_ones(v):
    ones = jnp.where(_iota((ONES_ROWS, v.shape[1]), 0) == 0, 1.0, 0.0).astype(v.dtype)
    return jnp.concatenate([v, ones], axis=0)


def _nsa_kernel(q_ref, gate_ref, kc_ref, vc_ref, ks_ref, kw_ref, vs_ref, vw_ref, o_ref,
                kse_ref, kwe_ref, imp_ref, *, seq):
    hp = pl.program_id(1)
    c = pl.program_id(2)
    t0 = c * Q_BLOCK
    n_cmp = kc_ref.shape[2]
    n_slc = seq // SLC_BLOCK
    cols = GROUP * Q_BLOCK
    n_chunks = seq // Q_BLOCK
    chunks_per_tile = KEY_TILE // Q_BLOCK
    heads = range(HEADS_PER_STEP)
    assert n_slc + HEAD_DIM == LANES and n_slc % 8 == 0

    @pl.when(c == 0)
    def _():
        r = _iota((KV_WIDTH, 2 * LANES), 0)
        ln = _iota((KV_WIDTH, 2 * LANES), 1)
        for hh in heads:
            h = hp * HEADS_PER_STEP + hh
            pick = ((r == h * HEAD_DIM + ln) & (ln < HEAD_DIM)).astype(BF16)

            def build_k(i, _, hh=hh, pick=pick):
                sl = pl.ds(pl.multiple_of(i * KEY_TILE, KEY_TILE), KEY_TILE)
                pos = i * KEY_TILE + _iota((KEY_TILE, 2 * LANES), 0)
                lane = _iota((KEY_TILE, 2 * LANES), 1)
                in_chunk = lane - LANES == (pos & (Q_BLOCK - 1))
                in_block = lane - HEAD_DIM == _div_pow2(pos, SLC_BLOCK)
                kse_ref[hh, sl, :] = (_dot(ks_ref[0, sl, :], pick)
                                      + jnp.where(in_chunk | in_block, 1.0, 0.0)).astype(BF16)
                kwe_ref[hh, sl, :] = (_dot(kw_ref[0, sl, :], pick)
                                      + jnp.where(in_chunk, 1.0, 0.0)).astype(BF16)
                return 0

            lax.fori_loop(0, seq // KEY_TILE, build_k, 0)
            kwe_ref[hh, pl.ds(seq, Q_BLOCK), :] = jnp.where(
                _iota((Q_BLOCK, 2 * LANES), 1) == HEAD_DIM, 1.0, 0.0).astype(BF16)

    def v_chunk(ref, hh, idx):
        return _with_ones(ref[0, idx, hh * HEAD_DIM:(hh + 1) * HEAD_DIM, :])

    def v_tile(hh, kt):
        return _with_ones(jnp.concatenate(
            [vs_ref[0, kt * chunks_per_tile + k, hh * HEAD_DIM:(hh + 1) * HEAD_DIM, :]
             for k in range(chunks_per_tile)], axis=1))

    def soft(scores, m):
        return [jnp.exp2(s - m).astype(BF16) for s in scores]

    kpos = _iota((Q_BLOCK, Q_BLOCK), 0)
    tpos = _iota((Q_BLOCK, Q_BLOCK), 1)
    causal = _tile4(jnp.where(kpos <= tpos, 0.0, NEG).astype(BF16))
    recent = _tile4(jnp.where(kpos > tpos, 0.0, NEG).astype(BF16))
    pad = jnp.zeros((HEAD_DIM, cols), BF16)
    dummy_row = jnp.where(_iota((HEAD_DIM, cols), 0) == 0, NEG, 0.0).astype(BF16)
    jj = _iota((n_slc, n_cmp), 0)
    ii = _iota((n_slc, n_cmp), 1)
    ov = (jnp.minimum(ii * CMP_STRIDE + CMP_BLOCK, jj * SLC_BLOCK + SLC_BLOCK)
          - jnp.maximum(ii * CMP_STRIDE, jj * SLC_BLOCK))
    ov = (jnp.maximum(ov, 0).astype(F32) * (1.0 / CMP_BLOCK)).astype(BF16)
    c_ok = (_iota((n_cmp, Q_BLOCK), 0) * CMP_STRIDE + (CMP_BLOCK - 1)
            <= t0 + _iota((n_cmp, Q_BLOCK), 1))
    c_bias = _tile4(jnp.where(c_ok, 0.0, NEG))
    any_ok = t0 + (_iota((1, cols), 1) & (Q_BLOCK - 1)) >= CMP_BLOCK - 1
    jb = _iota((n_slc, Q_BLOCK), 0)
    tb = _div_pow2(t0 + _iota((n_slc, Q_BLOCK), 1), SLC_BLOCK)
    forced = (jb == 0) | ((jb <= tb) & (jb > tb - N_LOCAL))
    sub = _iota((8, Q_BLOCK), 0)
    n_win = WIN_KEYS // Q_BLOCK
    first = c - (n_win - 1)
    k_chunk = [jnp.where(first + i >= 0, first + i, n_chunks) for i in range(n_win)]
    v_idx = [jnp.maximum(first + i, 0) for i in range(n_win)]
    k_rows = [pl.ds(pl.multiple_of(ch * Q_BLOCK, Q_BLOCK), Q_BLOCK) for ch in k_chunk]

    def before_sweep(hh):
        qt = q_ref[0, hh * GROUP_WIDTH:(hh + 1) * GROUP_WIDTH, :]
        wq = jnp.concatenate([qt[g * HEAD_DIM:(g + 1) * HEAD_DIM, :] for g in range(GROUP)], axis=1)

        sc = _dot(kc_ref[0, hh], jnp.concatenate([wq, pad], axis=0)) + c_bias
        yield
        pc = jnp.exp2(sc - _col_max(sc))
        pc = pc * jnp.where(any_ok, 1.0 / _col_sum(pc), 0.0)
        yield
        o_cmp = _dot(vc_ref[0, hh], pc.astype(BF16))

        pcs = pc[:, 0:Q_BLOCK]
        for g in range(1, GROUP):
            pcs = pcs + pc[:, g * Q_BLOCK:(g + 1) * Q_BLOCK]
        pcs_hi = pcs.astype(BF16)
        pcs_lo = (pcs - pcs_hi.astype(F32)).astype(BF16)
        imp = _dot(ov, pcs_hi) + _dot(ov, pcs_lo)
        yield
        imp = jnp.where(forced, FORCE, imp)
        imp = jnp.where(jb <= tb, imp, NEG)
        imp_ref[hh] = imp

        w_causal = jnp.concatenate([wq, pad, causal], axis=0)
        w_mid = jnp.concatenate([wq, dummy_row], axis=0)
        w_recent = jnp.concatenate([wq, dummy_row, recent], axis=0)
        sw = [_dot(kwe_ref[hh, k_rows[0], :], w_recent)]
        sw += [_dot(kwe_ref[hh, k_rows[i], 0:LANES], w_mid) for i in range(1, n_win - 1)]
        sw += [_dot(kwe_ref[hh, k_rows[n_win - 1], :], w_causal)]
        s_d = _dot(kse_ref[hh, pl.ds(pl.multiple_of(t0, Q_BLOCK), Q_BLOCK), :], w_causal)
        yield

        n_groups = n_slc // 8
        mine = [imp[8 * r:8 * r + 8, :] for r in range(n_groups)]
        ahead_count = [jnp.zeros((8, Q_BLOCK), F32)] * n_groups
        for j in range(n_slc):
            other = imp_ref[hh, j:j + 1, :]
            for r in range(n_groups):
                if 8 * r > j:
                    ahead = other >= mine[r]
                elif 8 * r + 7 <= j:
                    ahead = other > mine[r]
                else:
                    ahead = (other > mine[r]) | ((other == mine[r]) & (sub > j - 8 * r))
                ahead_count[r] = ahead_count[r] + jnp.where(ahead, 1.0, 0.0)
            if j % 8 == 7:
                yield
        sel = (jnp.concatenate(ahead_count, axis=0) < SLC_TOPK) & (imp > 0.5 * NEG)
        sel_bias = jnp.where(sel & (jb * SLC_BLOCK < t0), 0.0, NEG).astype(BF16)
        w_sel = jnp.concatenate([wq, _tile4(sel_bias)], axis=0)

        m_w = _col_max(sw[0])
        for s in sw[1:]:
            m_w = jnp.maximum(m_w, _col_max(s))
        yield
        p_w = soft(sw, m_w)
        yield
        o_win = None
        for idx, p in zip(v_idx, p_w):
            part = _dot(v_chunk(vw_ref, hh, idx), p)
            o_win = part if o_win is None else o_win + part
        m_d = _col_max(s_d)
        acc_d = _dot(v_chunk(vs_ref, hh, c), soft([s_d], m_d)[0])
        return w_sel, o_cmp, o_win, (m_d, acc_d)

    pre = _round_robin([before_sweep(hh) for hh in heads])

    def scores(hh, kt):
        k0 = pl.multiple_of(kt * KEY_TILE, KEY_TILE)
        return _dot(kse_ref[hh, pl.ds(k0, KEY_TILE), 0:LANES], pre[hh][0])

    def sweep(items, states, lookahead=2):
        states = list(states)
        s, m_new, p = {}, {}, {}

        def issue(n):
            s[n] = scores(*items[n])

        def value_matmul(n):
            hh, kt = items[n]
            m, acc = states[hh]
            states[hh] = m_new[n], jnp.exp2(m - m_new[n]) * acc + _dot(v_tile(hh, kt), p[n])

        for n in range(min(lookahead, len(items))):
            issue(n)
        for n, (hh, _) in enumerate(items):
            m_new[n] = jnp.maximum(states[hh][0], _col_max(s[n]))
            if n + lookahead < len(items):
                issue(n + lookahead)
            p[n] = jnp.exp2(s[n] - m_new[n]).astype(BF16)
            if n >= 1:
                value_matmul(n - 1)
        value_matmul(len(items) - 1)
        return tuple(states)

    def tile_pair(i, states):
        return sweep([(hh, 2 * i + k) for k in range(2) for hh in heads], states)

    def last_tile(states):
        return sweep([(hh, n_tiles - 1) for hh in heads], states)

    n_tiles = _div_pow2(t0 + KEY_TILE - 1, KEY_TILE)
    swept = lax.fori_loop(0, _div_pow2(n_tiles, 2), tile_pair, tuple(p[3] for p in pre))
    swept = lax.cond((n_tiles & 1) == 1, last_tile, lambda states: states, swept)

    for hh in heads:
        _, o_cmp, o_win, _ = pre[hh]
        o_slc = swept[hh][1]
        gates = gate_ref[0, hh * GROUP * N_GATES:(hh + 1) * GROUP * N_GATES, :]
        inv_ls = 1.0 / o_slc[HEAD_DIM:HEAD_DIM + 1, :]
        inv_lw = 1.0 / o_win[HEAD_DIM:HEAD_DIM + 1, :]
        pieces = []
        for g in range(GROUP):
            cs = slice(g * Q_BLOCK, (g + 1) * Q_BLOCK)
            row = g * N_GATES
            pieces.append(gates[row:row + 1, :] * o_cmp[:, cs]
                          + gates[row + 1:row + 2, :] * (o_slc[0:HEAD_DIM, cs] * inv_ls[:, cs])
                          + gates[row + 2:row + 3, :] * (o_win[0:HEAD_DIM, cs] * inv_lw[:, cs]))
        o_ref[0, hh * GROUP_WIDTH:(hh + 1) * GROUP_WIDTH, :] = jnp.concatenate(pieces, axis=0)


def _nsa(q_t, gates_t, kc, vc_t, ks, kw, vs_t, vw_t):
    bsz, _, seq = q_t.shape
    n_cmp = kc.shape[2]
    hps = HEADS_PER_STEP
    assert hps == N_KV
    k_spec = pl.BlockSpec((1, seq, KV_WIDTH), lambda b, h, c: (b, 0, 0))
    v_spec = pl.BlockSpec((1, seq // Q_BLOCK, hps * HEAD_DIM, Q_BLOCK), lambda b, h, c: (b, 0, h, 0))
    return pl.pallas_call(
        functools.partial(_nsa_kernel, seq=seq),
        name="nsa",
        grid=(bsz, N_KV // hps, seq // Q_BLOCK),
        in_specs=[
            pl.BlockSpec((1, hps * GROUP_WIDTH, Q_BLOCK), lambda b, h, c: (b, h, c)),
            pl.BlockSpec((1, GATE_ROWS, Q_BLOCK), lambda b, h, c: (b, 0, c)),
            pl.BlockSpec((1, hps, n_cmp, LANES), lambda b, h, c: (b, h, 0, 0)),
            pl.BlockSpec((1, hps, HEAD_DIM, n_cmp), lambda b, h, c: (b, h, 0, 0)),
            k_spec, k_spec, v_spec, v_spec,
        ],
        out_specs=pl.BlockSpec((1, hps * GROUP_WIDTH, Q_BLOCK), lambda b, h, c: (b, h, c)),
        out_shape=jax.ShapeDtypeStruct((bsz, ATTN_WIDTH, seq), F32),
        scratch_shapes=[
            pltpu.VMEM((hps, seq, 2 * LANES), BF16),
            pltpu.VMEM((hps, seq + Q_BLOCK, 2 * LANES), BF16),
            pltpu.VMEM((hps, seq // SLC_BLOCK, Q_BLOCK), F32),
        ],
        compiler_params=pltpu.CompilerParams(
            dimension_semantics=("parallel", "parallel", "arbitrary"),
            vmem_limit_bytes=VMEM_LIMIT),
    )(q_t, gates_t, kc, vc_t, ks, kw, vs_t, vw_t)


def _rope_tables(seq):
    pos = jnp.arange(seq, dtype=F32)
    inv = ROPE_THETA ** (-jnp.arange(0, HEAD_DIM, 2, dtype=F32) / HEAD_DIM)
    ang = pos[:, None] * inv[None, :]
    cos, sin = jnp.cos(ang), jnp.sin(ang)
    reps = LANES // HEAD_DIM
    return (jnp.tile(jnp.concatenate([cos, cos], axis=1), (1, reps)),
            jnp.tile(jnp.concatenate([-sin, sin], axis=1), (1, reps)))


def _compress_params(w1, w2, pe):
    eye = jnp.eye(HEADS_PER_TILE, dtype=w1.dtype)
    w1 = jnp.einsum('ldf,hk->lhdkf', w1, eye).reshape(CMP_BLOCK, LANES, HEADS_PER_TILE * CMP_HIDDEN)
    pe = jnp.tile(pe, (1, HEADS_PER_TILE)).reshape(CMP_BLOCK, 1, LANES)
    return pe, w1.astype(BF16), w2.astype(BF16)


def kernel(x, ffn1_norm, ffn1_wg, ffn1_wu, ffn1_wd, mix_norm, w_in, conv_w, cmp_k_w1, cmp_k_w2,
           cmp_k_pe, cmp_v_w1, cmp_v_w2, cmp_v_pe, conv_out_norm, attn_out_norm, w_out, ffn2_norm,
           ffn2_wg, ffn2_wu, ffn2_wd, final_norm):
    bsz, seq, d = x.shape
    assert CMP_BLOCK == 2 * CMP_STRIDE and seq % TOKEN_TILE == 0 and seq % KEY_TILE == 0
    assert x.shape[0] * seq % TOKEN_TILE == 0 and ffn1_wg.shape[0] == 1
    xt = x.reshape(bsz * seq, d)
    cos, sin = _rope_tables(seq)

    assert w_in.shape[1:] == (D_MODEL, IN_WIDTH)
    w = jnp.pad(w_in[0].astype(BF16), ((0, 0), (0, IN_WIDTH_PADDED - IN_WIDTH)))
    w_vt = jnp.concatenate([w[:, _VS0:_VS0 + KV_WIDTH], w[:, _VW0:_VW0 + KV_WIDTH]], axis=1).T

    x1, wg2, wu2, wd2, wo = _ffn1(xt, ffn1_norm, ffn1_wg[0].astype(BF16), ffn1_wu[0].astype(BF16),
                                  ffn1_wd[0].astype(BF16),
                                  [ffn2_wg[0], ffn2_wu[0], ffn2_wd[0], w_out[0]])
    yc = _conv_proj(x1, mix_norm, w, conv_w[0], conv_out_norm, seq)
    q_t, k_c, v_c, k_s, k_w, vs_t, vw_t, gates_t = _attn_proj(x1, mix_norm, w, w_vt, cos, sin, seq)

    pe_k, w1_k, w2_k = _compress_params(cmp_k_w1[0], cmp_k_w2[0], cmp_k_pe[0])
    pe_v, w1_v, w2_v = _compress_params(cmp_v_w1[0], cmp_v_w2[0], cmp_v_pe[0])
    kc = _compress(k_c, pe_k, w1_k, jnp.pad(w2_k, ((0, 0), (0, LANES - HEAD_DIM))), seq,
                   transposed=False)
    vc_t = _compress(v_c, pe_v, w1_v, w2_v.T, seq, transposed=True)

    per_batch = lambda a: a.reshape(bsz, seq, a.shape[-1])
    ya_t = _nsa(q_t, gates_t, kc, vc_t, per_batch(k_s), per_batch(k_w), vs_t, vw_t)

    out = _out_ffn2(x1, yc, ya_t, attn_out_norm, wo, ffn2_norm, wg2, wu2, wd2,
                    final_norm.reshape(1, d))
    return out.reshape(bsz, seq, d)
```

```python
import functools

import jax
import jax.numpy as jnp
from jax import lax
from jax.experimental import pallas as pl
from jax.experimental.pallas import tpu as pltpu

F32 = jnp.float32
BF16 = jnp.bfloat16

D_MODEL = 2048
D_FF = 5632
CONV_WIDTH = 1024
N_HEADS = 16
N_KV = 4
GROUP = N_HEADS // N_KV
HEAD_DIM = 64
ATTN_WIDTH = N_HEADS * HEAD_DIM
KV_WIDTH = N_KV * HEAD_DIM
CMP_BLOCK = 32
CMP_STRIDE = 16
CMP_HIDDEN = 256
SLC_BLOCK = 64
SLC_TOPK = 16
N_LOCAL = 2
WINDOW = 512
Q_BLOCK = 128
N_GATES = 3
ROPE_THETA = 10000.0
EPS = 1e-6
NEG = -1e30
FORCE = 1e9

LANES = 128
GROUP_WIDTH = GROUP * HEAD_DIM
GATE_ROWS = LANES
TOKEN_TILE = 512
FFN1_TOKEN_TILE = 1024
PROJ_TOKEN_TILE = 1024
FF_TILE = 512
KEY_TILE = 512
WIN_KEYS = WINDOW + Q_BLOCK
ONES_ROWS = 16
HEADS_PER_STEP = 4
VMEM_LIMIT = 56 * 1024 * 1024
FFN1_VMEM_LIMIT = 62 * 1024 * 1024
Q_SCALE = HEAD_DIM ** -0.5 * 1.4426950408889634


def _rms(x):
    return x * lax.rsqrt(jnp.mean(x * x, axis=-1, keepdims=True) + EPS)


def _dot(a, b):
    return jnp.dot(a, b, preferred_element_type=F32)


def _dot_nt(a, b):
    return lax.dot_general(a, b, (((1,), (1,)), ((), ())), preferred_element_type=F32)


def _div_pow2(x, n):
    assert n & (n - 1) == 0
    return jnp.right_shift(x, n.bit_length() - 1)


def _iota(shape, dim):
    return lax.broadcasted_iota(jnp.int32, shape, dim)


def _resident(shape):
    return pl.BlockSpec(shape, lambda *_: (0,) * len(shape), pipeline_mode=pl.Buffered(1))


def _round_robin(gens, skew=1):
    results = [None] * len(gens)
    live = {}
    waiting = dict(enumerate(gens))
    rnd = 0
    while waiting or live:
        for k in [k for k in waiting if rnd >= k * skew]:
            live[k] = waiting.pop(k)
        for k in sorted(live):
            try:
                next(live[k])
            except StopIteration as done:
                results[k] = done.value
                del live[k]
        rnd += 1
    return results


def _ffn_sweep(make_h, finish, wg_ref, wu_ref, wd_ref, h_ref, acc_ref):
    j = pl.program_id(1)
    last = pl.num_programs(1) - 1
    tm = h_ref.shape[0]
    halves = [slice(0, tm // 2), slice(tm // 2, tm)]

    def hidden(h):
        a = _dot(h, wg_ref[...])
        u = _dot(h, wu_ref[...])
        return (jax.nn.silu(a) * u).astype(BF16)

    @pl.when(j == 0)
    def _():
        def first(rows):
            h = yield from make_h(rows)
            h_ref[rows, :] = h
            yield
            act = hidden(h)
            yield
            acc_ref[rows, :] = _dot(act, wd_ref[...])

        _round_robin([first(rows) for rows in halves])

    @pl.when((j > 0) & (j < last))
    def _():
        acc_ref[...] += _dot(hidden(h_ref[...]), wd_ref[...])

    @pl.when(j == last)
    def _():
        def final(rows):
            act = hidden(h_ref[rows, :])
            yield
            finish(rows, acc_ref[rows, :] + _dot(act, wd_ref[...]))

        _round_robin([final(rows) for rows in halves])


def _ffn1_kernel(x_ref, g_ref, wg_ref, wu_ref, wd_ref, *refs, n_cast):
    cast_in = refs[:n_cast]
    o_ref = refs[n_cast]
    cast_out = refs[n_cast + 1:2 * n_cast + 1]
    (h_ref,) = refs[2 * n_cast + 1:]
    acc_ref = o_ref
    for src, dst in zip(cast_in, cast_out):
        dst[...] = src[...].astype(BF16)

    def make_h(rows):
        return (_rms(x_ref[rows, :]) * g_ref[...]).astype(BF16)
        yield

    def finish(rows, acc):
        o_ref[rows, :] = x_ref[rows, :] + 0.5 * acc

    _ffn_sweep(make_h, finish, wg_ref, wu_ref, wd_ref, h_ref, acc_ref)


def _cast_block(shape, ni, nj):
    rows, cols = shape
    sub, lane = 16, LANES
    if rows % ni == 0 and cols % nj == 0 and (rows // ni) % sub == 0 and (cols // nj) % lane == 0:
        return pl.BlockSpec((rows // ni, cols // nj), lambda i, j: (i, j))
    if rows % (ni * nj) == 0 and (rows // (ni * nj)) % sub == 0:
        return pl.BlockSpec((rows // (ni * nj), cols), lambda i, j: (i * nj + j, 0))
    assert rows % ni == 0 and (rows // ni) % sub == 0
    return pl.BlockSpec((rows // ni, cols), lambda i, j: (i, 0))


def _ffn1(x, g, wg, wu, wd, later_weights):
    t, d = x.shape
    f = wg.shape[1]
    tm = FFN1_TOKEN_TILE
    grid = (t // tm, f // FF_TILE)
    cast_specs = [_cast_block(w.shape, *grid) for w in later_weights]
    return pl.pallas_call(
        functools.partial(_ffn1_kernel, n_cast=len(later_weights)),
        name="ffn1",
        grid=grid,
        in_specs=[
            pl.BlockSpec((tm, d), lambda i, j: (i, 0)),
            pl.BlockSpec((1, d), lambda i, j: (0, 0)),
            pl.BlockSpec((d, FF_TILE), lambda i, j: (0, j)),
            pl.BlockSpec((d, FF_TILE), lambda i, j: (0, j)),
            pl.BlockSpec((FF_TILE, d), lambda i, j: (j, 0)),
        ] + cast_specs,
        out_specs=[pl.BlockSpec((tm, d), lambda i, j: (i, 0))] + cast_specs,
        out_shape=[jax.ShapeDtypeStruct((t, d), F32)] + [
            jax.ShapeDtypeStruct(w.shape, BF16) for w in later_weights],
        scratch_shapes=[pltpu.VMEM((tm, d), BF16)],
        compiler_params=pltpu.CompilerParams(
            dimension_semantics=("arbitrary", "arbitrary"), vmem_limit_bytes=FFN1_VMEM_LIMIT),
    )(x, g, wg, wu, wd, *later_weights)


def _out_ffn2_kernel(x_ref, yc_ref, ya_ref, ga_ref, wo_ref, g_ref, wg_ref, wu_ref, wd_ref,
                     fin_ref, o_ref, h_ref, acc_ref):
    def make_h(rows):
        ya = (_rms(ya_ref[0, :, rows].T) * ga_ref[...]).astype(BF16)
        yield
        mix = _dot(yc_ref[rows, :], wo_ref[0:CONV_WIDTH, :]) + _dot(ya, wo_ref[CONV_WIDTH:, :])
        yield
        x2 = x_ref[rows, :] + mix
        o_ref[rows, :] = x2
        return (_rms(x2) * g_ref[...]).astype(BF16)

    def finish(rows, acc):
        o_ref[rows, :] = _rms(o_ref[rows, :] + 0.5 * acc) * fin_ref[...]

    _ffn_sweep(make_h, finish, wg_ref, wu_ref, wd_ref, h_ref, acc_ref)


def _out_ffn2(x1, yc, ya_t, ga, wo, g, wg, wu, wd, fin):
    t, d = x1.shape
    f = wg.shape[1]
    tm = TOKEN_TILE
    tps = ya_t.shape[2] // tm
    return pl.pallas_call(
        _out_ffn2_kernel,
        name="out_ffn2",
        grid=(t // tm, f // FF_TILE),
        in_specs=[
            pl.BlockSpec((tm, d), lambda i, j: (i, 0)),
            pl.BlockSpec((tm, CONV_WIDTH), lambda i, j: (i, 0)),
            pl.BlockSpec((1, ATTN_WIDTH, tm), lambda i, j: (i // tps, 0, i % tps)),
            pl.BlockSpec((1, ATTN_WIDTH), lambda i, j: (0, 0)),
            _resident(wo.shape),
            pl.BlockSpec((1, d), lambda i, j: (0, 0)),
            pl.BlockSpec((d, FF_TILE), lambda i, j: (0, j)),
            pl.BlockSpec((d, FF_TILE), lambda i, j: (0, j)),
            pl.BlockSpec((FF_TILE, d), lambda i, j: (j, 0)),
            pl.BlockSpec((1, d), lambda i, j: (0, 0)),
        ],
        out_specs=pl.BlockSpec((tm, d), lambda i, j: (i, 0)),
        out_shape=jax.ShapeDtypeStruct((t, d), F32),
        scratch_shapes=[pltpu.VMEM((tm, d), BF16), pltpu.VMEM((tm, d), F32)],
        compiler_params=pltpu.CompilerParams(
            dimension_semantics=("parallel", "arbitrary"), vmem_limit_bytes=VMEM_LIMIT),
    )(x1, yc, ya_t, ga, wo, g, wg, wu, wd, fin)


def _conv_proj_kernel(x_ref, g_ref, w_ref, cw_ref, gc_ref, o_ref, carry_ref, *, tiles_per_seq):
    i = pl.program_id(0)

    @pl.when(i % tiles_per_seq == 0)
    def _():
        carry_ref[...] = jnp.zeros_like(carry_ref)

    tm = x_ref.shape[0]
    half = tm // 2
    tails = {-1: carry_ref[...]}

    def one_half(k):
        rows = slice(k * half, (k + 1) * half)
        h = (_rms(x_ref[rows, :]) * g_ref[...]).astype(BF16)
        yield
        c_h = _dot(h, w_ref[:, 0:CONV_WIDTH])
        c_c = _dot(h, w_ref[:, 2 * CONV_WIDTH:3 * CONV_WIDTH])
        u = c_c * c_h
        tails[k] = u[half - 8:half, :]
        c_b = _dot(h, w_ref[:, CONV_WIDTH:2 * CONV_WIDTH])
        yield
        row = lax.broadcasted_iota(jnp.int32, u.shape, 0)
        prev1 = tails[k - 1][7:8, :]
        prev2 = tails[k - 1][6:7, :]
        u1 = jnp.where(row == 0, prev1, pltpu.roll(u, 1, 0))
        u2 = jnp.where(row == 0, prev2, jnp.where(row == 1, prev1, pltpu.roll(u, 2, 0)))
        y = c_b * (cw_ref[0:1, :] * u2 + cw_ref[1:2, :] * u1 + cw_ref[2:3, :] * u)
        o_ref[rows, :] = (_rms(y) * gc_ref[...]).astype(BF16)

    _round_robin([one_half(0), one_half(1)])
    carry_ref[...] = tails[1]


def _conv_proj(x1, g, w_in, conv_w, gc, seq):
    t, d = x1.shape
    tm = PROJ_TOKEN_TILE
    return pl.pallas_call(
        functools.partial(_conv_proj_kernel, tiles_per_seq=seq // tm),
        name="conv_proj",
        grid=(t // tm,),
        in_specs=[
            pl.BlockSpec((tm, d), lambda i: (i, 0)),
            pl.BlockSpec((1, d), lambda i: (0, 0)),
            _w_in_window(_CONV0, 3 * CONV_WIDTH),
            pl.BlockSpec(conv_w.shape, lambda i: (0, 0)),
            pl.BlockSpec((1, CONV_WIDTH), lambda i: (0, 0)),
        ],
        out_specs=pl.BlockSpec((tm, CONV_WIDTH), lambda i: (i, 0)),
        out_shape=jax.ShapeDtypeStruct((t, CONV_WIDTH), BF16),
        scratch_shapes=[pltpu.VMEM((8, CONV_WIDTH), F32)],
        compiler_params=pltpu.CompilerParams(
            dimension_semantics=("arbitrary",), vmem_limit_bytes=VMEM_LIMIT),
    )(x1, g, w_in, conv_w, gc)


def _rope(x, cos, sin_signed, first_half):
    outs = []
    for k in range(x.shape[1] // LANES):
        xc = x[:, k * LANES:(k + 1) * LANES]
        partner = jnp.where(first_half, pltpu.roll(xc, LANES - HEAD_DIM // 2, 1),
                            pltpu.roll(xc, HEAD_DIM // 2, 1))
        outs.append(xc * cos + partner * sin_signed)
    return outs


_CONV0 = 0
_Q0 = 3 * CONV_WIDTH
_KC0 = _Q0 + ATTN_WIDTH
_VC0 = _KC0 + KV_WIDTH
_KS0 = _VC0 + KV_WIDTH
_VS0 = _KS0 + KV_WIDTH
_KW0 = _VS0 + KV_WIDTH
_VW0 = _KW0 + KV_WIDTH
_G0 = _VW0 + KV_WIDTH
IN_WIDTH = _G0 + N_HEADS * N_GATES
IN_WIDTH_PADDED = -(-IN_WIDTH // LANES) * LANES


def _w_in_window(c0, width):
    assert c0 % width == 0
    return pl.BlockSpec((D_MODEL, width), lambda *_: (0, c0 // width), pipeline_mode=pl.Buffered(1))


def _attn_proj_kernel(x_ref, g_ref, wq_ref, wkc_ref, wvc_ref, wks_ref, wkw_ref, wgate_ref, wvt_ref,
                      cos_ref, sin_ref,
                      q_ref, kc_ref, vc_ref, ks_ref, kw_ref, vs_ref, vw_ref, gate_ref):
    tm = x_ref.shape[0]
    half = tm // 2
    lane = lax.broadcasted_iota(jnp.int32, (half, LANES), 1)
    first_half = (lane & (HEAD_DIM - 1)) < HEAD_DIM // 2

    def one_half(k):
        rows = slice(k * half, (k + 1) * half)
        h = (_rms(x_ref[rows, :]) * g_ref[...]).astype(BF16)
        cos = cos_ref[rows, :]
        sin = sin_ref[rows, :]

        def proj(w_ref):
            return _dot(h, w_ref[...])

        def rope_to(ref, w_ref, dtype):
            for n, piece in enumerate(_rope(proj(w_ref), cos, sin, first_half)):
                ref[rows, n * LANES:(n + 1) * LANES] = piece.astype(dtype)

        yield
        for n, piece in enumerate(_rope(proj(wq_ref), cos, sin, first_half)):
            q_ref[0, n * LANES:(n + 1) * LANES, rows] = (piece * Q_SCALE).T.astype(BF16)
        yield
        rope_to(kc_ref, wkc_ref, F32)
        vc_ref[rows, :] = proj(wvc_ref)
        rope_to(ks_ref, wks_ref, BF16)
        rope_to(kw_ref, wkw_ref, BF16)
        yield
        gate_ref[0, :, rows] = jax.nn.sigmoid(proj(wgate_ref)).T
        vt = _dot_nt(wvt_ref[...], h)
        chunks = half // Q_BLOCK
        for n in range(chunks):
            cs = slice(n * Q_BLOCK, (n + 1) * Q_BLOCK)
            vs_ref[0, k * chunks + n] = vt[0:KV_WIDTH, cs].astype(BF16)
            vw_ref[0, k * chunks + n] = vt[KV_WIDTH:2 * KV_WIDTH, cs].astype(BF16)

    _round_robin([one_half(0), one_half(1)])


def _attn_proj(x1, g, w_in, w_vt, cos, sin, seq):
    t, d = x1.shape
    tm = PROJ_TOKEN_TILE
    tps = seq // tm
    bsz = t // seq
    cpt = tm // Q_BLOCK
    row = lambda w: pl.BlockSpec((tm, w), lambda i: (i, 0))
    col = lambda w: pl.BlockSpec((1, w, tm), lambda i: (i // tps, 0, i % tps))
    chunked = pl.BlockSpec((1, cpt, KV_WIDTH, Q_BLOCK), lambda i: (i // tps, i % tps, 0, 0))
    kv_dtypes = [F32, F32, BF16, BF16]
    out_specs = [col(ATTN_WIDTH)] + [row(KV_WIDTH)] * 4 + [chunked, chunked, col(GATE_ROWS)]
    out_shape = ([jax.ShapeDtypeStruct((bsz, ATTN_WIDTH, seq), BF16)]
                 + [jax.ShapeDtypeStruct((t, KV_WIDTH), dt) for dt in kv_dtypes]
                 + [jax.ShapeDtypeStruct((bsz, seq // Q_BLOCK, KV_WIDTH, Q_BLOCK), BF16)] * 2
                 + [jax.ShapeDtypeStruct((bsz, GATE_ROWS, seq), F32)])
    windows = [_w_in_window(_Q0, ATTN_WIDTH)] + [
        _w_in_window(c0, KV_WIDTH) for c0 in (_KC0, _VC0, _KS0, _KW0)] + [
        _w_in_window(_G0, GATE_ROWS)]
    return pl.pallas_call(
        _attn_proj_kernel,
        name="attn_proj",
        grid=(t // tm,),
        in_specs=[row(d), pl.BlockSpec((1, d), lambda i: (0, 0))] + windows + [
            _resident(w_vt.shape),
            pl.BlockSpec((tm, LANES), lambda i: (i % tps, 0)),
            pl.BlockSpec((tm, LANES), lambda i: (i % tps, 0)),
        ],
        out_specs=out_specs,
        out_shape=out_shape,
        compiler_params=pltpu.CompilerParams(
            dimension_semantics=("parallel",), vmem_limit_bytes=VMEM_LIMIT),
    )(x1, g, *([w_in] * len(windows)), w_vt, cos, sin)


HEADS_PER_TILE = LANES // HEAD_DIM


def _compress_kernel(*refs, transposed):
    tok_refs = refs[:N_KV // HEADS_PER_TILE]
    pe_ref, w1_ref, w2_ref, o_ref = refs[N_KV // HEADS_PER_TILE:]
    n = tok_refs[0].shape[0] // CMP_STRIDE
    for t, tok_ref in enumerate(tok_refs):
        first = jnp.zeros((n, HEADS_PER_TILE * CMP_HIDDEN), F32)
        second = jnp.zeros((n, HEADS_PER_TILE * CMP_HIDDEN), F32)
        for l in range(CMP_STRIDE):
            rows = tok_ref[pl.ds(l, n, stride=CMP_STRIDE), :]
            first = first + _dot((rows + pe_ref[l]).astype(BF16), w1_ref[l])
            second = second + _dot((rows + pe_ref[l + CMP_STRIDE]).astype(BF16),
                                   w1_ref[l + CMP_STRIDE])
        hid = jax.nn.gelu(first + pltpu.roll(second, n - 1, 0)).astype(BF16)
        for k in range(HEADS_PER_TILE):
            hid_h = hid[:, k * CMP_HIDDEN:(k + 1) * CMP_HIDDEN]
            h = t * HEADS_PER_TILE + k
            if transposed:
                o_ref[0, h] = _dot_nt(w2_ref[...], hid_h).astype(BF16)
            else:
                o_ref[0, h] = _dot(hid_h, w2_ref[...]).astype(BF16)


def _compress(tok, pe, w1, w2, seq, transposed):
    bsz = tok.shape[0] // seq
    n = seq // CMP_STRIDE
    out_block = (1, N_KV, HEAD_DIM, n) if transposed else (1, N_KV, n, LANES)
    lane_tile = lambda t: pl.BlockSpec((seq, LANES), lambda b: (b, t))
    return pl.pallas_call(
        functools.partial(_compress_kernel, transposed=transposed),
        name="compress_v" if transposed else "compress_k",
        grid=(bsz,),
        in_specs=[lane_tile(t) for t in range(N_KV // HEADS_PER_TILE)] + [
            _resident(pe.shape),
            _resident(w1.shape),
            _resident(w2.shape),
        ],
        out_specs=pl.BlockSpec(out_block, lambda b: (b, 0, 0, 0)),
        out_shape=jax.ShapeDtypeStruct((bsz,) + out_block[1:], BF16),
        compiler_params=pltpu.CompilerParams(
            dimension_semantics=("parallel",), vmem_limit_bytes=VMEM_LIMIT),
    )(*([tok] * (N_KV // HEADS_PER_TILE)), pe, w1, w2)


def _tile4(x):
    return jnp.concatenate([x] * GROUP, axis=1)


def _col_max(x):
    return jnp.max(x, axis=0, keepdims=True)


def _col_sum(x):
    return jnp.sum(x, axis=0, keepdims=True)


def _with_ones(v):
    ones = jnp.where(_iota((ONES_ROWS, v.shape[1]), 0) == 0, 1.0, 0.0).astype(v.dtype)
    return jnp.concatenate([v, ones], axis=0)


def _nsa_kernel(q_ref, gate_ref, kc_ref, vc_ref, ks_ref, kw_ref, vs_ref, vw_ref, o_ref,
                kse_ref, kwe_ref, imp_ref, *, seq):
    hp = pl.program_id(1)
    c = pl.program_id(2)
    t0 = c * Q_BLOCK
    n_cmp = kc_ref.shape[2]
    n_slc = seq // SLC_BLOCK
    cols = GROUP * Q_BLOCK
    n_chunks = seq // Q_BLOCK
    chunks_per_tile = KEY_TILE // Q_BLOCK
    heads = range(HEADS_PER_STEP)
    assert n_slc + HEAD_DIM == LANES and n_slc % 8 == 0

    @pl.when(c == 0)
    def _():
        r = _iota((KV_WIDTH, 2 * LANES), 0)
        ln = _iota((KV_WIDTH, 2 * LANES), 1)
        for hh in heads:
            h = hp * HEADS_PER_STEP + hh
            pick = ((r == h * HEAD_DIM + ln) & (ln < HEAD_DIM)).astype(BF16)

            def build_k(i, _, hh=hh, pick=pick):
                sl = pl.ds(pl.multiple_of(i * KEY_TILE, KEY_TILE), KEY_TILE)
                pos = i * KEY_TILE + _iota((KEY_TILE, 2 * LANES), 0)
                lane = _iota((KEY_TILE, 2 * LANES), 1)
                in_chunk = lane - LANES == (pos & (Q_BLOCK - 1))
                in_block = lane - HEAD_DIM == _div_pow2(pos, SLC_BLOCK)
                kse_ref[hh, sl, :] = (_dot(ks_ref[0, sl, :], pick)
                                      + jnp.where(in_chunk | in_block, 1.0, 0.0)).astype(BF16)
                kwe_ref[hh, sl, :] = (_dot(kw_ref[0, sl, :], pick)
                                      + jnp.where(in_chunk, 1.0, 0.0)).astype(BF16)
                return 0

            lax.fori_loop(0, seq // KEY_TILE, build_k, 0)
            kwe_ref[hh, pl.ds(seq, Q_BLOCK), :] = jnp.where(
                _iota((Q_BLOCK, 2 * LANES), 1) == HEAD_DIM, 1.0, 0.0).astype(BF16)

    def v_chunk(ref, hh, idx):
        return _with_ones(ref[0, idx, hh * HEAD_DIM:(hh + 1) * HEAD_DIM, :])

    def v_tile(hh, kt):
        return _with_ones(jnp.concatenate(
            [vs_ref[0, kt * chunks_per_tile + k, hh * HEAD_DIM:(hh + 1) * HEAD_DIM, :]
             for k in range(chunks_per_tile)], axis=1))

    def soft(scores, m):
        return [jnp.exp2(s - m).astype(BF16) for s in scores]

    kpos = _iota((Q_BLOCK, Q_BLOCK), 0)
    tpos = _iota((Q_BLOCK, Q_BLOCK), 1)
    causal = _tile4(jnp.where(kpos <= tpos, 0.0, NEG).astype(BF16))
    recent = _tile4(jnp.where(kpos > tpos, 0.0, NEG).astype(BF16))
    pad = jnp.zeros((HEAD_DIM, cols), BF16)
    dummy_row = jnp.where(_iota((HEAD_DIM, cols), 0) == 0, NEG, 0.0).astype(BF16)
    jj = _iota((n_slc, n_cmp), 0)
    ii = _iota((n_slc, n_cmp), 1)
    ov = (jnp.minimum(ii * CMP_STRIDE + CMP_BLOCK, jj * SLC_BLOCK + SLC_BLOCK)
          - jnp.maximum(ii * CMP_STRIDE, jj * SLC_BLOCK))
    ov = (jnp.maximum(ov, 0).astype(F32) * (1.0 / CMP_BLOCK)).astype(BF16)
    c_ok = (_iota((n_cmp, Q_BLOCK), 0) * CMP_STRIDE + (CMP_BLOCK - 1)
            <= t0 + _iota((n_cmp, Q_BLOCK), 1))
    c_bias = _tile4(jnp.where(c_ok, 0.0, NEG))
    any_ok = t0 + (_iota((1, cols), 1) & (Q_BLOCK - 1)) >= CMP_BLOCK - 1
    jb = _iota((n_slc, Q_BLOCK), 0)
    tb = _div_pow2(t0 + _iota((n_slc, Q_BLOCK), 1), SLC_BLOCK)
    forced = (jb == 0) | ((jb <= tb) & (jb > tb - N_LOCAL))
    sub = _iota((8, Q_BLOCK), 0)
    n_win = WIN_KEYS // Q_BLOCK
    first = c - (n_win - 1)
    k_chunk = [jnp.where(first + i >= 0, first + i, n_chunks) for i in range(n_win)]
    v_idx = [jnp.maximum(first + i, 0) for i in range(n_win)]
    k_rows = [pl.ds(pl.multiple_of(ch * Q_BLOCK, Q_BLOCK), Q_BLOCK) for ch in k_chunk]

    def before_sweep(hh):
        qt = q_ref[0, hh * GROUP_WIDTH:(hh + 1) * GROUP_WIDTH, :]
        wq = jnp.concatenate([qt[g * HEAD_DIM:(g + 1) * HEAD_DIM, :] for g in range(GROUP)], axis=1)

        sc = _dot(kc_ref[0, hh], jnp.concatenate([wq, pad], axis=0)) + c_bias
        yield
        pc = jnp.exp2(sc - _col_max(sc))
        pc = pc * jnp.where(any_ok, 1.0 / _col_sum(pc), 0.0)
        yield
        o_cmp = _dot(vc_ref[0, hh], pc.astype(BF16))

        pcs = pc[:, 0:Q_BLOCK]
        for g in range(1, GROUP):
            pcs = pcs + pc[:, g * Q_BLOCK:(g + 1) * Q_BLOCK]
        pcs_hi = pcs.astype(BF16)
        pcs_lo = (pcs - pcs_hi.astype(F32)).astype(BF16)
        imp = _dot(ov, pcs_hi) + _dot(ov, pcs_lo)
        yield
        imp = jnp.where(forced, FORCE, imp)
        imp = jnp.where(jb <= tb, imp, NEG)
        imp_ref[hh] = imp

        w_causal = jnp.concatenate([wq, pad, causal], axis=0)
        w_mid = jnp.concatenate([wq, dummy_row], axis=0)
        w_recent = jnp.concatenate([wq, dummy_row, recent], axis=0)
        sw = [_dot(kwe_ref[hh, k_rows[0], :], w_recent)]
        sw += [_dot(kwe_ref[hh, k_rows[i], 0:LANES], w_mid) for i in range(1, n_win - 1)]
        sw += [_dot(kwe_ref[hh, k_rows[n_win - 1], :], w_causal)]
        s_d = _dot(kse_ref[hh, pl.ds(pl.multiple_of(t0, Q_BLOCK), Q_BLOCK), :], w_causal)
        yield

        n_groups = n_slc // 8
        mine = [imp[8 * r:8 * r + 8, :] for r in range(n_groups)]
        ahead_count = [jnp.zeros((8, Q_BLOCK), F32)] * n_groups
        for j in range(n_slc):
            other = imp_ref[hh, j:j + 1, :]
            for r in range(n_groups):
                if 8 * r > j:
                    ahead = other >= mine[r]
                elif 8 * r + 7 <= j:
                    ahead = other > mine[r]
                else:
                    ahead = (other > mine[r]) | ((other == mine[r]) & (sub > j - 8 * r))
                ahead_count[r] = ahead_count[r] + jnp.where(ahead, 1.0, 0.0)
            if j % 8 == 7:
                yield
        sel = (jnp.concatenate(ahead_count, axis=0) < SLC_TOPK) & (imp > 0.5 * NEG)
        sel_bias = jnp.where(sel & (jb * SLC_BLOCK < t0), 0.0, NEG).astype(BF16)
        w_sel = jnp.concatenate([wq, _tile4(sel_bias)], axis=0)

        m_w = _col_max(sw[0])
        for s in sw[1:]:
            m_w = jnp.maximum(m_w, _col_max(s))
        yield
        p_w = soft(sw, m_w)
        yield
        o_win = None
        for idx, p in zip(v_idx, p_w):
            part = _dot(v_chunk(vw_ref, hh, idx), p)
            o_win = part if o_win is None else o_win + part
        m_d = _col_max(s_d)
        acc_d = _dot(v_chunk(vs_ref, hh, c), soft([s_d], m_d)[0])
        return w_sel, o_cmp, o_win, (m_d, acc_d)

    pre = _round_robin([before_sweep(hh) for hh in heads])

    def scores(hh, kt):
        k0 = pl.multiple_of(kt * KEY_TILE, KEY_TILE)
        return _dot(kse_ref[hh, pl.ds(k0, KEY_TILE), 0:LANES], pre[hh][0])

    def sweep(items, states, lookahead=2):
        states = list(states)
        s, m_new, p = {}, {}, {}

        def issue(n):
            s[n] = scores(*items[n])

        def value_matmul(n):
            hh, kt = items[n]
            m, acc = states[hh]
            states[hh] = m_new[n], jnp.exp2(m - m_new[n]) * acc + _dot(v_tile(hh, kt), p[n])

        for n in range(min(lookahead, len(items))):
            issue(n)
        for n, (hh, _) in enumerate(items):
            m_new[n] = jnp.maximum(states[hh][0], _col_max(s[n]))
            if n + lookahead < len(items):
                issue(n + lookahead)
            p[n] = jnp.exp2(s[n] - m_new[n]).astype(BF16)
            if n >= 1:
                value_matmul(n - 1)
        value_matmul(len(items) - 1)
        return tuple(states)

    def tile_pair(i, states):
        return sweep([(hh, 2 * i + k) for k in range(2) for hh in heads], states)

    def last_tile(states):
        return sweep([(hh, n_tiles - 1) for hh in heads], states)

    n_tiles = _div_pow2(t0 + KEY_TILE - 1, KEY_TILE)
    swept = lax.fori_loop(0, _div_pow2(n_tiles, 2), tile_pair, tuple(p[3] for p in pre))
    swept = lax.cond((n_tiles & 1) == 1, last_tile, lambda states: states, swept)

    for hh in heads:
        _, o_cmp, o_win, _ = pre[hh]
        o_slc = swept[hh][1]
        gates = gate_ref[0, hh * GROUP * N_GATES:(hh + 1) * GROUP * N_GATES, :]
        inv_ls = 1.0 / o_slc[HEAD_DIM:HEAD_DIM + 1, :]
        inv_lw = 1.0 / o_win[HEAD_DIM:HEAD_DIM + 1, :]
        pieces = []
        for g in range(GROUP):
            cs = slice(g * Q_BLOCK, (g + 1) * Q_BLOCK)
            row = g * N_GATES
            pieces.append(gates[row:row + 1, :] * o_cmp[:, cs]
                          + gates[row + 1:row + 2, :] * (o_slc[0:HEAD_DIM, cs] * inv_ls[:, cs])
                          + gates[row + 2:row + 3, :] * (o_win[0:HEAD_DIM, cs] * inv_lw[:, cs]))
        o_ref[0, hh * GROUP_WIDTH:(hh + 1) * GROUP_WIDTH, :] = jnp.concatenate(pieces, axis=0)


def _nsa(q_t, gates_t, kc, vc_t, ks, kw, vs_t, vw_t):
    bsz, _, seq = q_t.shape
    n_cmp = kc.shape[2]
    hps = HEADS_PER_STEP
    assert hps == N_KV
    k_spec = pl.BlockSpec((1, seq, KV_WIDTH), lambda b, h, c: (b, 0, 0))
    v_spec = pl.BlockSpec((1, seq // Q_BLOCK, hps * HEAD_DIM, Q_BLOCK), lambda b, h, c: (b, 0, h, 0))
    return pl.pallas_call(
        functools.partial(_nsa_kernel, seq=seq),
        name="nsa",
        grid=(bsz, N_KV // hps, seq // Q_BLOCK),
        in_specs=[
            pl.BlockSpec((1, hps * GROUP_WIDTH, Q_BLOCK), lambda b, h, c: (b, h, c)),
            pl.BlockSpec((1, GATE_ROWS, Q_BLOCK), lambda b, h, c: (b, 0, c)),
            pl.BlockSpec((1, hps, n_cmp, LANES), lambda b, h, c: (b, h, 0, 0)),
            pl.BlockSpec((1, hps, HEAD_DIM, n_cmp), lambda b, h, c: (b, h, 0, 0)),
            k_spec, k_spec, v_spec, v_spec,
        ],
        out_specs=pl.BlockSpec((1, hps * GROUP_WIDTH, Q_BLOCK), lambda b, h, c: (b, h, c)),
        out_shape=jax.ShapeDtypeStruct((bsz, ATTN_WIDTH, seq), F32),
        scratch_shapes=[
            pltpu.VMEM((hps, seq, 2 * LANES), BF16),
            pltpu.VMEM((hps, seq + Q_BLOCK, 2 * LANES), BF16),
            pltpu.VMEM((hps, seq // SLC_BLOCK, Q_BLOCK), F32),
        ],
        compiler_params=pltpu.CompilerParams(
            dimension_semantics=("parallel", "parallel", "arbitrary"),
            vmem_limit_bytes=VMEM_LIMIT),
    )(q_t, gates_t, kc, vc_t, ks, kw, vs_t, vw_t)


def _rope_tables(seq):
    pos = jnp.arange(seq, dtype=F32)
    inv = ROPE_THETA ** (-jnp.arange(0, HEAD_DIM, 2, dtype=F32) / HEAD_DIM)
    ang = pos[:, None] * inv[None, :]
    cos, sin = jnp.cos(ang), jnp.sin(ang)
    reps = LANES // HEAD_DIM
    return (jnp.tile(jnp.concatenate([cos, cos], axis=1), (1, reps)),
            jnp.tile(jnp.concatenate([-sin, sin], axis=1), (1, reps)))


def _compress_params(w1, w2, pe):
    eye = jnp.eye(HEADS_PER_TILE, dtype=w1.dtype)
    w1 = jnp.einsum('ldf,hk->lhdkf', w1, eye).reshape(CMP_BLOCK, LANES, HEADS_PER_TILE * CMP_HIDDEN)
    pe = jnp.tile(pe, (1, HEADS_PER_TILE)).reshape(CMP_BLOCK, 1, LANES)
    return pe, w1.astype(BF16), w2.astype(BF16)


def kernel(x, ffn1_norm, ffn1_wg, ffn1_wu, ffn1_wd, mix_norm, w_in, conv_w, cmp_k_w1, cmp_k_w2,
           cmp_k_pe, cmp_v_w1, cmp_v_w2, cmp_v_pe, conv_out_norm, attn_out_norm, w_out, ffn2_norm,
           ffn2_wg, ffn2_wu, ffn2_wd, final_norm):
    bsz, seq, d = x.shape
    assert CMP_BLOCK == 2 * CMP_STRIDE and seq % TOKEN_TILE == 0 and seq % KEY_TILE == 0
    assert x.shape[0] * seq % TOKEN_TILE == 0 and ffn1_wg.shape[0] == 1
    xt = x.reshape(bsz * seq, d)
    cos, sin = _rope_tables(seq)

    assert w_in.shape[1:] == (D_MODEL, IN_WIDTH)
    w = jnp.pad(w_in[0].astype(BF16), ((0, 0), (0, IN_WIDTH_PADDED - IN_WIDTH)))
    w_vt = jnp.concatenate([w[:, _VS0:_VS0 + KV_WIDTH], w[:, _VW0:_VW0 + KV_WIDTH]], axis=1).T

    x1, wg2, wu2, wd2, wo = _ffn1(xt, ffn1_norm, ffn1_wg[0].astype(BF16), ffn1_wu[0].astype(BF16),
                                  ffn1_wd[0].astype(BF16),
                                  [ffn2_wg[0], ffn2_wu[0], ffn2_wd[0], w_out[0]])
    yc = _conv_proj(x1, mix_norm, w, conv_w[0], conv_out_norm, seq)
    q_t, k_c, v_c, k_s, k_w, vs_t, vw_t, gates_t = _attn_proj(x1, mix_norm, w, w_vt, cos, sin, seq)

    pe_k, w1_k, w2_k = _compress_params(cmp_k_w1[0], cmp_k_w2[0], cmp_k_pe[0])
    pe_v, w1_v, w2_v = _compress_params(cmp_v_w1[0], cmp_v_w2[0], cmp_v_pe[0])
    kc = _compress(k_c, pe_k, w1_k, jnp.pad(w2_k, ((0, 0), (0, LANES - HEAD_DIM))), seq,
                   transposed=False)
    vc_t = _compress(v_c, pe_v, w1_v, w2_v.T, seq, transposed=True)

    per_batch = lambda a: a.reshape(bsz, seq, a.shape[-1])
    ya_t = _nsa(q_t, gates_t, kc, vc_t, per_batch(k_s), per_batch(k_w), vs_t, vw_t)

    out = _out_ffn2(x1, yc, ya_t, attn_out_norm, wo, ffn2_norm, wg2, wu2, wd2,
                    final_norm.reshape(1, d))
    return out.reshape(bsz, seq, d)
```

```python
import functools

import jax
import jax.numpy as jnp
from jax import lax
from jax.experimental import pallas as pl
from jax.experimental.pallas import tpu as pltpu

F32 = jnp.float32
BF16 = jnp.bfloat16

D_MODEL = 2048
D_FF = 5632
CONV_WIDTH = 1024
N_HEADS = 16
N_KV = 4
GROUP = N_HEADS // N_KV
HEAD_DIM = 64
ATTN_WIDTH = N_HEADS * HEAD_DIM
KV_WIDTH = N_KV * HEAD_DIM
CMP_BLOCK = 32
CMP_STRIDE = 16
CMP_HIDDEN = 256
SLC_BLOCK = 64
SLC_TOPK = 16
N_LOCAL = 2
WINDOW = 512
Q_BLOCK = 128
N_GATES = 3
ROPE_THETA = 10000.0
EPS = 1e-6
NEG = -1e30
FORCE = 1e9

LANES = 128
GROUP_WIDTH = GROUP * HEAD_DIM
GATE_ROWS = LANES
TOKEN_TILE = 512
FFN_TOKEN_TILE = 1024
FF_TILE = 512
KEY_TILE = 512
WIN_KEYS = WINDOW + Q_BLOCK
ONES_ROWS = 16
HEADS_PER_STEP = 4
VMEM_LIMIT = 56 * 1024 * 1024
FFN_VMEM_LIMIT = 62 * 1024 * 1024
Q_SCALE = HEAD_DIM ** -0.5 * 1.4426950408889634


def _rms(x):
    return x * lax.rsqrt(jnp.mean(x * x, axis=-1, keepdims=True) + EPS)


def _dot(a, b):
    return jnp.dot(a, b, preferred_element_type=F32)


def _dot_nt(a, b):
    return lax.dot_general(a, b, (((1,), (1,)), ((), ())), preferred_element_type=F32)


def _div_pow2(x, n):
    assert n & (n - 1) == 0
    return jnp.right_shift(x, n.bit_length() - 1)


def _iota(shape, dim):
    return lax.broadcasted_iota(jnp.int32, shape, dim)


def _resident(shape):
    return pl.BlockSpec(shape, lambda *_: (0,) * len(shape), pipeline_mode=pl.Buffered(1))


def _round_robin(gens, skew=1):
    results = [None] * len(gens)
    live = {}
    waiting = dict(enumerate(gens))
    rnd = 0
    while waiting or live:
        for k in [k for k in waiting if rnd >= k * skew]:
            live[k] = waiting.pop(k)
        for k in sorted(live):
            try:
                next(live[k])
            except StopIteration as done:
                results[k] = done.value
                del live[k]
        rnd += 1
    return results


def _ffn_sweep(make_h, finish, wg_ref, wu_ref, wd_ref, h_ref, acc_ref):
    j = pl.program_id(1)
    last = pl.num_programs(1) - 1
    tm = h_ref.shape[0]
    halves = [slice(0, tm // 2), slice(tm // 2, tm)]

    def hidden(h):
        a = _dot(h, wg_ref[...])
        u = _dot(h, wu_ref[...])
        return (jax.nn.silu(a) * u).astype(BF16)

    @pl.when(j == 0)
    def _():
        def first(rows):
            h = yield from make_h(rows)
            h_ref[rows, :] = h
            yield
            act = hidden(h)
            yield
            acc_ref[rows, :] = _dot(act, wd_ref[...])

        _round_robin([first(rows) for rows in halves])

    @pl.when((j > 0) & (j < last))
    def _():
        acc_ref[...] += _dot(hidden(h_ref[...]), wd_ref[...])

    @pl.when(j == last)
    def _():
        def final(rows):
            act = hidden(h_ref[rows, :])
            yield
            finish(rows, acc_ref[rows, :] + _dot(act, wd_ref[...]))

        _round_robin([final(rows) for rows in halves])


def _ffn_kernel(x_ref, g_ref, wg_ref, wu_ref, wd_ref, *refs, n_cast, final_norm):
    refs = list(refs)
    fin_ref = refs.pop(0) if final_norm else None
    cast_in = refs[:n_cast]
    o_ref = refs[n_cast]
    cast_out = refs[n_cast + 1:2 * n_cast + 1]
    (h_ref,) = refs[2 * n_cast + 1:]
    acc_ref = o_ref
    for src, dst in zip(cast_in, cast_out):
        dst[...] = src[...].astype(BF16)

    def make_h(rows):
        return (_rms(x_ref[rows, :]) * g_ref[...]).astype(BF16)
        yield

    def finish(rows, acc):
        y = x_ref[rows, :] + 0.5 * acc
        o_ref[rows, :] = _rms(y) * fin_ref[...] if final_norm else y

    _ffn_sweep(make_h, finish, wg_ref, wu_ref, wd_ref, h_ref, acc_ref)


def _cast_block(shape, ni, nj):
    rows, cols = shape
    sub, lane = 16, LANES
    if rows % ni == 0 and cols % nj == 0 and (rows // ni) % sub == 0 and (cols // nj) % lane == 0:
        return pl.BlockSpec((rows // ni, cols // nj), lambda i, j: (i, j))
    if rows % (ni * nj) == 0 and (rows // (ni * nj)) % sub == 0:
        return pl.BlockSpec((rows // (ni * nj), cols), lambda i, j: (i * nj + j, 0))
    assert rows % ni == 0 and (rows // ni) % sub == 0
    return pl.BlockSpec((rows // ni, cols), lambda i, j: (i, 0))


def _ffn(name, x, g, wg, wu, wd, later_weights=(), final_gain=None):
    t, d = x.shape
    f = wg.shape[1]
    tm = FFN_TOKEN_TILE
    grid = (t // tm, f // FF_TILE)
    cast_specs = [_cast_block(w.shape, *grid) for w in later_weights]
    gain = lambda: pl.BlockSpec((1, d), lambda i, j: (0, 0))
    final = [] if final_gain is None else [final_gain]
    return pl.pallas_call(
        functools.partial(_ffn_kernel, n_cast=len(later_weights), final_norm=bool(final)),
        name=name,
        grid=grid,
        in_specs=[
            pl.BlockSpec((tm, d), lambda i, j: (i, 0)),
            gain(),
            pl.BlockSpec((d, FF_TILE), lambda i, j: (0, j)),
            pl.BlockSpec((d, FF_TILE), lambda i, j: (0, j)),
            pl.BlockSpec((FF_TILE, d), lambda i, j: (j, 0)),
        ] + [gain() for _ in final] + cast_specs,
        out_specs=[pl.BlockSpec((tm, d), lambda i, j: (i, 0))] + cast_specs,
        out_shape=[jax.ShapeDtypeStruct((t, d), F32)] + [
            jax.ShapeDtypeStruct(w.shape, BF16) for w in later_weights],
        scratch_shapes=[pltpu.VMEM((tm, d), BF16)],
        compiler_params=pltpu.CompilerParams(
            dimension_semantics=("arbitrary", "arbitrary"), vmem_limit_bytes=FFN_VMEM_LIMIT),
    )(x, g, wg, wu, wd, *final, *later_weights)


def _out_proj_kernel(x_ref, yc_ref, ya_ref, ga_ref, wo_ref, o_ref):
    tm = x_ref.shape[0]

    def one_half(rows):
        ya = (_rms(ya_ref[0, :, rows].T) * ga_ref[...]).astype(BF16)
        yield
        mix = _dot(yc_ref[rows, :], wo_ref[0:CONV_WIDTH, :]) + _dot(ya, wo_ref[CONV_WIDTH:, :])
        yield
        o_ref[rows, :] = x_ref[rows, :] + mix

    _round_robin([one_half(slice(0, tm // 2)), one_half(slice(tm // 2, tm))])


def _out_proj(x1, yc, ya_t, ga, wo):
    t, d = x1.shape
    tm = TOKEN_TILE
    tps = ya_t.shape[2] // tm
    return pl.pallas_call(
        _out_proj_kernel,
        name="out_proj",
        grid=(t // tm,),
        in_specs=[
            pl.BlockSpec((tm, d), lambda i: (i, 0)),
            pl.BlockSpec((tm, CONV_WIDTH), lambda i: (i, 0)),
            pl.BlockSpec((1, ATTN_WIDTH, tm), lambda i: (i // tps, 0, i % tps)),
            pl.BlockSpec((1, ATTN_WIDTH), lambda i: (0, 0)),
            _resident(wo.shape),
        ],
        out_specs=pl.BlockSpec((tm, d), lambda i: (i, 0)),
        out_shape=jax.ShapeDtypeStruct((t, d), F32),
        compiler_params=pltpu.CompilerParams(
            dimension_semantics=("parallel",), vmem_limit_bytes=VMEM_LIMIT),
    )(x1, yc, ya_t, ga, wo)


def _conv_proj_kernel(x_ref, g_ref, w_ref, cw_ref, gc_ref, o_ref, carry_ref, *, tiles_per_seq):
    i = pl.program_id(0)

    @pl.when(i % tiles_per_seq == 0)
    def _():
        carry_ref[...] = jnp.zeros_like(carry_ref)

    tm = x_ref.shape[0]
    half = tm // 2
    tails = {-1: carry_ref[...]}

    def one_half(k):
        rows = slice(k * half, (k + 1) * half)
        h = (_rms(x_ref[rows, :]) * g_ref[...]).astype(BF16)
        yield
        c_h = _dot(h, w_ref[:, 0:CONV_WIDTH])
        c_c = _dot(h, w_ref[:, 2 * CONV_WIDTH:3 * CONV_WIDTH])
        u = c_c * c_h
        tails[k] = u[half - 8:half, :]
        c_b = _dot(h, w_ref[:, CONV_WIDTH:2 * CONV_WIDTH])
        yield
        row = lax.broadcasted_iota(jnp.int32, u.shape, 0)
        prev1 = tails[k - 1][7:8, :]
        prev2 = tails[k - 1][6:7, :]
        u1 = jnp.where(row == 0, prev1, pltpu.roll(u, 1, 0))
        u2 = jnp.where(row == 0, prev2, jnp.where(row == 1, prev1, pltpu.roll(u, 2, 0)))
        y = c_b * (cw_ref[0:1, :] * u2 + cw_ref[1:2, :] * u1 + cw_ref[2:3, :] * u)
        o_ref[rows, :] = (_rms(y) * gc_ref[...]).astype(BF16)

    _round_robin([one_half(0), one_half(1)])
    carry_ref[...] = tails[1]


def _conv_proj(x1, g, w_in, conv_w, gc, seq):
    t, d = x1.shape
    tm = TOKEN_TILE
    return pl.pallas_call(
        functools.partial(_conv_proj_kernel, tiles_per_seq=seq // tm),
        name="conv_proj",
        grid=(t // tm,),
        in_specs=[
            pl.BlockSpec((tm, d), lambda i: (i, 0)),
            pl.BlockSpec((1, d), lambda i: (0, 0)),
            _w_in_window(_CONV0, 3 * CONV_WIDTH),
            pl.BlockSpec(conv_w.shape, lambda i: (0, 0)),
            pl.BlockSpec((1, CONV_WIDTH), lambda i: (0, 0)),
        ],
        out_specs=pl.BlockSpec((tm, CONV_WIDTH), lambda i: (i, 0)),
        out_shape=jax.ShapeDtypeStruct((t, CONV_WIDTH), BF16),
        scratch_shapes=[pltpu.VMEM((8, CONV_WIDTH), F32)],
        compiler_params=pltpu.CompilerParams(
            dimension_semantics=("arbitrary",), vmem_limit_bytes=VMEM_LIMIT),
    )(x1, g, w_in, conv_w, gc)


def _rope(x, cos, sin_signed, first_half):
    outs = []
    for k in range(x.shape[1] // LANES):
        xc = x[:, k * LANES:(k + 1) * LANES]
        partner = jnp.where(first_half, pltpu.roll(xc, LANES - HEAD_DIM // 2, 1),
                            pltpu.roll(xc, HEAD_DIM // 2, 1))
        outs.append(xc * cos + partner * sin_signed)
    return outs


_CONV0 = 0
_Q0 = 3 * CONV_WIDTH
_KC0 = _Q0 + ATTN_WIDTH
_VC0 = _KC0 + KV_WIDTH
_KS0 = _VC0 + KV_WIDTH
_VS0 = _KS0 + KV_WIDTH
_KW0 = _VS0 + KV_WIDTH
_VW0 = _KW0 + KV_WIDTH
_G0 = _VW0 + KV_WIDTH
IN_WIDTH = _G0 + N_HEADS * N_GATES
IN_WIDTH_PADDED = -(-IN_WIDTH // LANES) * LANES


def _w_in_window(c0, width):
    assert c0 % width == 0
    return pl.BlockSpec((D_MODEL, width), lambda *_: (0, c0 // width), pipeline_mode=pl.Buffered(1))


def _attn_proj_kernel(x_ref, g_ref, wq_ref, wkc_ref, wvc_ref, wks_ref, wkw_ref, wgate_ref, wvt_ref,
                      cos_ref, sin_ref,
                      q_ref, kc_ref, vc_ref, ks_ref, kw_ref, vs_ref, vw_ref, gate_ref):
    tm = x_ref.shape[0]
    half = tm // 2
    lane = lax.broadcasted_iota(jnp.int32, (half, LANES), 1)
    first_half = (lane & (HEAD_DIM - 1)) < HEAD_DIM // 2

    def one_half(k):
        rows = slice(k * half, (k + 1) * half)
        h = (_rms(x_ref[rows, :]) * g_ref[...]).astype(BF16)
        cos = cos_ref[rows, :]
        sin = sin_ref[rows, :]

        def proj(w_ref):
            return _dot(h, w_ref[...])

        def rope_to(ref, w_ref, dtype):
            for n, piece in enumerate(_rope(proj(w_ref), cos, sin, first_half)):
                ref[rows, n * LANES:(n + 1) * LANES] = piece.astype(dtype)

        yield
        for n, piece in enumerate(_rope(proj(wq_ref), cos, sin, first_half)):
            q_ref[0, n * LANES:(n + 1) * LANES, rows] = (piece * Q_SCALE).T.astype(BF16)
        yield
        rope_to(kc_ref, wkc_ref, F32)
        vc_ref[rows, :] = proj(wvc_ref)
        rope_to(ks_ref, wks_ref, BF16)
        rope_to(kw_ref, wkw_ref, BF16)
        yield
        gate_ref[0, :, rows] = jax.nn.sigmoid(proj(wgate_ref)).T
        vt = _dot_nt(wvt_ref[...], h)
        chunks = half // Q_BLOCK
        for n in range(chunks):
            cs = slice(n * Q_BLOCK, (n + 1) * Q_BLOCK)
            vs_ref[0, k * chunks + n] = vt[0:KV_WIDTH, cs].astype(BF16)
            vw_ref[0, k * chunks + n] = vt[KV_WIDTH:2 * KV_WIDTH, cs].astype(BF16)

    _round_robin([one_half(0), one_half(1)])


def _attn_proj(x1, g, w_in, w_vt, cos, sin, seq):
    t, d = x1.shape
    tm = TOKEN_TILE
    tps = seq // tm
    bsz = t // seq
    cpt = tm // Q_BLOCK
    row = lambda w: pl.BlockSpec((tm, w), lambda i: (i, 0))
    col = lambda w: pl.BlockSpec((1, w, tm), lambda i: (i // tps, 0, i % tps))
    chunked = pl.BlockSpec((1, cpt, KV_WIDTH, Q_BLOCK), lambda i: (i // tps, i % tps, 0, 0))
    kv_dtypes = [F32, F32, BF16, BF16]
    out_specs = [col(ATTN_WIDTH)] + [row(KV_WIDTH)] * 4 + [chunked, chunked, col(GATE_ROWS)]
    out_shape = ([jax.ShapeDtypeStruct((bsz, ATTN_WIDTH, seq), BF16)]
                 + [jax.ShapeDtypeStruct((t, KV_WIDTH), dt) for dt in kv_dtypes]
                 + [jax.ShapeDtypeStruct((bsz, seq // Q_BLOCK, KV_WIDTH, Q_BLOCK), BF16)] * 2
                 + [jax.ShapeDtypeStruct((bsz, GATE_ROWS, seq), F32)])
    windows = [_w_in_window(_Q0, ATTN_WIDTH)] + [
        _w_in_window(c0, KV_WIDTH) for c0 in (_KC0, _VC0, _KS0, _KW0)] + [
        _w_in_window(_G0, GATE_ROWS)]
    return pl.pallas_call(
        _attn_proj_kernel,
        name="attn_proj",
        grid=(t // tm,),
        in_specs=[row(d), pl.BlockSpec((1, d), lambda i: (0, 0))] + windows + [
            _resident(w_vt.shape),
            pl.BlockSpec((tm, LANES), lambda i: (i % tps, 0)),
            pl.BlockSpec((tm, LANES), lambda i: (i % tps, 0)),
        ],
        out_specs=out_specs,
        out_shape=out_shape,
        compiler_params=pltpu.CompilerParams(
            dimension_semantics=("parallel",), vmem_limit_bytes=VMEM_LIMIT),
    )(x1, g, *([w_in] * len(windows)), w_vt, cos, sin)


HEADS_PER_TILE = LANES // HEAD_DIM


def _compress_kernel(*refs, transposed):
    tok_refs = refs[:N_KV // HEADS_PER_TILE]
    pe_ref, w1_ref, w2_ref, o_ref = refs[N_KV // HEADS_PER_TILE:]
    n = tok_refs[0].shape[0] // CMP_STRIDE
    for t, tok_ref in enumerate(tok_refs):
        first = jnp.zeros((n, HEADS_PER_TILE * CMP_HIDDEN), F32)
        second = jnp.zeros((n, HEADS_PER_TILE * CMP_HIDDEN), F32)
        for l in range(CMP_STRIDE):
            rows = tok_ref[pl.ds(l, n, stride=CMP_STRIDE), :]
            first = first + _dot((rows + pe_ref[l]).astype(BF16), w1_ref[l])
            second = second + _dot((rows + pe_ref[l + CMP_STRIDE]).astype(BF16),
                                   w1_ref[l + CMP_STRIDE])
        hid = jax.nn.gelu(first + pltpu.roll(second, n - 1, 0)).astype(BF16)
        for k in range(HEADS_PER_TILE):
            hid_h = hid[:, k * CMP_HIDDEN:(k + 1) * CMP_HIDDEN]
            h = t * HEADS_PER_TILE + k
            if transposed:
                o_ref[0, h] = _dot_nt(w2_ref[...], hid_h).astype(BF16)
            else:
                o_ref[0, h] = _dot(hid_h, w2_ref[...]).astype(BF16)


def _compress(tok, pe, w1, w2, seq, transposed):
    bsz = tok.shape[0] // seq
    n = seq // CMP_STRIDE
    out_block = (1, N_KV, HEAD_DIM, n) if transposed else (1, N_KV, n, LANES)
    lane_tile = lambda t: pl.BlockSpec((seq, LANES), lambda b: (b, t))
    return pl.pallas_call(
        functools.partial(_compress_kernel, transposed=transposed),
        name="compress_v" if transposed else "compress_k",
        grid=(bsz,),
        in_specs=[lane_tile(t) for t in range(N_KV // HEADS_PER_TILE)] + [
            _resident(pe.shape),
            _resident(w1.shape),
            _resident(w2.shape),
        ],
        out_specs=pl.BlockSpec(out_block, lambda b: (b, 0, 0, 0)),
        out_shape=jax.ShapeDtypeStruct((bsz,) + out_block[1:], BF16),
        compiler_params=pltpu.CompilerParams(
            dimension_semantics=("parallel",), vmem_limit_bytes=VMEM_LIMIT),
    )(*([tok] * (N_KV // HEADS_PER_TILE)), pe, w1, w2)


def _tile4(x):
    return jnp.concatenate([x] * GROUP, axis=1)


def _col_max(x):
    return jnp.max(x, axis=0, keepdims=True)


def _col_sum(x):
    return jnp.sum(x, axis=0, keepdims=True)


def _with_ones(v):
    ones = jnp.where(_iota((ONES_ROWS, v.shape[1]), 0) == 0, 1.0, 0.0).astype(v.dtype)
    return jnp.concatenate([v, ones], axis=0)


def _nsa_kernel(q_ref, gate_ref, kc_ref, vc_ref, ks_ref, kw_ref, vs_ref, vw_ref, o_ref,
                kse_ref, kwe_ref, imp_ref, *, seq):
    hp = pl.program_id(1)
    c = pl.program_id(2)
    t0 = c * Q_BLOCK
    n_cmp = kc_ref.shape[2]
    n_slc = seq // SLC_BLOCK
    cols = GROUP * Q_BLOCK
    n_chunks = seq // Q_BLOCK
    chunks_per_tile = KEY_TILE // Q_BLOCK
    heads = range(HEADS_PER_STEP)
    assert n_slc + HEAD_DIM == LANES and n_slc % 8 == 0

    @pl.when(c == 0)
    def _():
        r = _iota((KV_WIDTH, 2 * LANES), 0)
        ln = _iota((KV_WIDTH, 2 * LANES), 1)
        for hh in heads:
            h = hp * HEADS_PER_STEP + hh
            pick = ((r == h * HEAD_DIM + ln) & (ln < HEAD_DIM)).astype(BF16)

            def build_k(i, _, hh=hh, pick=pick):
                sl = pl.ds(pl.multiple_of(i * KEY_TILE, KEY_TILE), KEY_TILE)
                pos = i * KEY_TILE + _iota((KEY_TILE, 2 * LANES), 0)
                lane = _iota((KEY_TILE, 2 * LANES), 1)
                in_chunk = lane - LANES == (pos & (Q_BLOCK - 1))
                in_block = lane - HEAD_DIM == _div_pow2(pos, SLC_BLOCK)
                kse_ref[hh, sl, :] = (_dot(ks_ref[0, sl, :], pick)
                                      + jnp.where(in_chunk | in_block, 1.0, 0.0)).astype(BF16)
                kwe_ref[hh, sl, :] = (_dot(kw_ref[0, sl, :], pick)
                                      + jnp.where(in_chunk, 1.0, 0.0)).astype(BF16)
                return 0

            lax.fori_loop(0, seq // KEY_TILE, build_k, 0)
            kwe_ref[hh, pl.ds(seq, Q_BLOCK), :] = jnp.where(
                _iota((Q_BLOCK, 2 * LANES), 1) == HEAD_DIM, 1.0, 0.0).astype(BF16)

    def v_chunk(ref, hh, idx):
        return _with_ones(ref[0, idx, hh * HEAD_DIM:(hh + 1) * HEAD_DIM, :])

    def v_tile(hh, kt):
        return _with_ones(jnp.concatenate(
            [vs_ref[0, kt * chunks_per_tile + k, hh * HEAD_DIM:(hh + 1) * HEAD_DIM, :]
             for k in range(chunks_per_tile)], axis=1))

    def soft(scores, m):
        return [jnp.exp2(s - m).astype(BF16) for s in scores]

    kpos = _iota((Q_BLOCK, Q_BLOCK), 0)
    tpos = _iota((Q_BLOCK, Q_BLOCK), 1)
    causal = _tile4(jnp.where(kpos <= tpos, 0.0, NEG).astype(BF16))
    recent = _tile4(jnp.where(kpos > tpos, 0.0, NEG).astype(BF16))
    pad = jnp.zeros((HEAD_DIM, cols), BF16)
    dummy_row = jnp.where(_iota((HEAD_DIM, cols), 0) == 0, NEG, 0.0).astype(BF16)
    jj = _iota((n_slc, n_cmp), 0)
    ii = _iota((n_slc, n_cmp), 1)
    ov = (jnp.minimum(ii * CMP_STRIDE + CMP_BLOCK, jj * SLC_BLOCK + SLC_BLOCK)
          - jnp.maximum(ii * CMP_STRIDE, jj * SLC_BLOCK))
    ov = (jnp.maximum(ov, 0).astype(F32) * (1.0 / CMP_BLOCK)).astype(BF16)
    c_ok = (_iota((n_cmp, Q_BLOCK), 0) * CMP_STRIDE + (CMP_BLOCK - 1)
            <= t0 + _iota((n_cmp, Q_BLOCK), 1))
    c_bias = _tile4(jnp.where(c_ok, 0.0, NEG))
    any_ok = t0 + (_iota((1, cols), 1) & (Q_BLOCK - 1)) >= CMP_BLOCK - 1
    jb = _iota((n_slc, Q_BLOCK), 0)
    tb = _div_pow2(t0 + _iota((n_slc, Q_BLOCK), 1), SLC_BLOCK)
    forced = (jb == 0) | ((jb <= tb) & (jb > tb - N_LOCAL))
    sub = _iota((8, Q_BLOCK), 0)
    n_win = WIN_KEYS // Q_BLOCK
    first = c - (n_win - 1)
    k_chunk = [jnp.where(first + i >= 0, first + i, n_chunks) for i in range(n_win)]
    v_idx = [jnp.maximum(first + i, 0) for i in range(n_win)]
    k_rows = [pl.ds(pl.multiple_of(ch * Q_BLOCK, Q_BLOCK), Q_BLOCK) for ch in k_chunk]

    def before_sweep(hh):
        qt = q_ref[0, hh * GROUP_WIDTH:(hh + 1) * GROUP_WIDTH, :]
        wq = jnp.concatenate([qt[g * HEAD_DIM:(g + 1) * HEAD_DIM, :] for g in range(GROUP)], axis=1)

        sc = _dot(kc_ref[0, hh], jnp.concatenate([wq, pad], axis=0)) + c_bias
        yield
        pc = jnp.exp2(sc - _col_max(sc))
        pc = pc * jnp.where(any_ok, 1.0 / _col_sum(pc), 0.0)
        yield
        o_cmp = _dot(vc_ref[0, hh], pc.astype(BF16))

        pcs = pc[:, 0:Q_BLOCK]
        for g in range(1, GROUP):
            pcs = pcs + pc[:, g * Q_BLOCK:(g + 1) * Q_BLOCK]
        pcs_hi = pcs.astype(BF16)
        pcs_lo = (pcs - pcs_hi.astype(F32)).astype(BF16)
        imp = _dot(ov, pcs_hi) + _dot(ov, pcs_lo)
        yield
        imp = jnp.where(forced, FORCE, imp)
        imp = jnp.where(jb <= tb, imp, NEG)
        imp_ref[hh] = imp

        w_causal = jnp.concatenate([wq, pad, causal], axis=0)
        w_mid = jnp.concatenate([wq, dummy_row], axis=0)
        w_recent = jnp.concatenate([wq, dummy_row, recent], axis=0)
        sw = [_dot(kwe_ref[hh, k_rows[0], :], w_recent)]
        sw += [_dot(kwe_ref[hh, k_rows[i], 0:LANES], w_mid) for i in range(1, n_win - 1)]
        sw += [_dot(kwe_ref[hh, k_rows[n_win - 1], :], w_causal)]
        s_d = _dot(kse_ref[hh, pl.ds(pl.multiple_of(t0, Q_BLOCK), Q_BLOCK), :], w_causal)
        yield

        n_groups = n_slc // 8
        mine = [imp[8 * r:8 * r + 8, :] for r in range(n_groups)]
        ahead_count = [jnp.zeros((8, Q_BLOCK), F32)] * n_groups
        for j in range(n_slc):
            other = imp_ref[hh, j:j + 1, :]
            for r in range(n_groups):
                if 8 * r > j:
                    ahead = other >= mine[r]
                elif 8 * r + 7 <= j:
                    ahead = other > mine[r]
                else:
                    ahead = (other > mine[r]) | ((other == mine[r]) & (sub > j - 8 * r))
                ahead_count[r] = ahead_count[r] + jnp.where(ahead, 1.0, 0.0)
            if j % 8 == 7:
                yield
        sel = (jnp.concatenate(ahead_count, axis=0) < SLC_TOPK) & (imp > 0.5 * NEG)
        sel_bias = jnp.where(sel & (jb * SLC_BLOCK < t0), 0.0, NEG).astype(BF16)
        w_sel = jnp.concatenate([wq, _tile4(sel_bias)], axis=0)

        m_w = _col_max(sw[0])
        for s in sw[1:]:
            m_w = jnp.maximum(m_w, _col_max(s))
        yield
        p_w = soft(sw, m_w)
        yield
        o_win = None
        for idx, p in zip(v_idx, p_w):
            part = _dot(v_chunk(vw_ref, hh, idx), p)
            o_win = part if o_win is None else o_win + part
        m_d = _col_max(s_d)
        acc_d = _dot(v_chunk(vs_ref, hh, c), soft([s_d], m_d)[0])
        return w_sel, o_cmp, o_win, (m_d, acc_d)

    pre = _round_robin([before_sweep(hh) for hh in heads])

    def scores(hh, kt):
        k0 = pl.multiple_of(kt * KEY_TILE, KEY_TILE)
        return _dot(kse_ref[hh, pl.ds(k0, KEY_TILE), 0:LANES], pre[hh][0])

    def sweep(items, states, lookahead=2):
        states = list(states)
        s, m_new, p = {}, {}, {}

        def issue(n):
            s[n] = scores(*items[n])

        def value_matmul(n):
            hh, kt = items[n]
            m, acc = states[hh]
            states[hh] = m_new[n], jnp.exp2(m - m_new[n]) * acc + _dot(v_tile(hh, kt), p[n])

        for n in range(min(lookahead, len(items))):
            issue(n)
        for n, (hh, _) in enumerate(items):
            m_new[n] = jnp.maximum(states[hh][0], _col_max(s[n]))
            if n + lookahead < len(items):
                issue(n + lookahead)
            p[n] = jnp.exp2(s[n] - m_new[n]).astype(BF16)
            if n >= 1:
                value_matmul(n - 1)
        value_matmul(len(items) - 1)
        return tuple(states)

    def tile_pair(i, states):
        return sweep([(hh, 2 * i + k) for k in range(2) for hh in heads], states)

    def last_tile(states):
        return sweep([(hh, n_tiles - 1) for hh in heads], states)

    n_tiles = _div_pow2(t0 + KEY_TILE - 1, KEY_TILE)
    swept = lax.fori_loop(0, _div_pow2(n_tiles, 2), tile_pair, tuple(p[3] for p in pre))
    swept = lax.cond((n_tiles & 1) == 1, last_tile, lambda states: states, swept)

    for hh in heads:
        _, o_cmp, o_win, _ = pre[hh]
        o_slc = swept[hh][1]
        gates = gate_ref[0, hh * GROUP * N_GATES:(hh + 1) * GROUP * N_GATES, :]
        inv_ls = 1.0 / o_slc[HEAD_DIM:HEAD_DIM + 1, :]
        inv_lw = 1.0 / o_win[HEAD_DIM:HEAD_DIM + 1, :]
        pieces = []
        for g in range(GROUP):
            cs = slice(g * Q_BLOCK, (g + 1) * Q_BLOCK)
            row = g * N_GATES
            pieces.append(gates[row:row + 1, :] * o_cmp[:, cs]
                          + gates[row + 1:row + 2, :] * (o_slc[0:HEAD_DIM, cs] * inv_ls[:, cs])
                          + gates[row + 2:row + 3, :] * (o_win[0:HEAD_DIM, cs] * inv_lw[:, cs]))
        o_ref[0, hh * GROUP_WIDTH:(hh + 1) * GROUP_WIDTH, :] = jnp.concatenate(pieces, axis=0)


def _nsa(q_t, gates_t, kc, vc_t, ks, kw, vs_t, vw_t):
    bsz, _, seq = q_t.shape
    n_cmp = kc.shape[2]
    hps = HEADS_PER_STEP
    assert hps == N_KV
    k_spec = pl.BlockSpec((1, seq, KV_WIDTH), lambda b, h, c: (b, 0, 0))
    v_spec = pl.BlockSpec((1, seq // Q_BLOCK, hps * HEAD_DIM, Q_BLOCK), lambda b, h, c: (b, 0, h, 0))
    return pl.pallas_call(
        functools.partial(_nsa_kernel, seq=seq),
        name="nsa",
        grid=(bsz, N_KV // hps, seq // Q_BLOCK),
        in_specs=[
            pl.BlockSpec((1, hps * GROUP_WIDTH, Q_BLOCK), lambda b, h, c: (b, h, c)),
            pl.BlockSpec((1, GATE_ROWS, Q_BLOCK), lambda b, h, c: (b, 0, c)),
            pl.BlockSpec((1, hps, n_cmp, LANES), lambda b, h, c: (b, h, 0, 0)),
            pl.BlockSpec((1, hps, HEAD_DIM, n_cmp), lambda b, h, c: (b, h, 0, 0)),
            k_spec, k_spec, v_spec, v_spec,
        ],
        out_specs=pl.BlockSpec((1, hps * GROUP_WIDTH, Q_BLOCK), lambda b, h, c: (b, h, c)),
        out_shape=jax.ShapeDtypeStruct((bsz, ATTN_WIDTH, seq), F32),
        scratch_shapes=[
            pltpu.VMEM((hps, seq, 2 * LANES), BF16),
            pltpu.VMEM((hps, seq + Q_BLOCK, 2 * LANES), BF16),
            pltpu.VMEM((hps, seq // SLC_BLOCK, Q_BLOCK), F32),
        ],
        compiler_params=pltpu.CompilerParams(
            dimension_semantics=("parallel", "parallel", "arbitrary"),
            vmem_limit_bytes=VMEM_LIMIT),
    )(q_t, gates_t, kc, vc_t, ks, kw, vs_t, vw_t)


def _rope_tables(seq):
    pos = jnp.arange(seq, dtype=F32)
    inv = ROPE_THETA ** (-jnp.arange(0, HEAD_DIM, 2, dtype=F32) / HEAD_DIM)
    ang = pos[:, None] * inv[None, :]
    cos, sin = jnp.cos(ang), jnp.sin(ang)
    reps = LANES // HEAD_DIM
    return (jnp.tile(jnp.concatenate([cos, cos], axis=1), (1, reps)),
            jnp.tile(jnp.concatenate([-sin, sin], axis=1), (1, reps)))


def _compress_params(w1, w2, pe):
    eye = jnp.eye(HEADS_PER_TILE, dtype=w1.dtype)
    w1 = jnp.einsum('ldf,hk->lhdkf', w1, eye).reshape(CMP_BLOCK, LANES, HEADS_PER_TILE * CMP_HIDDEN)
    pe = jnp.tile(pe, (1, HEADS_PER_TILE)).reshape(CMP_BLOCK, 1, LANES)
    return pe, w1.astype(BF16), w2.astype(BF16)


def kernel(x, ffn1_norm, ffn1_wg, ffn1_wu, ffn1_wd, mix_norm, w_in, conv_w, cmp_k_w1, cmp_k_w2,
           cmp_k_pe, cmp_v_w1, cmp_v_w2, cmp_v_pe, conv_out_norm, attn_out_norm, w_out, ffn2_norm,
           ffn2_wg, ffn2_wu, ffn2_wd, final_norm):
    bsz, seq, d = x.shape
    assert CMP_BLOCK == 2 * CMP_STRIDE and seq % TOKEN_TILE == 0 and seq % KEY_TILE == 0
    assert x.shape[0] * seq % TOKEN_TILE == 0 and ffn1_wg.shape[0] == 1
    xt = x.reshape(bsz * seq, d)
    cos, sin = _rope_tables(seq)

    assert w_in.shape[1:] == (D_MODEL, IN_WIDTH)
    w = jnp.pad(w_in[0].astype(BF16), ((0, 0), (0, IN_WIDTH_PADDED - IN_WIDTH)))
    w_vt = jnp.concatenate([w[:, _VS0:_VS0 + KV_WIDTH], w[:, _VW0:_VW0 + KV_WIDTH]], axis=1).T

    x1, wg2, wu2, wd2, wo = _ffn("ffn1", xt, ffn1_norm, ffn1_wg[0].astype(BF16),
                                 ffn1_wu[0].astype(BF16), ffn1_wd[0].astype(BF16),
                                 later_weights=[ffn2_wg[0], ffn2_wu[0], ffn2_wd[0], w_out[0]])
    yc = _conv_proj(x1, mix_norm, w, conv_w[0], conv_out_norm, seq)
    q_t, k_c, v_c, k_s, k_w, vs_t, vw_t, gates_t = _attn_proj(x1, mix_norm, w, w_vt, cos, sin, seq)

    pe_k, w1_k, w2_k = _compress_params(cmp_k_w1[0], cmp_k_w2[0], cmp_k_pe[0])
    pe_v, w1_v, w2_v = _compress_params(cmp_v_w1[0], cmp_v_w2[0], cmp_v_pe[0])
    kc = _compress(k_c, pe_k, w1_k, jnp.pad(w2_k, ((0, 0), (0, LANES - HEAD_DIM))), seq,
                   transposed=False)
    vc_t = _compress(v_c, pe_v, w1_v, w2_v.T, seq, transposed=True)

    per_batch = lambda a: a.reshape(bsz, seq, a.shape[-1])
    ya_t = _nsa(q_t, gates_t, kc, vc_t, per_batch(k_s), per_batch(k_w), vs_t, vw_t)

    x2 = _out_proj(x1, yc, ya_t, attn_out_norm, wo)
    (out,) = _ffn("ffn2", x2, ffn2_norm, wg2, wu2, wd2, final_gain=final_norm.reshape(1, d))
    return out.reshape(bsz, seq, d)
```

```python
import functools

import jax
import jax.numpy as jnp
from jax import lax
from jax.experimental import pallas as pl
from jax.experimental.pallas import tpu as pltpu

F32 = jnp.float32
BF16 = jnp.bfloat16

D_MODEL = 2048
D_FF = 5632
CONV_WIDTH = 1024
N_HEADS = 16
N_KV = 4
GROUP = N_HEADS // N_KV
HEAD_DIM = 64
ATTN_WIDTH = N_HEADS * HEAD_DIM
KV_WIDTH = N_KV * HEAD_DIM
CMP_BLOCK = 32
CMP_STRIDE = 16
CMP_HIDDEN = 256
SLC_BLOCK = 64
SLC_TOPK = 16
N_LOCAL = 2
WINDOW = 512
Q_BLOCK = 128
N_GATES = 3
ROPE_THETA = 10000.0
EPS = 1e-6
NEG = -1e30
FORCE = 1e9

LANES = 128
GROUP_WIDTH = GROUP * HEAD_DIM
GATE_ROWS = LANES
TOKEN_TILE = 512
FFN_TOKEN_TILE = 1024
FF_TILE = 512
KEY_TILE = 256
WIN_KEYS = WINDOW + Q_BLOCK
ONES_ROWS = 16
HEADS_PER_STEP = 4
VMEM_LIMIT = 56 * 1024 * 1024
FFN_VMEM_LIMIT = 62 * 1024 * 1024
Q_SCALE = HEAD_DIM ** -0.5 * 1.4426950408889634


def _rms(x):
    return x * lax.rsqrt(jnp.mean(x * x, axis=-1, keepdims=True) + EPS)


def _dot(a, b):
    return jnp.dot(a, b, preferred_element_type=F32)


def _dot_nt(a, b):
    return lax.dot_general(a, b, (((1,), (1,)), ((), ())), preferred_element_type=F32)


def _div_pow2(x, n):
    assert n & (n - 1) == 0
    return jnp.right_shift(x, n.bit_length() - 1)


def _iota(shape, dim):
    return lax.broadcasted_iota(jnp.int32, shape, dim)


def _resident(shape):
    return pl.BlockSpec(shape, lambda *_: (0,) * len(shape), pipeline_mode=pl.Buffered(1))


def _round_robin(gens, skew=1):
    results = [None] * len(gens)
    live = {}
    waiting = dict(enumerate(gens))
    rnd = 0
    while waiting or live:
        for k in [k for k in waiting if rnd >= k * skew]:
            live[k] = waiting.pop(k)
        for k in sorted(live):
            try:
                next(live[k])
            except StopIteration as done:
                results[k] = done.value
                del live[k]
        rnd += 1
    return results


def _ffn_sweep(make_h, finish, wg_ref, wu_ref, wd_ref, h_ref, acc_ref):
    j = pl.program_id(1)
    last = pl.num_programs(1) - 1
    tm = h_ref.shape[0]
    halves = [slice(0, tm // 2), slice(tm // 2, tm)]

    def hidden(h):
        a = _dot(h, wg_ref[...])
        u = _dot(h, wu_ref[...])
        return (jax.nn.silu(a) * u).astype(BF16)

    @pl.when(j == 0)
    def _():
        def first(rows):
            h = yield from make_h(rows)
            h_ref[rows, :] = h
            yield
            act = hidden(h)
            yield
            acc_ref[rows, :] = _dot(act, wd_ref[...])

        _round_robin([first(rows) for rows in halves])

    @pl.when((j > 0) & (j < last))
    def _():
        acc_ref[...] += _dot(hidden(h_ref[...]), wd_ref[...])

    @pl.when(j == last)
    def _():
        def final(rows):
            act = hidden(h_ref[rows, :])
            yield
            finish(rows, acc_ref[rows, :] + _dot(act, wd_ref[...]))

        _round_robin([final(rows) for rows in halves])


def _ffn_kernel(x_ref, g_ref, wg_ref, wu_ref, wd_ref, *refs, n_cast, final_norm):
    refs = list(refs)
    fin_ref = refs.pop(0) if final_norm else None
    cast_in = refs[:n_cast]
    o_ref = refs[n_cast]
    cast_out = refs[n_cast + 1:2 * n_cast + 1]
    (h_ref,) = refs[2 * n_cast + 1:]
    acc_ref = o_ref
    for src, dst in zip(cast_in, cast_out):
        dst[...] = src[...].astype(BF16)

    def make_h(rows):
        return (_rms(x_ref[rows, :]) * g_ref[...]).astype(BF16)
        yield

    def finish(rows, acc):
        y = x_ref[rows, :] + 0.5 * acc
        o_ref[rows, :] = _rms(y) * fin_ref[...] if final_norm else y

    _ffn_sweep(make_h, finish, wg_ref, wu_ref, wd_ref, h_ref, acc_ref)


def _cast_block(shape, ni, nj):
    rows, cols = shape
    sub, lane = 16, LANES
    if rows % ni == 0 and cols % nj == 0 and (rows // ni) % sub == 0 and (cols // nj) % lane == 0:
        return pl.BlockSpec((rows // ni, cols // nj), lambda i, j: (i, j))
    if rows % (ni * nj) == 0 and (rows // (ni * nj)) % sub == 0:
        return pl.BlockSpec((rows // (ni * nj), cols), lambda i, j: (i * nj + j, 0))
    assert rows % ni == 0 and (rows // ni) % sub == 0
    return pl.BlockSpec((rows // ni, cols), lambda i, j: (i, 0))


def _ffn(name, x, g, wg, wu, wd, later_weights=(), final_gain=None):
    t, d = x.shape
    f = wg.shape[1]
    tm = FFN_TOKEN_TILE
    grid = (t // tm, f // FF_TILE)
    cast_specs = [_cast_block(w.shape, *grid) for w in later_weights]
    gain = lambda: pl.BlockSpec((1, d), lambda i, j: (0, 0))
    final = [] if final_gain is None else [final_gain]
    return pl.pallas_call(
        functools.partial(_ffn_kernel, n_cast=len(later_weights), final_norm=bool(final)),
        name=name,
        grid=grid,
        in_specs=[
            pl.BlockSpec((tm, d), lambda i, j: (i, 0)),
            gain(),
            pl.BlockSpec((d, FF_TILE), lambda i, j: (0, j)),
            pl.BlockSpec((d, FF_TILE), lambda i, j: (0, j)),
            pl.BlockSpec((FF_TILE, d), lambda i, j: (j, 0)),
        ] + [gain() for _ in final] + cast_specs,
        out_specs=[pl.BlockSpec((tm, d), lambda i, j: (i, 0))] + cast_specs,
        out_shape=[jax.ShapeDtypeStruct((t, d), F32)] + [
            jax.ShapeDtypeStruct(w.shape, BF16) for w in later_weights],
        scratch_shapes=[pltpu.VMEM((tm, d), BF16)],
        compiler_params=pltpu.CompilerParams(
            dimension_semantics=("arbitrary", "arbitrary"), vmem_limit_bytes=FFN_VMEM_LIMIT),
    )(x, g, wg, wu, wd, *final, *later_weights)


def _out_proj_kernel(x_ref, yc_ref, ya_ref, ga_ref, wo_ref, o_ref):
    tm = x_ref.shape[0]

    def one_half(rows):
        ya = (_rms(ya_ref[0, :, rows].T) * ga_ref[...]).astype(BF16)
        yield
        mix = _dot(yc_ref[rows, :], wo_ref[0:CONV_WIDTH, :]) + _dot(ya, wo_ref[CONV_WIDTH:, :])
        yield
        o_ref[rows, :] = x_ref[rows, :] + mix

    _round_robin([one_half(slice(0, tm // 2)), one_half(slice(tm // 2, tm))])


def _out_proj(x1, yc, ya_t, ga, wo):
    t, d = x1.shape
    tm = TOKEN_TILE
    tps = ya_t.shape[2] // tm
    return pl.pallas_call(
        _out_proj_kernel,
        name="out_proj",
        grid=(t // tm,),
        in_specs=[
            pl.BlockSpec((tm, d), lambda i: (i, 0)),
            pl.BlockSpec((tm, CONV_WIDTH), lambda i: (i, 0)),
            pl.BlockSpec((1, ATTN_WIDTH, tm), lambda i: (i // tps, 0, i % tps)),
            pl.BlockSpec((1, ATTN_WIDTH), lambda i: (0, 0)),
            _resident(wo.shape),
        ],
        out_specs=pl.BlockSpec((tm, d), lambda i: (i, 0)),
        out_shape=jax.ShapeDtypeStruct((t, d), F32),
        compiler_params=pltpu.CompilerParams(
            dimension_semantics=("parallel",), vmem_limit_bytes=VMEM_LIMIT),
    )(x1, yc, ya_t, ga, wo)


def _conv_proj_kernel(x_ref, g_ref, w_ref, cw_ref, gc_ref, o_ref, carry_ref, *, tiles_per_seq):
    i = pl.program_id(0)

    @pl.when(i % tiles_per_seq == 0)
    def _():
        carry_ref[...] = jnp.zeros_like(carry_ref)

    tm = x_ref.shape[0]
    half = tm // 2
    tails = {-1: carry_ref[...]}

    def one_half(k):
        rows = slice(k * half, (k + 1) * half)
        h = (_rms(x_ref[rows, :]) * g_ref[...]).astype(BF16)
        yield
        c_h = _dot(h, w_ref[:, 0:CONV_WIDTH])
        c_c = _dot(h, w_ref[:, 2 * CONV_WIDTH:3 * CONV_WIDTH])
        u = c_c * c_h
        tails[k] = u[half - 8:half, :]
        c_b = _dot(h, w_ref[:, CONV_WIDTH:2 * CONV_WIDTH])
        yield
        row = lax.broadcasted_iota(jnp.int32, u.shape, 0)
        prev1 = tails[k - 1][7:8, :]
        prev2 = tails[k - 1][6:7, :]
        u1 = jnp.where(row == 0, prev1, pltpu.roll(u, 1, 0))
        u2 = jnp.where(row == 0, prev2, jnp.where(row == 1, prev1, pltpu.roll(u, 2, 0)))
        y = c_b * (cw_ref[0:1, :] * u2 + cw_ref[1:2, :] * u1 + cw_ref[2:3, :] * u)
        o_ref[rows, :] = (_rms(y) * gc_ref[...]).astype(BF16)

    _round_robin([one_half(0), one_half(1)])
    carry_ref[...] = tails[1]


def _conv_proj(x1, g, w_in, conv_w, gc, seq):
    t, d = x1.shape
    tm = TOKEN_TILE
    return pl.pallas_call(
        functools.partial(_conv_proj_kernel, tiles_per_seq=seq // tm),
        name="conv_proj",
        grid=(t // tm,),
        in_specs=[
            pl.BlockSpec((tm, d), lambda i: (i, 0)),
            pl.BlockSpec((1, d), lambda i: (0, 0)),
            _w_in_window(_CONV0, 3 * CONV_WIDTH),
            pl.BlockSpec(conv_w.shape, lambda i: (0, 0)),
            pl.BlockSpec((1, CONV_WIDTH), lambda i: (0, 0)),
        ],
        out_specs=pl.BlockSpec((tm, CONV_WIDTH), lambda i: (i, 0)),
        out_shape=jax.ShapeDtypeStruct((t, CONV_WIDTH), BF16),
        scratch_shapes=[pltpu.VMEM((8, CONV_WIDTH), F32)],
        compiler_params=pltpu.CompilerParams(
            dimension_semantics=("arbitrary",), vmem_limit_bytes=VMEM_LIMIT),
    )(x1, g, w_in, conv_w, gc)


def _rope(x, cos, sin_signed, first_half):
    outs = []
    for k in range(x.shape[1] // LANES):
        xc = x[:, k * LANES:(k + 1) * LANES]
        partner = jnp.where(first_half, pltpu.roll(xc, LANES - HEAD_DIM // 2, 1),
                            pltpu.roll(xc, HEAD_DIM // 2, 1))
        outs.append(xc * cos + partner * sin_signed)
    return outs


_CONV0 = 0
_Q0 = 3 * CONV_WIDTH
_KC0 = _Q0 + ATTN_WIDTH
_VC0 = _KC0 + KV_WIDTH
_KS0 = _VC0 + KV_WIDTH
_VS0 = _KS0 + KV_WIDTH
_KW0 = _VS0 + KV_WIDTH
_VW0 = _KW0 + KV_WIDTH
_G0 = _VW0 + KV_WIDTH
IN_WIDTH = _G0 + N_HEADS * N_GATES
IN_WIDTH_PADDED = -(-IN_WIDTH // LANES) * LANES


def _w_in_window(c0, width):
    assert c0 % width == 0
    return pl.BlockSpec((D_MODEL, width), lambda *_: (0, c0 // width), pipeline_mode=pl.Buffered(1))


def _attn_proj_kernel(x_ref, g_ref, wq_ref, wkc_ref, wvc_ref, wks_ref, wkw_ref, wgate_ref, wvt_ref,
                      cos_ref, sin_ref,
                      q_ref, kc_ref, vc_ref, ks_ref, kw_ref, vs_ref, vw_ref, gate_ref):
    tm = x_ref.shape[0]
    half = tm // 2
    lane = lax.broadcasted_iota(jnp.int32, (half, LANES), 1)
    first_half = (lane & (HEAD_DIM - 1)) < HEAD_DIM // 2

    def one_half(k):
        rows = slice(k * half, (k + 1) * half)
        h = (_rms(x_ref[rows, :]) * g_ref[...]).astype(BF16)
        cos = cos_ref[rows, :]
        sin = sin_ref[rows, :]

        def proj(w_ref):
            return _dot(h, w_ref[...])

        def rope_to(ref, w_ref, dtype):
            for n, piece in enumerate(_rope(proj(w_ref), cos, sin, first_half)):
                ref[rows, n * LANES:(n + 1) * LANES] = piece.astype(dtype)

        yield
        for n, piece in enumerate(_rope(proj(wq_ref), cos, sin, first_half)):
            q_ref[0, n * LANES:(n + 1) * LANES, rows] = (piece * Q_SCALE).T.astype(BF16)
        yield
        rope_to(kc_ref, wkc_ref, F32)
        vc_ref[rows, :] = proj(wvc_ref)
        rope_to(ks_ref, wks_ref, BF16)
        rope_to(kw_ref, wkw_ref, BF16)
        yield
        gate_ref[0, :, rows] = jax.nn.sigmoid(proj(wgate_ref)).T
        vt = _dot_nt(wvt_ref[...], h)
        chunks = half // Q_BLOCK
        for n in range(chunks):
            cs = slice(n * Q_BLOCK, (n + 1) * Q_BLOCK)
            vs_ref[0, k * chunks + n] = vt[0:KV_WIDTH, cs].astype(BF16)
            vw_ref[0, k * chunks + n] = vt[KV_WIDTH:2 * KV_WIDTH, cs].astype(BF16)

    _round_robin([one_half(0), one_half(1)])


def _attn_proj(x1, g, w_in, w_vt, cos, sin, seq):
    t, d = x1.shape
    tm = TOKEN_TILE
    tps = seq // tm
    bsz = t // seq
    cpt = tm // Q_BLOCK
    row = lambda w: pl.BlockSpec((tm, w), lambda i: (i, 0))
    col = lambda w: pl.BlockSpec((1, w, tm), lambda i: (i // tps, 0, i % tps))
    chunked = pl.BlockSpec((1, cpt, KV_WIDTH, Q_BLOCK), lambda i: (i // tps, i % tps, 0, 0))
    kv_dtypes = [F32, F32, BF16, BF16]
    out_specs = [col(ATTN_WIDTH)] + [row(KV_WIDTH)] * 4 + [chunked, chunked, col(GATE_ROWS)]
    out_shape = ([jax.ShapeDtypeStruct((bsz, ATTN_WIDTH, seq), BF16)]
                 + [jax.ShapeDtypeStruct((t, KV_WIDTH), dt) for dt in kv_dtypes]
                 + [jax.ShapeDtypeStruct((bsz, seq // Q_BLOCK, KV_WIDTH, Q_BLOCK), BF16)] * 2
                 + [jax.ShapeDtypeStruct((bsz, GATE_ROWS, seq), F32)])
    windows = [_w_in_window(_Q0, ATTN_WIDTH)] + [
        _w_in_window(c0, KV_WIDTH) for c0 in (_KC0, _VC0, _KS0, _KW0)] + [
        _w_in_window(_G0, GATE_ROWS)]
    return pl.pallas_call(
        _attn_proj_kernel,
        name="attn_proj",
        grid=(t // tm,),
        in_specs=[row(d), pl.BlockSpec((1, d), lambda i: (0, 0))] + windows + [
            _resident(w_vt.shape),
            pl.BlockSpec((tm, LANES), lambda i: (i % tps, 0)),
            pl.BlockSpec((tm, LANES), lambda i: (i % tps, 0)),
        ],
        out_specs=out_specs,
        out_shape=out_shape,
        compiler_params=pltpu.CompilerParams(
            dimension_semantics=("parallel",), vmem_limit_bytes=VMEM_LIMIT),
    )(x1, g, *([w_in] * len(windows)), w_vt, cos, sin)


HEADS_PER_TILE = LANES // HEAD_DIM


def _compress_kernel(*refs, transposed):
    tok_refs = refs[:N_KV // HEADS_PER_TILE]
    pe_ref, w1_ref, w2_ref, o_ref = refs[N_KV // HEADS_PER_TILE:]
    n = tok_refs[0].shape[0] // CMP_STRIDE
    for t, tok_ref in enumerate(tok_refs):
        first = jnp.zeros((n, HEADS_PER_TILE * CMP_HIDDEN), F32)
        second = jnp.zeros((n, HEADS_PER_TILE * CMP_HIDDEN), F32)
        for l in range(CMP_STRIDE):
            rows = tok_ref[pl.ds(l, n, stride=CMP_STRIDE), :]
            first = first + _dot((rows + pe_ref[l]).astype(BF16), w1_ref[l])
            second = second + _dot((rows + pe_ref[l + CMP_STRIDE]).astype(BF16),
                                   w1_ref[l + CMP_STRIDE])
        hid = jax.nn.gelu(first + pltpu.roll(second, n - 1, 0)).astype(BF16)
        for k in range(HEADS_PER_TILE):
            hid_h = hid[:, k * CMP_HIDDEN:(k + 1) * CMP_HIDDEN]
            h = t * HEADS_PER_TILE + k
            if transposed:
                o_ref[0, h] = _dot_nt(w2_ref[...], hid_h).astype(BF16)
            else:
                o_ref[0, h] = _dot(hid_h, w2_ref[...]).astype(BF16)


def _compress(tok, pe, w1, w2, seq, transposed):
    bsz = tok.shape[0] // seq
    n = seq // CMP_STRIDE
    out_block = (1, N_KV, HEAD_DIM, n) if transposed else (1, N_KV, n, LANES)
    lane_tile = lambda t: pl.BlockSpec((seq, LANES), lambda b: (b, t))
    return pl.pallas_call(
        functools.partial(_compress_kernel, transposed=transposed),
        name="compress_v" if transposed else "compress_k",
        grid=(bsz,),
        in_specs=[lane_tile(t) for t in range(N_KV // HEADS_PER_TILE)] + [
            _resident(pe.shape),
            _resident(w1.shape),
            _resident(w2.shape),
        ],
        out_specs=pl.BlockSpec(out_block, lambda b: (b, 0, 0, 0)),
        out_shape=jax.ShapeDtypeStruct((bsz,) + out_block[1:], BF16),
        compiler_params=pltpu.CompilerParams(
            dimension_semantics=("parallel",), vmem_limit_bytes=VMEM_LIMIT),
    )(*([tok] * (N_KV // HEADS_PER_TILE)), pe, w1, w2)


def _tile4(x):
    return jnp.concatenate([x] * GROUP, axis=1)


def _col_max(x):
    return jnp.max(x, axis=0, keepdims=True)


def _col_sum(x):
    return jnp.sum(x, axis=0, keepdims=True)


def _with_ones(v):
    ones = jnp.where(_iota((ONES_ROWS, v.shape[1]), 0) == 0, 1.0, 0.0).astype(v.dtype)
    return jnp.concatenate([v, ones], axis=0)


def _nsa_kernel(q_ref, gate_ref, kc_ref, vc_ref, ks_ref, kw_ref, vs_ref, vw_ref, o_ref,
                kse_ref, kwe_ref, imp_ref, *, seq):
    hp = pl.program_id(1)
    c = pl.program_id(2)
    t0 = c * Q_BLOCK
    n_cmp = kc_ref.shape[2]
    n_slc = seq // SLC_BLOCK
    cols = GROUP * Q_BLOCK
    n_chunks = seq // Q_BLOCK
    chunks_per_tile = KEY_TILE // Q_BLOCK
    heads = range(HEADS_PER_STEP)
    assert n_slc + HEAD_DIM == LANES and n_slc % 8 == 0

    @pl.when(c == 0)
    def _():
        r = _iota((KV_WIDTH, 2 * LANES), 0)
        ln = _iota((KV_WIDTH, 2 * LANES), 1)
        for hh in heads:
            h = hp * HEADS_PER_STEP + hh
            pick = ((r == h * HEAD_DIM + ln) & (ln < HEAD_DIM)).astype(BF16)

            def build_k(i, _, hh=hh, pick=pick):
                sl = pl.ds(pl.multiple_of(i * KEY_TILE, KEY_TILE), KEY_TILE)
                pos = i * KEY_TILE + _iota((KEY_TILE, 2 * LANES), 0)
                lane = _iota((KEY_TILE, 2 * LANES), 1)
                in_chunk = lane - LANES == (pos & (Q_BLOCK - 1))
                in_block = lane - HEAD_DIM == _div_pow2(pos, SLC_BLOCK)
                kse_ref[hh, sl, :] = (_dot(ks_ref[0, sl, :], pick)
                                      + jnp.where(in_chunk | in_block, 1.0, 0.0)).astype(BF16)
                kwe_ref[hh, sl, :] = (_dot(kw_ref[0, sl, :], pick)
                                      + jnp.where(in_chunk, 1.0, 0.0)).astype(BF16)
                return 0

            lax.fori_loop(0, seq // KEY_TILE, build_k, 0)
            kwe_ref[hh, pl.ds(seq, Q_BLOCK), :] = jnp.where(
                _iota((Q_BLOCK, 2 * LANES), 1) == HEAD_DIM, 1.0, 0.0).astype(BF16)

    def v_chunk(ref, hh, idx):
        return _with_ones(ref[0, idx, hh * HEAD_DIM:(hh + 1) * HEAD_DIM, :])

    def v_tile(hh, kt):
        return _with_ones(jnp.concatenate(
            [vs_ref[0, kt * chunks_per_tile + k, hh * HEAD_DIM:(hh + 1) * HEAD_DIM, :]
             for k in range(chunks_per_tile)], axis=1))

    def soft(scores, m):
        return [jnp.exp2(s - m).astype(BF16) for s in scores]

    kpos = _iota((Q_BLOCK, Q_BLOCK), 0)
    tpos = _iota((Q_BLOCK, Q_BLOCK), 1)
    causal = _tile4(jnp.where(kpos <= tpos, 0.0, NEG).astype(BF16))
    recent = _tile4(jnp.where(kpos > tpos, 0.0, NEG).astype(BF16))
    pad = jnp.zeros((HEAD_DIM, cols), BF16)
    dummy_row = jnp.where(_iota((HEAD_DIM, cols), 0) == 0, NEG, 0.0).astype(BF16)
    jj = _iota((n_slc, n_cmp), 0)
    ii = _iota((n_slc, n_cmp), 1)
    ov = (jnp.minimum(ii * CMP_STRIDE + CMP_BLOCK, jj * SLC_BLOCK + SLC_BLOCK)
          - jnp.maximum(ii * CMP_STRIDE, jj * SLC_BLOCK))
    ov = (jnp.maximum(ov, 0).astype(F32) * (1.0 / CMP_BLOCK)).astype(BF16)
    c_ok = (_iota((n_cmp, Q_BLOCK), 0) * CMP_STRIDE + (CMP_BLOCK - 1)
            <= t0 + _iota((n_cmp, Q_BLOCK), 1))
    c_bias = _tile4(jnp.where(c_ok, 0.0, NEG))
    any_ok = t0 + (_iota((1, cols), 1) & (Q_BLOCK - 1)) >= CMP_BLOCK - 1
    jb = _iota((n_slc, Q_BLOCK), 0)
    tb = _div_pow2(t0 + _iota((n_slc, Q_BLOCK), 1), SLC_BLOCK)
    forced = (jb == 0) | ((jb <= tb) & (jb > tb - N_LOCAL))
    sub = _iota((8, Q_BLOCK), 0)
    n_win = WIN_KEYS // Q_BLOCK
    first = c - (n_win - 1)
    k_chunk = [jnp.where(first + i >= 0, first + i, n_chunks) for i in range(n_win)]
    v_idx = [jnp.maximum(first + i, 0) for i in range(n_win)]
    k_rows = [pl.ds(pl.multiple_of(ch * Q_BLOCK, Q_BLOCK), Q_BLOCK) for ch in k_chunk]

    def before_sweep(hh):
        qt = q_ref[0, hh * GROUP_WIDTH:(hh + 1) * GROUP_WIDTH, :]
        wq = jnp.concatenate([qt[g * HEAD_DIM:(g + 1) * HEAD_DIM, :] for g in range(GROUP)], axis=1)

        sc = _dot(kc_ref[0, hh], jnp.concatenate([wq, pad], axis=0)) + c_bias
        yield
        pc = jnp.exp2(sc - _col_max(sc))
        pc = pc * jnp.where(any_ok, 1.0 / _col_sum(pc), 0.0)
        yield
        o_cmp = _dot(vc_ref[0, hh], pc.astype(BF16))

        pcs = pc[:, 0:Q_BLOCK]
        for g in range(1, GROUP):
            pcs = pcs + pc[:, g * Q_BLOCK:(g + 1) * Q_BLOCK]
        pcs_hi = pcs.astype(BF16)
        pcs_lo = (pcs - pcs_hi.astype(F32)).astype(BF16)
        imp = _dot(ov, pcs_hi) + _dot(ov, pcs_lo)
        yield
        imp = jnp.where(forced, FORCE, imp)
        imp = jnp.where(jb <= tb, imp, NEG)
        imp_ref[hh] = imp

        w_causal = jnp.concatenate([wq, pad, causal], axis=0)
        w_mid = jnp.concatenate([wq, dummy_row], axis=0)
        w_recent = jnp.concatenate([wq, dummy_row, recent], axis=0)
        sw = [_dot(kwe_ref[hh, k_rows[0], :], w_recent)]
        sw += [_dot(kwe_ref[hh, k_rows[i], 0:LANES], w_mid) for i in range(1, n_win - 1)]
        sw += [_dot(kwe_ref[hh, k_rows[n_win - 1], :], w_causal)]
        s_d = _dot(kse_ref[hh, pl.ds(pl.multiple_of(t0, Q_BLOCK), Q_BLOCK), :], w_causal)
        yield

        n_groups = n_slc // 8
        mine = [imp[8 * r:8 * r + 8, :] for r in range(n_groups)]
        ahead_count = [jnp.zeros((8, Q_BLOCK), F32)] * n_groups
        for j in range(n_slc):
            other = imp_ref[hh, j:j + 1, :]
            for r in range(n_groups):
                if 8 * r > j:
                    ahead = other >= mine[r]
                elif 8 * r + 7 <= j:
                    ahead = other > mine[r]
                else:
                    ahead = (other > mine[r]) | ((other == mine[r]) & (sub > j - 8 * r))
                ahead_count[r] = ahead_count[r] + jnp.where(ahead, 1.0, 0.0)
            if j % 8 == 7:
                yield
        sel = (jnp.concatenate(ahead_count, axis=0) < SLC_TOPK) & (imp > 0.5 * NEG)
        sel_bias = jnp.where(sel & (jb * SLC_BLOCK < t0), 0.0, NEG).astype(BF16)
        w_sel = jnp.concatenate([wq, _tile4(sel_bias)], axis=0)

        m_w = _col_max(sw[0])
        for s in sw[1:]:
            m_w = jnp.maximum(m_w, _col_max(s))
        yield
        p_w = soft(sw, m_w)
        yield
        o_win = None
        for idx, p in zip(v_idx, p_w):
            part = _dot(v_chunk(vw_ref, hh, idx), p)
            o_win = part if o_win is None else o_win + part
        m_d = _col_max(s_d)
        acc_d = _dot(v_chunk(vs_ref, hh, c), soft([s_d], m_d)[0])
        return w_sel, o_cmp, o_win, (m_d, acc_d)

    pre = _round_robin([before_sweep(hh) for hh in heads])

    def scores(hh, kt):
        k0 = pl.multiple_of(kt * KEY_TILE, KEY_TILE)
        return _dot(kse_ref[hh, pl.ds(k0, KEY_TILE), 0:LANES], pre[hh][0])

    def sweep(items, states, lookahead=2):
        states = list(states)
        s, m_new, p = {}, {}, {}

        def issue(n):
            s[n] = scores(*items[n])

        def value_matmul(n):
            hh, kt = items[n]
            m, acc = states[hh]
            states[hh] = m_new[n], jnp.exp2(m - m_new[n]) * acc + _dot(v_tile(hh, kt), p[n])

        for n in range(min(lookahead, len(items))):
            issue(n)
        for n, (hh, _) in enumerate(items):
            m_new[n] = jnp.maximum(states[hh][0], _col_max(s[n]))
            if n + lookahead < len(items):
                issue(n + lookahead)
            p[n] = jnp.exp2(s[n] - m_new[n]).astype(BF16)
            if n >= 1:
                value_matmul(n - 1)
        value_matmul(len(items) - 1)
        return tuple(states)

    def tile_pair(i, states):
        return sweep([(hh, 2 * i + k) for k in range(2) for hh in heads], states)

    def last_tile(states):
        return sweep([(hh, n_tiles - 1) for hh in heads], states)

    n_tiles = _div_pow2(t0 + KEY_TILE - 1, KEY_TILE)
    swept = lax.fori_loop(0, _div_pow2(n_tiles, 2), tile_pair, tuple(p[3] for p in pre))
    swept = lax.cond((n_tiles & 1) == 1, last_tile, lambda states: states, swept)

    for hh in heads:
        _, o_cmp, o_win, _ = pre[hh]
        o_slc = swept[hh][1]
        gates = gate_ref[0, hh * GROUP * N_GATES:(hh + 1) * GROUP * N_GATES, :]
        inv_ls = 1.0 / o_slc[HEAD_DIM:HEAD_DIM + 1, :]
        inv_lw = 1.0 / o_win[HEAD_DIM:HEAD_DIM + 1, :]
        pieces = []
        for g in range(GROUP):
            cs = slice(g * Q_BLOCK, (g + 1) * Q_BLOCK)
            row = g * N_GATES
            pieces.append(gates[row:row + 1, :] * o_cmp[:, cs]
                          + gates[row + 1:row + 2, :] * (o_slc[0:HEAD_DIM, cs] * inv_ls[:, cs])
                          + gates[row + 2:row + 3, :] * (o_win[0:HEAD_DIM, cs] * inv_lw[:, cs]))
        o_ref[0, hh * GROUP_WIDTH:(hh + 1) * GROUP_WIDTH, :] = jnp.concatenate(pieces, axis=0)


def _nsa(q_t, gates_t, kc, vc_t, ks, kw, vs_t, vw_t):
    bsz, _, seq = q_t.shape
    n_cmp = kc.shape[2]
    hps = HEADS_PER_STEP
    assert hps == N_KV
    k_spec = pl.BlockSpec((1, seq, KV_WIDTH), lambda b, h, c: (b, 0, 0))
    v_spec = pl.BlockSpec((1, seq // Q_BLOCK, hps * HEAD_DIM, Q_BLOCK), lambda b, h, c: (b, 0, h, 0))
    return pl.pallas_call(
        functools.partial(_nsa_kernel, seq=seq),
        name="nsa",
        grid=(bsz, N_KV // hps, seq // Q_BLOCK),
        in_specs=[
            pl.BlockSpec((1, hps * GROUP_WIDTH, Q_BLOCK), lambda b, h, c: (b, h, c)),
            pl.BlockSpec((1, GATE_ROWS, Q_BLOCK), lambda b, h, c: (b, 0, c)),
            pl.BlockSpec((1, hps, n_cmp, LANES), lambda b, h, c: (b, h, 0, 0)),
            pl.BlockSpec((1, hps, HEAD_DIM, n_cmp), lambda b, h, c: (b, h, 0, 0)),
            k_spec, k_spec, v_spec, v_spec,
        ],
        out_specs=pl.BlockSpec((1, hps * GROUP_WIDTH, Q_BLOCK), lambda b, h, c: (b, h, c)),
        out_shape=jax.ShapeDtypeStruct((bsz, ATTN_WIDTH, seq), F32),
        scratch_shapes=[
            pltpu.VMEM((hps, seq, 2 * LANES), BF16),
            pltpu.VMEM((hps, seq + Q_BLOCK, 2 * LANES), BF16),
            pltpu.VMEM((hps, seq // SLC_BLOCK, Q_BLOCK), F32),
        ],
        compiler_params=pltpu.CompilerParams(
            dimension_semantics=("parallel", "parallel", "arbitrary"),
            vmem_limit_bytes=VMEM_LIMIT),
    )(q_t, gates_t, kc, vc_t, ks, kw, vs_t, vw_t)


def _rope_tables(seq):
    pos = jnp.arange(seq, dtype=F32)
    inv = ROPE_THETA ** (-jnp.arange(0, HEAD_DIM, 2, dtype=F32) / HEAD_DIM)
    ang = pos[:, None] * inv[None, :]
    cos, sin = jnp.cos(ang), jnp.sin(ang)
    reps = LANES // HEAD_DIM
    return (jnp.tile(jnp.concatenate([cos, cos], axis=1), (1, reps)),
            jnp.tile(jnp.concatenate([-sin, sin], axis=1), (1, reps)))


def _compress_params(w1, w2, pe):
    eye = jnp.eye(HEADS_PER_TILE, dtype=w1.dtype)
    w1 = jnp.einsum('ldf,hk->lhdkf', w1, eye).reshape(CMP_BLOCK, LANES, HEADS_PER_TILE * CMP_HIDDEN)
    pe = jnp.tile(pe, (1, HEADS_PER_TILE)).reshape(CMP_BLOCK, 1, LANES)
    return pe, w1.astype(BF16), w2.astype(BF16)


def kernel(x, ffn1_norm, ffn1_wg, ffn1_wu, ffn1_wd, mix_norm, w_in, conv_w, cmp_k_w1, cmp_k_w2,
           cmp_k_pe, cmp_v_w1, cmp_v_w2, cmp_v_pe, conv_out_norm, attn_out_norm, w_out, ffn2_norm,
           ffn2_wg, ffn2_wu, ffn2_wd, final_norm):
    bsz, seq, d = x.shape
    assert CMP_BLOCK == 2 * CMP_STRIDE and seq % TOKEN_TILE == 0 and seq % KEY_TILE == 0
    assert x.shape[0] * seq % TOKEN_TILE == 0 and ffn1_wg.shape[0] == 1
    xt = x.reshape(bsz * seq, d)
    cos, sin = _rope_tables(seq)

    assert w_in.shape[1:] == (D_MODEL, IN_WIDTH)
    w = jnp.pad(w_in[0].astype(BF16), ((0, 0), (0, IN_WIDTH_PADDED - IN_WIDTH)))
    w_vt = jnp.concatenate([w[:, _VS0:_VS0 + KV_WIDTH], w[:, _VW0:_VW0 + KV_WIDTH]], axis=1).T

    x1, wg2, wu2, wd2, wo = _ffn("ffn1", xt, ffn1_norm, ffn1_wg[0].astype(BF16),
                                 ffn1_wu[0].astype(BF16), ffn1_wd[0].astype(BF16),
                                 later_weights=[ffn2_wg[0], ffn2_wu[0], ffn2_wd[0], w_out[0]])
    yc = _conv_proj(x1, mix_norm, w, conv_w[0], conv_out_norm, seq)
    q_t, k_c, v_c, k_s, k_w, vs_t, vw_t, gates_t = _attn_proj(x1, mix_norm, w, w_vt, cos, sin, seq)

    pe_k, w1_k, w2_k = _compress_params(cmp_k_w1[0], cmp_k_w2[0], cmp_k_pe[0])
    pe_v, w1_v, w2_v = _compress_params(cmp_v_w1[0], cmp_v_w2[0], cmp_v_pe[0])
    kc = _compress(k_c, pe_k, w1_k, jnp.pad(w2_k, ((0, 0), (0, LANES - HEAD_DIM))), seq,
                   transposed=False)
    vc_t = _compress(v_c, pe_v, w1_v, w2_v.T, seq, transposed=True)

    per_batch = lambda a: a.reshape(bsz, seq, a.shape[-1])
    ya_t = _nsa(q_t, gates_t, kc, vc_t, per_batch(k_s), per_batch(k_w), vs_t, vw_t)

    x2 = _out_proj(x1, yc, ya_t, attn_out_norm, wo)
    (out,) = _ffn("ffn2", x2, ffn2_norm, wg2, wu2, wd2, final_gain=final_norm.reshape(1, d))
    return out.reshape(bsz, seq, d)
```

```python
import functools

import jax
import jax.numpy as jnp
from jax import lax
from jax.experimental import pallas as pl
from jax.experimental.pallas import tpu as pltpu

F32 = jnp.float32
BF16 = jnp.bfloat16

D_MODEL = 2048
D_FF = 5632
CONV_WIDTH = 1024
N_HEADS = 16
N_KV = 4
GROUP = N_HEADS // N_KV
HEAD_DIM = 64
ATTN_WIDTH = N_HEADS * HEAD_DIM
KV_WIDTH = N_KV * HEAD_DIM
CMP_BLOCK = 32
CMP_STRIDE = 16
CMP_HIDDEN = 256
SLC_BLOCK = 64
SLC_TOPK = 16
N_LOCAL = 2
WINDOW = 512
Q_BLOCK = 128
N_GATES = 3
ROPE_THETA = 10000.0
EPS = 1e-6
NEG = -1e30
FORCE = 1e9

LANES = 128
GROUP_WIDTH = GROUP * HEAD_DIM
GATE_ROWS = LANES
TOKEN_TILE = 512
FFN_TOKEN_TILE = 1024
FF_TILE = 512
KEY_TILE = 256
WIN_KEYS = WINDOW + Q_BLOCK
ONES_ROWS = 16
HEADS_PER_STEP = 4
VMEM_LIMIT = 56 * 1024 * 1024
FFN_VMEM_LIMIT = 62 * 1024 * 1024
Q_SCALE = HEAD_DIM ** -0.5 * 1.4426950408889634


def _rms(x):
    return x * lax.rsqrt(jnp.mean(x * x, axis=-1, keepdims=True) + EPS)


def _dot(a, b):
    return jnp.dot(a, b, preferred_element_type=F32)


def _dot_nt(a, b):
    return lax.dot_general(a, b, (((1,), (1,)), ((), ())), preferred_element_type=F32)


def _div_pow2(x, n):
    assert n & (n - 1) == 0
    return jnp.right_shift(x, n.bit_length() - 1)


def _iota(shape, dim):
    return lax.broadcasted_iota(jnp.int32, shape, dim)


def _resident(shape):
    return pl.BlockSpec(shape, lambda *_: (0,) * len(shape), pipeline_mode=pl.Buffered(1))


def _round_robin(gens, skew=1):
    results = [None] * len(gens)
    live = {}
    waiting = dict(enumerate(gens))
    rnd = 0
    while waiting or live:
        for k in [k for k in waiting if rnd >= k * skew]:
            live[k] = waiting.pop(k)
        for k in sorted(live):
            try:
                next(live[k])
            except StopIteration as done:
                results[k] = done.value
                del live[k]
        rnd += 1
    return results


def _ffn_sweep(make_h, finish, wg_ref, wu_ref, wd_ref, h_ref, acc_ref):
    j = pl.program_id(1)
    last = pl.num_programs(1) - 1
    tm = h_ref.shape[0]
    halves = [slice(0, tm // 2), slice(tm // 2, tm)]

    def hidden(h):
        a = _dot(h, wg_ref[...])
        u = _dot(h, wu_ref[...])
        return (jax.nn.silu(a) * u).astype(BF16)

    @pl.when(j == 0)
    def _():
        def first(rows):
            h = yield from make_h(rows)
            h_ref[rows, :] = h
            yield
            act = hidden(h)
            yield
            acc_ref[rows, :] = _dot(act, wd_ref[...])

        _round_robin([first(rows) for rows in halves])

    @pl.when((j > 0) & (j < last))
    def _():
        def middle(rows):
            act = hidden(h_ref[rows, :])
            yield
            acc_ref[rows, :] += _dot(act, wd_ref[...])

        _round_robin([middle(rows) for rows in halves])

    @pl.when(j == last)
    def _():
        def final(rows):
            act = hidden(h_ref[rows, :])
            yield
            finish(rows, acc_ref[rows, :] + _dot(act, wd_ref[...]))

        _round_robin([final(rows) for rows in halves])


def _ffn_kernel(x_ref, g_ref, wg_ref, wu_ref, wd_ref, *refs, n_cast, final_norm):
    refs = list(refs)
    fin_ref = refs.pop(0) if final_norm else None
    cast_in = refs[:n_cast]
    o_ref = refs[n_cast]
    cast_out = refs[n_cast + 1:2 * n_cast + 1]
    (h_ref,) = refs[2 * n_cast + 1:]
    acc_ref = o_ref
    for src, dst in zip(cast_in, cast_out):
        dst[...] = src[...].astype(BF16)

    def make_h(rows):
        return (_rms(x_ref[rows, :]) * g_ref[...]).astype(BF16)
        yield

    def finish(rows, acc):
        y = x_ref[rows, :] + 0.5 * acc
        o_ref[rows, :] = _rms(y) * fin_ref[...] if final_norm else y

    _ffn_sweep(make_h, finish, wg_ref, wu_ref, wd_ref, h_ref, acc_ref)


def _cast_block(shape, ni, nj):
    rows, cols = shape
    sub, lane = 16, LANES
    if rows % ni == 0 and cols % nj == 0 and (rows // ni) % sub == 0 and (cols // nj) % lane == 0:
        return pl.BlockSpec((rows // ni, cols // nj), lambda i, j: (i, j))
    if rows % (ni * nj) == 0 and (rows // (ni * nj)) % sub == 0:
        return pl.BlockSpec((rows // (ni * nj), cols), lambda i, j: (i * nj + j, 0))
    assert rows % ni == 0 and (rows // ni) % sub == 0
    return pl.BlockSpec((rows // ni, cols), lambda i, j: (i, 0))


def _ffn(name, x, g, wg, wu, wd, later_weights=(), final_gain=None):
    t, d = x.shape
    f = wg.shape[1]
    tm = FFN_TOKEN_TILE
    grid = (t // tm, f // FF_TILE)
    cast_specs = [_cast_block(w.shape, *grid) for w in later_weights]
    gain = lambda: pl.BlockSpec((1, d), lambda i, j: (0, 0))
    final = [] if final_gain is None else [final_gain]
    return pl.pallas_call(
        functools.partial(_ffn_kernel, n_cast=len(later_weights), final_norm=bool(final)),
        name=name,
        grid=grid,
        in_specs=[
            pl.BlockSpec((tm, d), lambda i, j: (i, 0)),
            gain(),
            pl.BlockSpec((d, FF_TILE), lambda i, j: (0, j)),
            pl.BlockSpec((d, FF_TILE), lambda i, j: (0, j)),
            pl.BlockSpec((FF_TILE, d), lambda i, j: (j, 0)),
        ] + [gain() for _ in final] + cast_specs,
        out_specs=[pl.BlockSpec((tm, d), lambda i, j: (i, 0))] + cast_specs,
        out_shape=[jax.ShapeDtypeStruct((t, d), F32)] + [
            jax.ShapeDtypeStruct(w.shape, BF16) for w in later_weights],
        scratch_shapes=[pltpu.VMEM((tm, d), BF16)],
        compiler_params=pltpu.CompilerParams(
            dimension_semantics=("arbitrary", "arbitrary"), vmem_limit_bytes=FFN_VMEM_LIMIT),
    )(x, g, wg, wu, wd, *final, *later_weights)


def _out_proj_kernel(x_ref, yc_ref, ya_ref, ga_ref, wo_ref, o_ref):
    tm = x_ref.shape[0]

    def one_half(rows):
        ya = (_rms(ya_ref[0, :, rows].T) * ga_ref[...]).astype(BF16)
        yield
        mix = _dot(yc_ref[rows, :], wo_ref[0:CONV_WIDTH, :]) + _dot(ya, wo_ref[CONV_WIDTH:, :])
        yield
        o_ref[rows, :] = x_ref[rows, :] + mix

    _round_robin([one_half(slice(0, tm // 2)), one_half(slice(tm // 2, tm))])


def _out_proj(x1, yc, ya_t, ga, wo):
    t, d = x1.shape
    tm = TOKEN_TILE
    tps = ya_t.shape[2] // tm
    return pl.pallas_call(
        _out_proj_kernel,
        name="out_proj",
        grid=(t // tm,),
        in_specs=[
            pl.BlockSpec((tm, d), lambda i: (i, 0)),
            pl.BlockSpec((tm, CONV_WIDTH), lambda i: (i, 0)),
            pl.BlockSpec((1, ATTN_WIDTH, tm), lambda i: (i // tps, 0, i % tps)),
            pl.BlockSpec((1, ATTN_WIDTH), lambda i: (0, 0)),
            _resident(wo.shape),
        ],
        out_specs=pl.BlockSpec((tm, d), lambda i: (i, 0)),
        out_shape=jax.ShapeDtypeStruct((t, d), F32),
        compiler_params=pltpu.CompilerParams(
            dimension_semantics=("parallel",), vmem_limit_bytes=VMEM_LIMIT),
    )(x1, yc, ya_t, ga, wo)


def _conv_proj_kernel(x_ref, g_ref, w_ref, cw_ref, gc_ref, o_ref, carry_ref, *, tiles_per_seq):
    i = pl.program_id(0)

    @pl.when(i % tiles_per_seq == 0)
    def _():
        carry_ref[...] = jnp.zeros_like(carry_ref)

    tm = x_ref.shape[0]
    half = tm // 2
    tails = {-1: carry_ref[...]}

    def one_half(k):
        rows = slice(k * half, (k + 1) * half)
        h = (_rms(x_ref[rows, :]) * g_ref[...]).astype(BF16)
        yield
        c_h = _dot(h, w_ref[:, 0:CONV_WIDTH])
        c_c = _dot(h, w_ref[:, 2 * CONV_WIDTH:3 * CONV_WIDTH])
        u = c_c * c_h
        tails[k] = u[half - 8:half, :]
        c_b = _dot(h, w_ref[:, CONV_WIDTH:2 * CONV_WIDTH])
        yield
        row = lax.broadcasted_iota(jnp.int32, u.shape, 0)
        prev1 = tails[k - 1][7:8, :]
        prev2 = tails[k - 1][6:7, :]
        u1 = jnp.where(row == 0, prev1, pltpu.roll(u, 1, 0))
        u2 = jnp.where(row == 0, prev2, jnp.where(row == 1, prev1, pltpu.roll(u, 2, 0)))
        y = c_b * (cw_ref[0:1, :] * u2 + cw_ref[1:2, :] * u1 + cw_ref[2:3, :] * u)
        o_ref[rows, :] = (_rms(y) * gc_ref[...]).astype(BF16)

    _round_robin([one_half(0), one_half(1)])
    carry_ref[...] = tails[1]


def _conv_proj(x1, g, w_in, conv_w, gc, seq):
    t, d = x1.shape
    tm = TOKEN_TILE
    return pl.pallas_call(
        functools.partial(_conv_proj_kernel, tiles_per_seq=seq // tm),
        name="conv_proj",
        grid=(t // tm,),
        in_specs=[
            pl.BlockSpec((tm, d), lambda i: (i, 0)),
            pl.BlockSpec((1, d), lambda i: (0, 0)),
            _w_in_window(_CONV0, 3 * CONV_WIDTH),
            pl.BlockSpec(conv_w.shape, lambda i: (0, 0)),
            pl.BlockSpec((1, CONV_WIDTH), lambda i: (0, 0)),
        ],
        out_specs=pl.BlockSpec((tm, CONV_WIDTH), lambda i: (i, 0)),
        out_shape=jax.ShapeDtypeStruct((t, CONV_WIDTH), BF16),
        scratch_shapes=[pltpu.VMEM((8, CONV_WIDTH), F32)],
        compiler_params=pltpu.CompilerParams(
            dimension_semantics=("arbitrary",), vmem_limit_bytes=VMEM_LIMIT),
    )(x1, g, w_in, conv_w, gc)


def _rope(x, cos, sin_signed, first_half):
    outs = []
    for k in range(x.shape[1] // LANES):
        xc = x[:, k * LANES:(k + 1) * LANES]
        partner = jnp.where(first_half, pltpu.roll(xc, LANES - HEAD_DIM // 2, 1),
                            pltpu.roll(xc, HEAD_DIM // 2, 1))
        outs.append(xc * cos + partner * sin_signed)
    return outs


_CONV0 = 0
_Q0 = 3 * CONV_WIDTH
_KC0 = _Q0 + ATTN_WIDTH
_VC0 = _KC0 + KV_WIDTH
_KS0 = _VC0 + KV_WIDTH
_VS0 = _KS0 + KV_WIDTH
_KW0 = _VS0 + KV_WIDTH
_VW0 = _KW0 + KV_WIDTH
_G0 = _VW0 + KV_WIDTH
IN_WIDTH = _G0 + N_HEADS * N_GATES
IN_WIDTH_PADDED = -(-IN_WIDTH // LANES) * LANES


def _w_in_window(c0, width):
    assert c0 % width == 0
    return pl.BlockSpec((D_MODEL, width), lambda *_: (0, c0 // width), pipeline_mode=pl.Buffered(1))


def _attn_proj_kernel(x_ref, g_ref, wq_ref, wkc_ref, wvc_ref, wks_ref, wkw_ref, wgate_ref, wvt_ref,
                      cos_ref, sin_ref,
                      q_ref, kc_ref, vc_ref, ks_ref, kw_ref, vs_ref, vw_ref, gate_ref):
    tm = x_ref.shape[0]
    half = tm // 2
    lane = lax.broadcasted_iota(jnp.int32, (half, LANES), 1)
    first_half = (lane & (HEAD_DIM - 1)) < HEAD_DIM // 2

    def one_half(k):
        rows = slice(k * half, (k + 1) * half)
        h = (_rms(x_ref[rows, :]) * g_ref[...]).astype(BF16)
        cos = cos_ref[rows, :]
        sin = sin_ref[rows, :]

        def proj(w_ref):
            return _dot(h, w_ref[...])

        def rope_to(ref, w_ref, dtype):
            for n, piece in enumerate(_rope(proj(w_ref), cos, sin, first_half)):
                ref[rows, n * LANES:(n + 1) * LANES] = piece.astype(dtype)

        yield
        for n, piece in enumerate(_rope(proj(wq_ref), cos, sin, first_half)):
            q_ref[0, n * LANES:(n + 1) * LANES, rows] = (piece * Q_SCALE).T.astype(BF16)
        yield
        rope_to(kc_ref, wkc_ref, F32)
        vc_ref[rows, :] = proj(wvc_ref)
        rope_to(ks_ref, wks_ref, BF16)
        rope_to(kw_ref, wkw_ref, BF16)
        yield
        gate_ref[0, :, rows] = jax.nn.sigmoid(proj(wgate_ref)).T
        vt = _dot_nt(wvt_ref[...], h)
        chunks = half // Q_BLOCK
        for n in range(chunks):
            cs = slice(n * Q_BLOCK, (n + 1) * Q_BLOCK)
            vs_ref[0, k * chunks + n] = vt[0:KV_WIDTH, cs].astype(BF16)
            vw_ref[0, k * chunks + n] = vt[KV_WIDTH:2 * KV_WIDTH, cs].astype(BF16)

    _round_robin([one_half(0), one_half(1)])


def _attn_proj(x1, g, w_in, w_vt, cos, sin, seq):
    t, d = x1.shape
    tm = TOKEN_TILE
    tps = seq // tm
    bsz = t // seq
    cpt = tm // Q_BLOCK
    row = lambda w: pl.BlockSpec((tm, w), lambda i: (i, 0))
    col = lambda w: pl.BlockSpec((1, w, tm), lambda i: (i // tps, 0, i % tps))
    chunked = pl.BlockSpec((1, cpt, KV_WIDTH, Q_BLOCK), lambda i: (i // tps, i % tps, 0, 0))
    kv_dtypes = [F32, F32, BF16, BF16]
    out_specs = [col(ATTN_WIDTH)] + [row(KV_WIDTH)] * 4 + [chunked, chunked, col(GATE_ROWS)]
    out_shape = ([jax.ShapeDtypeStruct((bsz, ATTN_WIDTH, seq), BF16)]
                 + [jax.ShapeDtypeStruct((t, KV_WIDTH), dt) for dt in kv_dtypes]
                 + [jax.ShapeDtypeStruct((bsz, seq // Q_BLOCK, KV_WIDTH, Q_BLOCK), BF16)] * 2
                 + [jax.ShapeDtypeStruct((bsz, GATE_ROWS, seq), F32)])
    windows = [_w_in_window(_Q0, ATTN_WIDTH)] + [
        _w_in_window(c0, KV_WIDTH) for c0 in (_KC0, _VC0, _KS0, _KW0)] + [
        _w_in_window(_G0, GATE_ROWS)]
    return pl.pallas_call(
        _attn_proj_kernel,
        name="attn_proj",
        grid=(t // tm,),
        in_specs=[row(d), pl.BlockSpec((1, d), lambda i: (0, 0))] + windows + [
            _resident(w_vt.shape),
            pl.BlockSpec((tm, LANES), lambda i: (i % tps, 0)),
            pl.BlockSpec((tm, LANES), lambda i: (i % tps, 0)),
        ],
        out_specs=out_specs,
        out_shape=out_shape,
        compiler_params=pltpu.CompilerParams(
            dimension_semantics=("parallel",), vmem_limit_bytes=VMEM_LIMIT),
    )(x1, g, *([w_in] * len(windows)), w_vt, cos, sin)


HEADS_PER_TILE = LANES // HEAD_DIM


def _compress_kernel(*refs, transposed):
    tok_refs = refs[:N_KV // HEADS_PER_TILE]
    pe_ref, w1_ref, w2_ref, o_ref = refs[N_KV // HEADS_PER_TILE:]
    n = tok_refs[0].shape[0] // CMP_STRIDE
    for t, tok_ref in enumerate(tok_refs):
        first = jnp.zeros((n, HEADS_PER_TILE * CMP_HIDDEN), F32)
        second = jnp.zeros((n, HEADS_PER_TILE * CMP_HIDDEN), F32)
        for l in range(CMP_STRIDE):
            rows = tok_ref[pl.ds(l, n, stride=CMP_STRIDE), :]
            first = first + _dot((rows + pe_ref[l]).astype(BF16), w1_ref[l])
            second = second + _dot((rows + pe_ref[l + CMP_STRIDE]).astype(BF16),
                                   w1_ref[l + CMP_STRIDE])
        hid = jax.nn.gelu(first + pltpu.roll(second, n - 1, 0)).astype(BF16)
        for k in range(HEADS_PER_TILE):
            hid_h = hid[:, k * CMP_HIDDEN:(k + 1) * CMP_HIDDEN]
            h = t * HEADS_PER_TILE + k
            if transposed:
                o_ref[0, h] = _dot_nt(w2_ref[...], hid_h).astype(BF16)
            else:
                o_ref[0, h] = _dot(hid_h, w2_ref[...]).astype(BF16)


def _compress(tok, pe, w1, w2, seq, transposed):
    bsz = tok.shape[0] // seq
    n = seq // CMP_STRIDE
    out_block = (1, N_KV, HEAD_DIM, n) if transposed else (1, N_KV, n, LANES)
    lane_tile = lambda t: pl.BlockSpec((seq, LANES), lambda b: (b, t))
    return pl.pallas_call(
        functools.partial(_compress_kernel, transposed=transposed),
        name="compress_v" if transposed else "compress_k",
        grid=(bsz,),
        in_specs=[lane_tile(t) for t in range(N_KV // HEADS_PER_TILE)] + [
            _resident(pe.shape),
            _resident(w1.shape),
            _resident(w2.shape),
        ],
        out_specs=pl.BlockSpec(out_block, lambda b: (b, 0, 0, 0)),
        out_shape=jax.ShapeDtypeStruct((bsz,) + out_block[1:], BF16),
        compiler_params=pltpu.CompilerParams(
            dimension_semantics=("parallel",), vmem_limit_bytes=VMEM_LIMIT),
    )(*([tok] * (N_KV // HEADS_PER_TILE)), pe, w1, w2)


def _tile4(x):
    return jnp.concatenate([x] * GROUP, axis=1)


def _col_max(x):
    return jnp.max(x, axis=0, keepdims=True)


def _col_sum(x):
    return jnp.sum(x, axis=0, keepdims=True)


def _with_ones(v):
    ones = jnp.where(_iota((ONES_ROWS, v.shape[1]), 0) == 0, 1.0, 0.0).astype(v.dtype)
    return jnp.concatenate([v, ones], axis=0)


def _nsa_kernel(q_ref, gate_ref, kc_ref, vc_ref, ks_ref, kw_ref, vs_ref, vw_ref, o_ref,
                kse_ref, kwe_ref, imp_ref, *, seq):
    hp = pl.program_id(1)
    c = pl.program_id(2)
    t0 = c * Q_BLOCK
    n_cmp = kc_ref.shape[2]
    n_slc = seq // SLC_BLOCK
    cols = GROUP * Q_BLOCK
    n_chunks = seq // Q_BLOCK
    chunks_per_tile = KEY_TILE // Q_BLOCK
    heads = range(HEADS_PER_STEP)
    assert n_slc + HEAD_DIM == LANES and n_slc % 8 == 0

    @pl.when(c == 0)
    def _():
        r = _iota((KV_WIDTH, 2 * LANES), 0)
        ln = _iota((KV_WIDTH, 2 * LANES), 1)
        for hh in heads:
            h = hp * HEADS_PER_STEP + hh
            pick = ((r == h * HEAD_DIM + ln) & (ln < HEAD_DIM)).astype(BF16)

            def build_k(i, _, hh=hh, pick=pick):
                sl = pl.ds(pl.multiple_of(i * KEY_TILE, KEY_TILE), KEY_TILE)
                pos = i * KEY_TILE + _iota((KEY_TILE, 2 * LANES), 0)
                lane = _iota((KEY_TILE, 2 * LANES), 1)
                in_chunk = lane - LANES == (pos & (Q_BLOCK - 1))
                in_block = lane - HEAD_DIM == _div_pow2(pos, SLC_BLOCK)
                kse_ref[hh, sl, :] = (_dot(ks_ref[0, sl, :], pick)
                                      + jnp.where(in_chunk | in_block, 1.0, 0.0)).astype(BF16)
                kwe_ref[hh, sl, :] = (_dot(kw_ref[0, sl, :], pick)
                                      + jnp.where(in_chunk, 1.0, 0.0)).astype(BF16)
                return 0

            lax.fori_loop(0, seq // KEY_TILE, build_k, 0)
            kwe_ref[hh, pl.ds(seq, Q_BLOCK), :] = jnp.where(
                _iota((Q_BLOCK, 2 * LANES), 1) == HEAD_DIM, 1.0, 0.0).astype(BF16)

    def v_chunk(ref, hh, idx):
        return _with_ones(ref[0, idx, hh * HEAD_DIM:(hh + 1) * HEAD_DIM, :])

    def v_tile(hh, kt):
        return _with_ones(jnp.concatenate(
            [vs_ref[0, kt * chunks_per_tile + k, hh * HEAD_DIM:(hh + 1) * HEAD_DIM, :]
             for k in range(chunks_per_tile)], axis=1))

    def soft(scores, m):
        return [jnp.exp2(s - m).astype(BF16) for s in scores]

    kpos = _iota((Q_BLOCK, Q_BLOCK), 0)
    tpos = _iota((Q_BLOCK, Q_BLOCK), 1)
    causal = _tile4(jnp.where(kpos <= tpos, 0.0, NEG).astype(BF16))
    recent = _tile4(jnp.where(kpos > tpos, 0.0, NEG).astype(BF16))
    pad = jnp.zeros((HEAD_DIM, cols), BF16)
    dummy_row = jnp.where(_iota((HEAD_DIM, cols), 0) == 0, NEG, 0.0).astype(BF16)
    jj = _iota((n_slc, n_cmp), 0)
    ii = _iota((n_slc, n_cmp), 1)
    ov = (jnp.minimum(ii * CMP_STRIDE + CMP_BLOCK, jj * SLC_BLOCK + SLC_BLOCK)
          - jnp.maximum(ii * CMP_STRIDE, jj * SLC_BLOCK))
    ov = (jnp.maximum(ov, 0).astype(F32) * (1.0 / CMP_BLOCK)).astype(BF16)
    c_ok = (_iota((n_cmp, Q_BLOCK), 0) * CMP_STRIDE + (CMP_BLOCK - 1)
            <= t0 + _iota((n_cmp, Q_BLOCK), 1))
    c_bias = _tile4(jnp.where(c_ok, 0.0, NEG))
    any_ok = t0 + (_iota((1, cols), 1) & (Q_BLOCK - 1)) >= CMP_BLOCK - 1
    jb = _iota((n_slc, Q_BLOCK), 0)
    tb = _div_pow2(t0 + _iota((n_slc, Q_BLOCK), 1), SLC_BLOCK)
    forced = (jb == 0) | ((jb <= tb) & (jb > tb - N_LOCAL))
    sub = _iota((8, Q_BLOCK), 0)
    n_win = WIN_KEYS // Q_BLOCK
    first = c - (n_win - 1)
    k_chunk = [jnp.where(first + i >= 0, first + i, n_chunks) for i in range(n_win)]
    v_idx = [jnp.maximum(first + i, 0) for i in range(n_win)]
    k_rows = [pl.ds(pl.multiple_of(ch * Q_BLOCK, Q_BLOCK), Q_BLOCK) for ch in k_chunk]

    def before_sweep(hh):
        qt = q_ref[0, hh * GROUP_WIDTH:(hh + 1) * GROUP_WIDTH, :]
        wq = jnp.concatenate([qt[g * HEAD_DIM:(g + 1) * HEAD_DIM, :] for g in range(GROUP)], axis=1)

        sc = _dot(kc_ref[0, hh], jnp.concatenate([wq, pad], axis=0)) + c_bias
        yield
        pc = jnp.exp2(sc - _col_max(sc))
        pc = pc * jnp.where(any_ok, 1.0 / _col_sum(pc), 0.0)
        yield
        o_cmp = _dot(vc_ref[0, hh], pc.astype(BF16))

        pcs = pc[:, 0:Q_BLOCK]
        for g in range(1, GROUP):
            pcs = pcs + pc[:, g * Q_BLOCK:(g + 1) * Q_BLOCK]
        pcs_hi = pcs.astype(BF16)
        pcs_lo = (pcs - pcs_hi.astype(F32)).astype(BF16)
        imp = _dot(ov, pcs_hi) + _dot(ov, pcs_lo)
        yield
        imp = jnp.where(forced, FORCE, imp)
        imp = jnp.where(jb <= tb, imp, NEG)
        imp_ref[hh] = imp

        w_causal = jnp.concatenate([wq, pad, causal], axis=0)
        w_mid = jnp.concatenate([wq, dummy_row], axis=0)
        w_recent = jnp.concatenate([wq, dummy_row, recent], axis=0)
        sw = [_dot(kwe_ref[hh, k_rows[0], :], w_recent)]
        sw += [_dot(kwe_ref[hh, k_rows[i], 0:LANES], w_mid) for i in range(1, n_win - 1)]
        sw += [_dot(kwe_ref[hh, k_rows[n_win - 1], :], w_causal)]
        s_d = _dot(kse_ref[hh, pl.ds(pl.multiple_of(t0, Q_BLOCK), Q_BLOCK), :], w_causal)
        yield

        n_groups = n_slc // 8
        mine = [imp[8 * r:8 * r + 8, :] for r in range(n_groups)]
        ahead_count = [jnp.zeros((8, Q_BLOCK), F32)] * n_groups
        for j in range(n_slc):
            other = imp_ref[hh, j:j + 1, :]
            for r in range(n_groups):
                if 8 * r > j:
                    ahead = other >= mine[r]
                elif 8 * r + 7 <= j:
                    ahead = other > mine[r]
                else:
                    ahead = (other > mine[r]) | ((other == mine[r]) & (sub > j - 8 * r))
                ahead_count[r] = ahead_count[r] + jnp.where(ahead, 1.0, 0.0)
            if j % 8 == 7:
                yield
        sel = (jnp.concatenate(ahead_count, axis=0) < SLC_TOPK) & (imp > 0.5 * NEG)
        sel_bias = jnp.where(sel & (jb * SLC_BLOCK < t0), 0.0, NEG).astype(BF16)
        w_sel = jnp.concatenate([wq, _tile4(sel_bias)], axis=0)

        m_w = _col_max(sw[0])
        for s in sw[1:]:
            m_w = jnp.maximum(m_w, _col_max(s))
        yield
        p_w = soft(sw, m_w)
        yield
        o_win = None
        for idx, p in zip(v_idx, p_w):
            part = _dot(v_chunk(vw_ref, hh, idx), p)
            o_win = part if o_win is None else o_win + part
        m_d = _col_max(s_d)
        acc_d = _dot(v_chunk(vs_ref, hh, c), soft([s_d], m_d)[0])
        return w_sel, o_cmp, o_win, (m_d, acc_d)

    pre = _round_robin([before_sweep(hh) for hh in heads])

    def scores(hh, kt):
        k0 = pl.multiple_of(kt * KEY_TILE, KEY_TILE)
        return _dot(kse_ref[hh, pl.ds(k0, KEY_TILE), 0:LANES], pre[hh][0])

    def sweep(items, states, lookahead=2):
        states = list(states)
        s, m_new, p = {}, {}, {}

        def issue(n):
            s[n] = scores(*items[n])

        def value_matmul(n):
            hh, kt = items[n]
            m, acc = states[hh]
            states[hh] = m_new[n], jnp.exp2(m - m_new[n]) * acc + _dot(v_tile(hh, kt), p[n])

        for n in range(min(lookahead, len(items))):
            issue(n)
        for n, (hh, _) in enumerate(items):
            m_new[n] = jnp.maximum(states[hh][0], _col_max(s[n]))
            if n + lookahead < len(items):
                issue(n + lookahead)
            p[n] = jnp.exp2(s[n] - m_new[n]).astype(BF16)
            if n >= 1:
                value_matmul(n - 1)
        value_matmul(len(items) - 1)
        return tuple(states)

    def tile_pair(i, states):
        return sweep([(hh, 2 * i + k) for k in range(2) for hh in heads], states)

    def last_tile(states):
        return sweep([(hh, n_tiles - 1) for hh in heads], states)

    n_tiles = _div_pow2(t0 + KEY_TILE - 1, KEY_TILE)
    swept = lax.fori_loop(0, _div_pow2(n_tiles, 2), tile_pair, tuple(p[3] for p in pre))
    swept = lax.cond((n_tiles & 1) == 1, last_tile, lambda states: states, swept)

    for hh in heads:
        _, o_cmp, o_win, _ = pre[hh]
        o_slc = swept[hh][1]
        gates = gate_ref[0, hh * GROUP * N_GATES:(hh + 1) * GROUP * N_GATES, :]
        inv_ls = 1.0 / o_slc[HEAD_DIM:HEAD_DIM + 1, :]
        inv_lw = 1.0 / o_win[HEAD_DIM:HEAD_DIM + 1, :]
        pieces = []
        for g in range(GROUP):
            cs = slice(g * Q_BLOCK, (g + 1) * Q_BLOCK)
            row = g * N_GATES
            pieces.append(gates[row:row + 1, :] * o_cmp[:, cs]
                          + gates[row + 1:row + 2, :] * (o_slc[0:HEAD_DIM, cs] * inv_ls[:, cs])
                          + gates[row + 2:row + 3, :] * (o_win[0:HEAD_DIM, cs] * inv_lw[:, cs]))
        o_ref[0, hh * GROUP_WIDTH:(hh + 1) * GROUP_WIDTH, :] = jnp.concatenate(pieces, axis=0)


def _nsa(q_t, gates_t, kc, vc_t, ks, kw, vs_t, vw_t):
    bsz, _, seq = q_t.shape
    n_cmp = kc.shape[2]
    hps = HEADS_PER_STEP
    assert hps == N_KV
    k_spec = pl.BlockSpec((1, seq, KV_WIDTH), lambda b, h, c: (b, 0, 0))
    v_spec = pl.BlockSpec((1, seq // Q_BLOCK, hps * HEAD_DIM, Q_BLOCK), lambda b, h, c: (b, 0, h, 0))
    return pl.pallas_call(
        functools.partial(_nsa_kernel, seq=seq),
        name="nsa",
        grid=(bsz, N_KV // hps, seq // Q_BLOCK),
        in_specs=[
            pl.BlockSpec((1, hps * GROUP_WIDTH, Q_BLOCK), lambda b, h, c: (b, h, c)),
            pl.BlockSpec((1, GATE_ROWS, Q_BLOCK), lambda b, h, c: (b, 0, c)),
            pl.BlockSpec((1, hps, n_cmp, LANES), lambda b, h, c: (b, h, 0, 0)),
            pl.BlockSpec((1, hps, HEAD_DIM, n_cmp), lambda b, h, c: (b, h, 0, 0)),
            k_spec, k_spec, v_spec, v_spec,
        ],
        out_specs=pl.BlockSpec((1, hps * GROUP_WIDTH, Q_BLOCK), lambda b, h, c: (b, h, c)),
        out_shape=jax.ShapeDtypeStruct((bsz, ATTN_WIDTH, seq), F32),
        scratch_shapes=[
            pltpu.VMEM((hps, seq, 2 * LANES), BF16),
            pltpu.VMEM((hps, seq + Q_BLOCK, 2 * LANES), BF16),
            pltpu.VMEM((hps, seq // SLC_BLOCK, Q_BLOCK), F32),
        ],
        compiler_params=pltpu.CompilerParams(
            dimension_semantics=("parallel", "parallel", "arbitrary"),
            vmem_limit_bytes=VMEM_LIMIT),
    )(q_t, gates_t, kc, vc_t, ks, kw, vs_t, vw_t)


def _rope_tables(seq):
    pos = jnp.arange(seq, dtype=F32)
    inv = ROPE_THETA ** (-jnp.arange(0, HEAD_DIM, 2, dtype=F32) / HEAD_DIM)
    ang = pos[:, None] * inv[None, :]
    cos, sin = jnp.cos(ang), jnp.sin(ang)
    reps = LANES // HEAD_DIM
    return (jnp.tile(jnp.concatenate([cos, cos], axis=1), (1, reps)),
            jnp.tile(jnp.concatenate([-sin, sin], axis=1), (1, reps)))


def _compress_params(w1, w2, pe):
    eye = jnp.eye(HEADS_PER_TILE, dtype=w1.dtype)
    w1 = jnp.einsum('ldf,hk->lhdkf', w1, eye).reshape(CMP_BLOCK, LANES, HEADS_PER_TILE * CMP_HIDDEN)
    pe = jnp.tile(pe, (1, HEADS_PER_TILE)).reshape(CMP_BLOCK, 1, LANES)
    return pe, w1.astype(BF16), w2.astype(BF16)


def kernel(x, ffn1_norm, ffn1_wg, ffn1_wu, ffn1_wd, mix_norm, w_in, conv_w, cmp_k_w1, cmp_k_w2,
           cmp_k_pe, cmp_v_w1, cmp_v_w2, cmp_v_pe, conv_out_norm, attn_out_norm, w_out, ffn2_norm,
           ffn2_wg, ffn2_wu, ffn2_wd, final_norm):
    bsz, seq, d = x.shape
    assert CMP_BLOCK == 2 * CMP_STRIDE and seq % TOKEN_TILE == 0 and seq % KEY_TILE == 0
    assert x.shape[0] * seq % TOKEN_TILE == 0 and ffn1_wg.shape[0] == 1
    xt = x.reshape(bsz * seq, d)
    cos, sin = _rope_tables(seq)

    assert w_in.shape[1:] == (D_MODEL, IN_WIDTH)
    w = jnp.pad(w_in[0].astype(BF16), ((0, 0), (0, IN_WIDTH_PADDED - IN_WIDTH)))
    w_vt = jnp.concatenate([w[:, _VS0:_VS0 + KV_WIDTH], w[:, _VW0:_VW0 + KV_WIDTH]], axis=1).T

    x1, wg2, wu2, wd2, wo = _ffn("ffn1", xt, ffn1_norm, ffn1_wg[0].astype(BF16),
                                 ffn1_wu[0].astype(BF16), ffn1_wd[0].astype(BF16),
                                 later_weights=[ffn2_wg[0], ffn2_wu[0], ffn2_wd[0], w_out[0]])
    yc = _conv_proj(x1, mix_norm, w, conv_w[0], conv_out_norm, seq)
    q_t, k_c, v_c, k_s, k_w, vs_t, vw_t, gates_t = _attn_proj(x1, mix_norm, w, w_vt, cos, sin, seq)

    pe_k, w1_k, w2_k = _compress_params(cmp_k_w1[0], cmp_k_w2[0], cmp_k_pe[0])
    pe_v, w1_v, w2_v = _compress_params(cmp_v_w1[0], cmp_v_w2[0], cmp_v_pe[0])
    kc = _compress(k_c, pe_k, w1_k, jnp.pad(w2_k, ((0, 0), (0, LANES - HEAD_DIM))), seq,
                   transposed=False)
    vc_t = _compress(v_c, pe_v, w1_v, w2_v.T, seq, transposed=True)

    per_batch = lambda a: a.reshape(bsz, seq, a.shape[-1])
    ya_t = _nsa(q_t, gates_t, kc, vc_t, per_batch(k_s), per_batch(k_w), vs_t, vw_t)

    x2 = _out_proj(x1, yc, ya_t, attn_out_norm, wo)
    (out,) = _ffn("ffn2", x2, ffn2_norm, wg2, wu2, wd2, final_gain=final_norm.reshape(1, d))
    return out.reshape(bsz, seq, d)
```

```python
import functools

import jax
import jax.numpy as jnp
from jax import lax
from jax.experimental import pallas as pl
from jax.experimental.pallas import tpu as pltpu

F32 = jnp.float32
BF16 = jnp.bfloat16

D_MODEL = 2048
D_FF = 5632
CONV_WIDTH = 1024
N_HEADS = 16
N_KV = 4
GROUP = N_HEADS // N_KV
HEAD_DIM = 64
ATTN_WIDTH = N_HEADS * HEAD_DIM
KV_WIDTH = N_KV * HEAD_DIM
CMP_BLOCK = 32
CMP_STRIDE = 16
CMP_HIDDEN = 256
SLC_BLOCK = 64
SLC_TOPK = 16
N_LOCAL = 2
WINDOW = 512
Q_BLOCK = 128
N_GATES = 3
ROPE_THETA = 10000.0
EPS = 1e-6
NEG = -1e30
FORCE = 1e9

LANES = 128
GROUP_WIDTH = GROUP * HEAD_DIM
GATE_ROWS = LANES
TOKEN_TILE = 512
FFN_TOKEN_TILE = 1024
FF_TILE = 512
KEY_TILE = 256
WIN_KEYS = WINDOW + Q_BLOCK
ONES_ROWS = 16
HEADS_PER_STEP = 4
VMEM_LIMIT = 56 * 1024 * 1024
FFN_VMEM_LIMIT = 62 * 1024 * 1024
Q_SCALE = HEAD_DIM ** -0.5 * 1.4426950408889634


def _rms(x):
    return x * lax.rsqrt(jnp.mean(x * x, axis=-1, keepdims=True) + EPS)


def _dot(a, b):
    return jnp.dot(a, b, preferred_element_type=F32)


def _dot_nt(a, b):
    return lax.dot_general(a, b, (((1,), (1,)), ((), ())), preferred_element_type=F32)


def _div_pow2(x, n):
    assert n & (n - 1) == 0
    return jnp.right_shift(x, n.bit_length() - 1)


def _iota(shape, dim):
    return lax.broadcasted_iota(jnp.int32, shape, dim)


def _resident(shape):
    return pl.BlockSpec(shape, lambda *_: (0,) * len(shape), pipeline_mode=pl.Buffered(1))


def _round_robin(gens, skew=1):
    results = [None] * len(gens)
    live = {}
    waiting = dict(enumerate(gens))
    rnd = 0
    while waiting or live:
        for k in [k for k in waiting if rnd >= k * skew]:
            live[k] = waiting.pop(k)
        for k in sorted(live):
            try:
                next(live[k])
            except StopIteration as done:
                results[k] = done.value
                del live[k]
        rnd += 1
    return results


def _ffn_sweep(make_h, finish, wg_ref, wu_ref, wd_ref, h_ref, acc_ref):
    j = pl.program_id(1)
    last = pl.num_programs(1) - 1
    tm = h_ref.shape[0]
    halves = [slice(0, tm // 2), slice(tm // 2, tm)]

    def hidden(h):
        a = _dot(h, wg_ref[...])
        u = _dot(h, wu_ref[...])
        return (jax.nn.silu(a) * u).astype(BF16)

    @pl.when(j == 0)
    def _():
        def first(rows):
            h = yield from make_h(rows)
            h_ref[rows, :] = h
            yield
            act = hidden(h)
            yield
            acc_ref[rows, :] = _dot(act, wd_ref[...])

        _round_robin([first(rows) for rows in halves])

    @pl.when((j > 0) & (j < last))
    def _():
        acc_ref[...] += _dot(hidden(h_ref[...]), wd_ref[...])

    @pl.when(j == last)
    def _():
        def final(rows):
            act = hidden(h_ref[rows, :])
            yield
            finish(rows, acc_ref[rows, :] + _dot(act, wd_ref[...]))

        _round_robin([final(rows) for rows in halves])


def _ffn_kernel(x_ref, g_ref, wg_ref, wu_ref, wd_ref, *refs, n_cast, final_norm):
    refs = list(refs)
    fin_ref = refs.pop(0) if final_norm else None
    cast_in = refs[:n_cast]
    o_ref = refs[n_cast]
    cast_out = refs[n_cast + 1:2 * n_cast + 1]
    (h_ref,) = refs[2 * n_cast + 1:]
    acc_ref = o_ref
    for src, dst in zip(cast_in, cast_out):
        dst[...] = src[...].astype(BF16)

    def make_h(rows):
        return (_rms(x_ref[rows, :]) * g_ref[...]).astype(BF16)
        yield

    def finish(rows, acc):
        y = x_ref[rows, :] + 0.5 * acc
        o_ref[rows, :] = _rms(y) * fin_ref[...] if final_norm else y

    _ffn_sweep(make_h, finish, wg_ref, wu_ref, wd_ref, h_ref, acc_ref)


def _cast_block(shape, ni, nj):
    rows, cols = shape
    sub, lane = 16, LANES
    if rows % ni == 0 and cols % nj == 0 and (rows // ni) % sub == 0 and (cols // nj) % lane == 0:
        return pl.BlockSpec((rows // ni, cols // nj), lambda i, j: (i, j))
    if rows % (ni * nj) == 0 and (rows // (ni * nj)) % sub == 0:
        return pl.BlockSpec((rows // (ni * nj), cols), lambda i, j: (i * nj + j, 0))
    assert rows % ni == 0 and (rows // ni) % sub == 0
    return pl.BlockSpec((rows // ni, cols), lambda i, j: (i, 0))


def _ffn(name, x, g, wg, wu, wd, later_weights=(), final_gain=None):
    t, d = x.shape
    f = wg.shape[1]
    tm = FFN_TOKEN_TILE
    grid = (t // tm, f // FF_TILE)
    cast_specs = [_cast_block(w.shape, *grid) for w in later_weights]
    gain = lambda: pl.BlockSpec((1, d), lambda i, j: (0, 0))
    final = [] if final_gain is None else [final_gain]
    return pl.pallas_call(
        functools.partial(_ffn_kernel, n_cast=len(later_weights), final_norm=bool(final)),
        name=name,
        grid=grid,
        in_specs=[
            pl.BlockSpec((tm, d), lambda i, j: (i, 0)),
            gain(),
            pl.BlockSpec((d, FF_TILE), lambda i, j: (0, j)),
            pl.BlockSpec((d, FF_TILE), lambda i, j: (0, j)),
            pl.BlockSpec((FF_TILE, d), lambda i, j: (j, 0)),
        ] + [gain() for _ in final] + cast_specs,
        out_specs=[pl.BlockSpec((tm, d), lambda i, j: (i, 0))] + cast_specs,
        out_shape=[jax.ShapeDtypeStruct((t, d), F32)] + [
            jax.ShapeDtypeStruct(w.shape, BF16) for w in later_weights],
        scratch_shapes=[pltpu.VMEM((tm, d), BF16)],
        compiler_params=pltpu.CompilerParams(
            dimension_semantics=("arbitrary", "arbitrary"), vmem_limit_bytes=FFN_VMEM_LIMIT),
    )(x, g, wg, wu, wd, *final, *later_weights)


def _out_proj_kernel(x_ref, yc_ref, ya_ref, ga_ref, wo_ref, o_ref):
    tm = x_ref.shape[0]

    def one_half(rows):
        ya = (_rms(ya_ref[0, :, rows].T) * ga_ref[...]).astype(BF16)
        yield
        mix = _dot(yc_ref[rows, :], wo_ref[0:CONV_WIDTH, :]) + _dot(ya, wo_ref[CONV_WIDTH:, :])
        yield
        o_ref[rows, :] = x_ref[rows, :] + mix

    _round_robin([one_half(slice(0, tm // 2)), one_half(slice(tm // 2, tm))])


def _out_proj(x1, yc, ya_t, ga, wo):
    t, d = x1.shape
    tm = TOKEN_TILE
    tps = ya_t.shape[2] // tm
    return pl.pallas_call(
        _out_proj_kernel,
        name="out_proj",
        grid=(t // tm,),
        in_specs=[
            pl.BlockSpec((tm, d), lambda i: (i, 0)),
            pl.BlockSpec((tm, CONV_WIDTH), lambda i: (i, 0)),
            pl.BlockSpec((1, ATTN_WIDTH, tm), lambda i: (i // tps, 0, i % tps)),
            pl.BlockSpec((1, ATTN_WIDTH), lambda i: (0, 0)),
            _resident(wo.shape),
        ],
        out_specs=pl.BlockSpec((tm, d), lambda i: (i, 0)),
        out_shape=jax.ShapeDtypeStruct((t, d), F32),
        compiler_params=pltpu.CompilerParams(
            dimension_semantics=("parallel",), vmem_limit_bytes=VMEM_LIMIT),
    )(x1, yc, ya_t, ga, wo)


def _conv_proj_kernel(x_ref, g_ref, w_ref, cw_ref, gc_ref, o_ref, carry_ref, *, tiles_per_seq):
    i = pl.program_id(0)

    @pl.when(i % tiles_per_seq == 0)
    def _():
        carry_ref[...] = jnp.zeros_like(carry_ref)

    tm = x_ref.shape[0]
    half = tm // 2
    tails = {-1: carry_ref[...]}

    def one_half(k):
        rows = slice(k * half, (k + 1) * half)
        h = (_rms(x_ref[rows, :]) * g_ref[...]).astype(BF16)
        yield
        c_h = _dot(h, w_ref[:, 0:CONV_WIDTH])
        c_c = _dot(h, w_ref[:, 2 * CONV_WIDTH:3 * CONV_WIDTH])
        u = c_c * c_h
        tails[k] = u[half - 8:half, :]
        c_b = _dot(h, w_ref[:, CONV_WIDTH:2 * CONV_WIDTH])
        yield
        row = lax.broadcasted_iota(jnp.int32, u.shape, 0)
        prev1 = tails[k - 1][7:8, :]
        prev2 = tails[k - 1][6:7, :]
        u1 = jnp.where(row == 0, prev1, pltpu.roll(u, 1, 0))
        u2 = jnp.where(row == 0, prev2, jnp.where(row == 1, prev1, pltpu.roll(u, 2, 0)))
        y = c_b * (cw_ref[0:1, :] * u2 + cw_ref[1:2, :] * u1 + cw_ref[2:3, :] * u)
        o_ref[rows, :] = (_rms(y) * gc_ref[...]).astype(BF16)

    _round_robin([one_half(0), one_half(1)])
    carry_ref[...] = tails[1]


def _conv_proj(x1, g, w_in, conv_w, gc, seq):
    t, d = x1.shape
    tm = TOKEN_TILE
    return pl.pallas_call(
        functools.partial(_conv_proj_kernel, tiles_per_seq=seq // tm),
        name="conv_proj",
        grid=(t // tm,),
        in_specs=[
            pl.BlockSpec((tm, d), lambda i: (i, 0)),
            pl.BlockSpec((1, d), lambda i: (0, 0)),
            _w_in_window(_CONV0, 3 * CONV_WIDTH),
            pl.BlockSpec(conv_w.shape, lambda i: (0, 0)),
            pl.BlockSpec((1, CONV_WIDTH), lambda i: (0, 0)),
        ],
        out_specs=pl.BlockSpec((tm, CONV_WIDTH), lambda i: (i, 0)),
        out_shape=jax.ShapeDtypeStruct((t, CONV_WIDTH), BF16),
        scratch_shapes=[pltpu.VMEM((8, CONV_WIDTH), F32)],
        compiler_params=pltpu.CompilerParams(
            dimension_semantics=("arbitrary",), vmem_limit_bytes=VMEM_LIMIT),
    )(x1, g, w_in, conv_w, gc)


def _rope(x, cos, sin_signed, first_half):
    outs = []
    for k in range(x.shape[1] // LANES):
        xc = x[:, k * LANES:(k + 1) * LANES]
        partner = jnp.where(first_half, pltpu.roll(xc, LANES - HEAD_DIM // 2, 1),
                            pltpu.roll(xc, HEAD_DIM // 2, 1))
        outs.append(xc * cos + partner * sin_signed)
    return outs


_CONV0 = 0
_Q0 = 3 * CONV_WIDTH
_KC0 = _Q0 + ATTN_WIDTH
_VC0 = _KC0 + KV_WIDTH
_KS0 = _VC0 + KV_WIDTH
_VS0 = _KS0 + KV_WIDTH
_KW0 = _VS0 + KV_WIDTH
_VW0 = _KW0 + KV_WIDTH
_G0 = _VW0 + KV_WIDTH
IN_WIDTH = _G0 + N_HEADS * N_GATES
IN_WIDTH_PADDED = -(-IN_WIDTH // LANES) * LANES


def _w_in_window(c0, width):
    assert c0 % width == 0
    return pl.BlockSpec((D_MODEL, width), lambda *_: (0, c0 // width), pipeline_mode=pl.Buffered(1))


def _attn_proj_kernel(x_ref, g_ref, wq_ref, wkc_ref, wvc_ref, wks_ref, wkw_ref, wgate_ref, wvt_ref,
                      cos_ref, sin_ref,
                      q_ref, kc_ref, vc_ref, ks_ref, kw_ref, vs_ref, vw_ref, gate_ref):
    tm = x_ref.shape[0]
    half = tm // 2
    lane = lax.broadcasted_iota(jnp.int32, (half, LANES), 1)
    first_half = (lane & (HEAD_DIM - 1)) < HEAD_DIM // 2

    def one_half(k):
        rows = slice(k * half, (k + 1) * half)
        h = (_rms(x_ref[rows, :]) * g_ref[...]).astype(BF16)
        cos = cos_ref[rows, :]
        sin = sin_ref[rows, :]

        def proj(w_ref):
            return _dot(h, w_ref[...])

        def rope_to(ref, w_ref, dtype):
            for n, piece in enumerate(_rope(proj(w_ref), cos, sin, first_half)):
                ref[rows, n * LANES:(n + 1) * LANES] = piece.astype(dtype)

        yield
        for n, piece in enumerate(_rope(proj(wq_ref), cos, sin, first_half)):
            q_ref[0, n * LANES:(n + 1) * LANES, rows] = (piece * Q_SCALE).T.astype(BF16)
        yield
        rope_to(kc_ref, wkc_ref, F32)
        vc_ref[rows, :] = proj(wvc_ref)
        rope_to(ks_ref, wks_ref, BF16)
        rope_to(kw_ref, wkw_ref, BF16)
        yield
        gate_ref[0, :, rows] = jax.nn.sigmoid(proj(wgate_ref)).T
        vt = _dot_nt(wvt_ref[...], h)
        chunks = half // Q_BLOCK
        for n in range(chunks):
            cs = slice(n * Q_BLOCK, (n + 1) * Q_BLOCK)
            vs_ref[0, k * chunks + n] = vt[0:KV_WIDTH, cs].astype(BF16)
            vw_ref[0, k * chunks + n] = vt[KV_WIDTH:2 * KV_WIDTH, cs].astype(BF16)

    _round_robin([one_half(0), one_half(1)])


def _attn_proj(x1, g, w_in, w_vt, cos, sin, seq):
    t, d = x1.shape
    tm = TOKEN_TILE
    tps = seq // tm
    bsz = t // seq
    cpt = tm // Q_BLOCK
    row = lambda w: pl.BlockSpec((tm, w), lambda i: (i, 0))
    col = lambda w: pl.BlockSpec((1, w, tm), lambda i: (i // tps, 0, i % tps))
    chunked = pl.BlockSpec((1, cpt, KV_WIDTH, Q_BLOCK), lambda i: (i // tps, i % tps, 0, 0))
    kv_dtypes = [F32, F32, BF16, BF16]
    out_specs = [col(ATTN_WIDTH)] + [row(KV_WIDTH)] * 4 + [chunked, chunked, col(GATE_ROWS)]
    out_shape = ([jax.ShapeDtypeStruct((bsz, ATTN_WIDTH, seq), BF16)]
                 + [jax.ShapeDtypeStruct((t, KV_WIDTH), dt) for dt in kv_dtypes]
                 + [jax.ShapeDtypeStruct((bsz, seq // Q_BLOCK, KV_WIDTH, Q_BLOCK), BF16)] * 2
                 + [jax.ShapeDtypeStruct((bsz, GATE_ROWS, seq), F32)])
    windows = [_w_in_window(_Q0, ATTN_WIDTH)] + [
        _w_in_window(c0, KV_WIDTH) for c0 in (_KC0, _VC0, _KS0, _KW0)] + [
        _w_in_window(_G0, GATE_ROWS)]
    return pl.pallas_call(
        _attn_proj_kernel,
        name="attn_proj",
        grid=(t // tm,),
        in_specs=[row(d), pl.BlockSpec((1, d), lambda i: (0, 0))] + windows + [
            _resident(w_vt.shape),
            pl.BlockSpec((tm, LANES), lambda i: (i % tps, 0)),
            pl.BlockSpec((tm, LANES), lambda i: (i % tps, 0)),
        ],
        out_specs=out_specs,
        out_shape=out_shape,
        compiler_params=pltpu.CompilerParams(
            dimension_semantics=("parallel",), vmem_limit_bytes=VMEM_LIMIT),
    )(x1, g, *([w_in] * len(windows)), w_vt, cos, sin)


HEADS_PER_TILE = LANES // HEAD_DIM


def _compress_kernel(*refs, transposed):
    tok_refs = refs[:N_KV // HEADS_PER_TILE]
    pe_ref, w1_ref, w2_ref, o_ref = refs[N_KV // HEADS_PER_TILE:]
    n = tok_refs[0].shape[0] // CMP_STRIDE
    for t, tok_ref in enumerate(tok_refs):
        first = jnp.zeros((n, HEADS_PER_TILE * CMP_HIDDEN), F32)
        second = jnp.zeros((n, HEADS_PER_TILE * CMP_HIDDEN), F32)
        for l in range(CMP_STRIDE):
            rows = tok_ref[pl.ds(l, n, stride=CMP_STRIDE), :]
            first = first + _dot((rows + pe_ref[l]).astype(BF16), w1_ref[l])
            second = second + _dot((rows + pe_ref[l + CMP_STRIDE]).astype(BF16),
                                   w1_ref[l + CMP_STRIDE])
        hid = jax.nn.gelu(first + pltpu.roll(second, n - 1, 0)).astype(BF16)
        for k in range(HEADS_PER_TILE):
            hid_h = hid[:, k * CMP_HIDDEN:(k + 1) * CMP_HIDDEN]
            h = t * HEADS_PER_TILE + k
            if transposed:
                o_ref[0, h] = _dot_nt(w2_ref[...], hid_h).astype(BF16)
            else:
                o_ref[0, h] = _dot(hid_h, w2_ref[...]).astype(BF16)


def _compress(tok, pe, w1, w2, seq, transposed):
    bsz = tok.shape[0] // seq
    n = seq // CMP_STRIDE
    out_block = (1, N_KV, HEAD_DIM, n) if transposed else (1, N_KV, n, LANES)
    lane_tile = lambda t: pl.BlockSpec((seq, LANES), lambda b: (b, t))
    return pl.pallas_call(
        functools.partial(_compress_kernel, transposed=transposed),
        name="compress_v" if transposed else "compress_k",
        grid=(bsz,),
        in_specs=[lane_tile(t) for t in range(N_KV // HEADS_PER_TILE)] + [
            _resident(pe.shape),
            _resident(w1.shape),
            _resident(w2.shape),
        ],
        out_specs=pl.BlockSpec(out_block, lambda b: (b, 0, 0, 0)),
        out_shape=jax.ShapeDtypeStruct((bsz,) + out_block[1:], BF16),
        compiler_params=pltpu.CompilerParams(
            dimension_semantics=("parallel",), vmem_limit_bytes=VMEM_LIMIT),
    )(*([tok] * (N_KV // HEADS_PER_TILE)), pe, w1, w2)


def _tile4(x):
    return jnp.concatenate([x] * GROUP, axis=1)


def _col_max(x):
    return jnp.max(x, axis=0, keepdims=True)


def _col_sum(x):
    return jnp.sum(x, axis=0, keepdims=True)


def _with_ones(v):
    ones = jnp.where(_iota((ONES_ROWS, v.shape[1]), 0) == 0, 1.0, 0.0).astype(v.dtype)
    return jnp.concatenate([v, ones], axis=0)


def _nsa_kernel(q_ref, gate_ref, kc_ref, vc_ref, ks_ref, kw_ref, vs_ref, vw_ref, o_ref,
                kse_ref, kwe_ref, imp_ref, *, seq):
    hp = pl.program_id(1)
    c = pl.program_id(2)
    t0 = c * Q_BLOCK
    n_cmp = kc_ref.shape[2]
    n_slc = seq // SLC_BLOCK
    cols = GROUP * Q_BLOCK
    n_chunks = seq // Q_BLOCK
    chunks_per_tile = KEY_TILE // Q_BLOCK
    heads = range(HEADS_PER_STEP)
    assert n_slc + HEAD_DIM == LANES and n_slc % 8 == 0

    @pl.when(c == 0)
    def _():
        r = _iota((KV_WIDTH, 2 * LANES), 0)
        ln = _iota((KV_WIDTH, 2 * LANES), 1)
        for hh in heads:
            h = hp * HEADS_PER_STEP + hh
            pick = ((r == h * HEAD_DIM + ln) & (ln < HEAD_DIM)).astype(BF16)

            def build_k(i, _, hh=hh, pick=pick):
                sl = pl.ds(pl.multiple_of(i * KEY_TILE, KEY_TILE), KEY_TILE)
                pos = i * KEY_TILE + _iota((KEY_TILE, 2 * LANES), 0)
                lane = _iota((KEY_TILE, 2 * LANES), 1)
                in_chunk = lane - LANES == (pos & (Q_BLOCK - 1))
                in_block = lane - HEAD_DIM == _div_pow2(pos, SLC_BLOCK)
                kse_ref[hh, sl, :] = (_dot(ks_ref[0, sl, :], pick)
                                      + jnp.where(in_chunk | in_block, 1.0, 0.0)).astype(BF16)
                kwe_ref[hh, sl, :] = (_dot(kw_ref[0, sl, :], pick)
                                      + jnp.where(in_chunk, 1.0, 0.0)).astype(BF16)
                return 0

            lax.fori_loop(0, seq // KEY_TILE, build_k, 0)
            kwe_ref[hh, pl.ds(seq, Q_BLOCK), :] = jnp.where(
                _iota((Q_BLOCK, 2 * LANES), 1) == HEAD_DIM, 1.0, 0.0).astype(BF16)

    def v_chunk(ref, hh, idx):
        return _with_ones(ref[0, idx, hh * HEAD_DIM:(hh + 1) * HEAD_DIM, :])

    def v_tile(hh, kt):
        return _with_ones(jnp.concatenate(
            [vs_ref[0, kt * chunks_per_tile + k, hh * HEAD_DIM:(hh + 1) * HEAD_DIM, :]
             for k in range(chunks_per_tile)], axis=1))

    def soft(scores, m):
        return [jnp.exp2(s - m).astype(BF16) for s in scores]

    kpos = _iota((Q_BLOCK, Q_BLOCK), 0)
    tpos = _iota((Q_BLOCK, Q_BLOCK), 1)
    causal = _tile4(jnp.where(kpos <= tpos, 0.0, NEG).astype(BF16))
    recent = _tile4(jnp.where(kpos > tpos, 0.0, NEG).astype(BF16))
    pad = jnp.zeros((HEAD_DIM, cols), BF16)
    dummy_row = jnp.where(_iota((HEAD_DIM, cols), 0) == 0, NEG, 0.0).astype(BF16)
    jj = _iota((n_slc, n_cmp), 0)
    ii = _iota((n_slc, n_cmp), 1)
    ov = (jnp.minimum(ii * CMP_STRIDE + CMP_BLOCK, jj * SLC_BLOCK + SLC_BLOCK)
          - jnp.maximum(ii * CMP_STRIDE, jj * SLC_BLOCK))
    ov = (jnp.maximum(ov, 0).astype(F32) * (1.0 / CMP_BLOCK)).astype(BF16)
    c_ok = (_iota((n_cmp, Q_BLOCK), 0) * CMP_STRIDE + (CMP_BLOCK - 1)
            <= t0 + _iota((n_cmp, Q_BLOCK), 1))
    c_bias = _tile4(jnp.where(c_ok, 0.0, NEG))
    any_ok = t0 + (_iota((1, cols), 1) & (Q_BLOCK - 1)) >= CMP_BLOCK - 1
    jb = _iota((n_slc, Q_BLOCK), 0)
    tb = _div_pow2(t0 + _iota((n_slc, Q_BLOCK), 1), SLC_BLOCK)
    forced = (jb == 0) | ((jb <= tb) & (jb > tb - N_LOCAL))
    sub = _iota((8, Q_BLOCK), 0)
    n_win = WIN_KEYS // Q_BLOCK
    first = c - (n_win - 1)
    k_chunk = [jnp.where(first + i >= 0, first + i, n_chunks) for i in range(n_win)]
    v_idx = [jnp.maximum(first + i, 0) for i in range(n_win)]
    k_rows = [pl.ds(pl.multiple_of(ch * Q_BLOCK, Q_BLOCK), Q_BLOCK) for ch in k_chunk]

    def before_sweep(hh):
        qt = q_ref[0, hh * GROUP_WIDTH:(hh + 1) * GROUP_WIDTH, :]
        wq = jnp.concatenate([qt[g * HEAD_DIM:(g + 1) * HEAD_DIM, :] for g in range(GROUP)], axis=1)

        sc = _dot(kc_ref[0, hh], jnp.concatenate([wq, pad], axis=0)) + c_bias
        yield
        pc = jnp.exp2(sc - _col_max(sc))
        pc = pc * jnp.where(any_ok, 1.0 / _col_sum(pc), 0.0)
        yield
        o_cmp = _dot(vc_ref[0, hh], pc.astype(BF16))

        pcs = pc[:, 0:Q_BLOCK]
        for g in range(1, GROUP):
            pcs = pcs + pc[:, g * Q_BLOCK:(g + 1) * Q_BLOCK]
        pcs_hi = pcs.astype(BF16)
        pcs_lo = (pcs - pcs_hi.astype(F32)).astype(BF16)
        imp = _dot(ov, pcs_hi) + _dot(ov, pcs_lo)
        yield
        imp = jnp.where(forced, FORCE, imp)
        imp = jnp.where(jb <= tb, imp, NEG)
        imp_ref[hh] = imp

        w_causal = jnp.concatenate([wq, pad, causal], axis=0)
        w_mid = jnp.concatenate([wq, dummy_row], axis=0)
        w_recent = jnp.concatenate([wq, dummy_row, recent], axis=0)
        sw = [_dot(kwe_ref[hh, k_rows[0], :], w_recent)]
        sw += [_dot(kwe_ref[hh, k_rows[i], 0:LANES], w_mid) for i in range(1, n_win - 1)]
        sw += [_dot(kwe_ref[hh, k_rows[n_win - 1], :], w_causal)]
        s_d = _dot(kse_ref[hh, pl.ds(pl.multiple_of(t0, Q_BLOCK), Q_BLOCK), :], w_causal)
        yield

        n_groups = n_slc // 8
        mine = [imp[8 * r:8 * r + 8, :] for r in range(n_groups)]
        ahead_count = [jnp.zeros((8, Q_BLOCK), F32)] * n_groups
        for j in range(n_slc):
            other = imp_ref[hh, j:j + 1, :]
            for r in range(n_groups):
                if 8 * r > j:
                    ahead = other >= mine[r]
                elif 8 * r + 7 <= j:
                    ahead = other > mine[r]
                else:
                    ahead = (other > mine[r]) | ((other == mine[r]) & (sub > j - 8 * r))
                ahead_count[r] = ahead_count[r] + jnp.where(ahead, 1.0, 0.0)
            if j % 8 == 7:
                yield
        sel = (jnp.concatenate(ahead_count, axis=0) < SLC_TOPK) & (imp > 0.5 * NEG)
        sel_bias = jnp.where(sel & (jb * SLC_BLOCK < t0), 0.0, NEG).astype(BF16)
        w_sel = jnp.concatenate([wq, _tile4(sel_bias)], axis=0)

        m_w = _col_max(sw[0])
        for s in sw[1:]:
            m_w = jnp.maximum(m_w, _col_max(s))
        yield
        p_w = soft(sw, m_w)
        yield
        o_win = None
        for idx, p in zip(v_idx, p_w):
            part = _dot(v_chunk(vw_ref, hh, idx), p)
            o_win = part if o_win is None else o_win + part
        m_d = _col_max(s_d)
        acc_d = _dot(v_chunk(vs_ref, hh, c), soft([s_d], m_d)[0])
        return w_sel, o_cmp, o_win, (m_d, acc_d)

    pre = _round_robin([before_sweep(hh) for hh in heads])

    def scores(hh, kt):
        k0 = pl.multiple_of(kt * KEY_TILE, KEY_TILE)
        return _dot(kse_ref[hh, pl.ds(k0, KEY_TILE), 0:LANES], pre[hh][0])

    def sweep(items, states, lookahead=2):
        states = list(states)
        s, m_new, p = {}, {}, {}

        def issue(n):
            s[n] = scores(*items[n])

        def value_matmul(n):
            hh, kt = items[n]
            m, acc = states[hh]
            states[hh] = m_new[n], jnp.exp2(m - m_new[n]) * acc + _dot(v_tile(hh, kt), p[n])

        for n in range(min(lookahead, len(items))):
            issue(n)
        for n, (hh, _) in enumerate(items):
            m_new[n] = jnp.maximum(states[hh][0], _col_max(s[n]))
            if n + lookahead < len(items):
                issue(n + lookahead)
            p[n] = jnp.exp2(s[n] - m_new[n]).astype(BF16)
            if n >= 1:
                value_matmul(n - 1)
        value_matmul(len(items) - 1)
        return tuple(states)

    def tiles(first, count):
        return lambda states: sweep([(hh, first + k) for k in range(count) for hh in heads], states)

    n_tiles = _div_pow2(t0 + KEY_TILE - 1, KEY_TILE)
    keep = lambda states: states
    swept = lax.fori_loop(0, _div_pow2(n_tiles, 4), lambda i, st: tiles(4 * i, 4)(st),
                          tuple(p[3] for p in pre))
    swept = lax.cond((n_tiles & 2) == 2, tiles(n_tiles & ~3, 2), keep, swept)
    swept = lax.cond((n_tiles & 1) == 1, tiles(n_tiles - 1, 1), keep, swept)

    for hh in heads:
        _, o_cmp, o_win, _ = pre[hh]
        o_slc = swept[hh][1]
        gates = gate_ref[0, hh * GROUP * N_GATES:(hh + 1) * GROUP * N_GATES, :]
        inv_ls = 1.0 / o_slc[HEAD_DIM:HEAD_DIM + 1, :]
        inv_lw = 1.0 / o_win[HEAD_DIM:HEAD_DIM + 1, :]
        pieces = []
        for g in range(GROUP):
            cs = slice(g * Q_BLOCK, (g + 1) * Q_BLOCK)
            row = g * N_GATES
            pieces.append(gates[row:row + 1, :] * o_cmp[:, cs]
                          + gates[row + 1:row + 2, :] * (o_slc[0:HEAD_DIM, cs] * inv_ls[:, cs])
                          + gates[row + 2:row + 3, :] * (o_win[0:HEAD_DIM, cs] * inv_lw[:, cs]))
        o_ref[0, hh * GROUP_WIDTH:(hh + 1) * GROUP_WIDTH, :] = jnp.concatenate(pieces, axis=0)


def _nsa(q_t, gates_t, kc, vc_t, ks, kw, vs_t, vw_t):
    bsz, _, seq = q_t.shape
    n_cmp = kc.shape[2]
    hps = HEADS_PER_STEP
    assert hps == N_KV
    k_spec = pl.BlockSpec((1, seq, KV_WIDTH), lambda b, h, c: (b, 0, 0))
    v_spec = pl.BlockSpec((1, seq // Q_BLOCK, hps * HEAD_DIM, Q_BLOCK), lambda b, h, c: (b, 0, h, 0))
    return pl.pallas_call(
        functools.partial(_nsa_kernel, seq=seq),
        name="nsa",
        grid=(bsz, N_KV // hps, seq // Q_BLOCK),
        in_specs=[
            pl.BlockSpec((1, hps * GROUP_WIDTH, Q_BLOCK), lambda b, h, c: (b, h, c)),
            pl.BlockSpec((1, GATE_ROWS, Q_BLOCK), lambda b, h, c: (b, 0, c)),
            pl.BlockSpec((1, hps, n_cmp, LANES), lambda b, h, c: (b, h, 0, 0)),
            pl.BlockSpec((1, hps, HEAD_DIM, n_cmp), lambda b, h, c: (b, h, 0, 0)),
            k_spec, k_spec, v_spec, v_spec,
        ],
        out_specs=pl.BlockSpec((1, hps * GROUP_WIDTH, Q_BLOCK), lambda b, h, c: (b, h, c)),
        out_shape=jax.ShapeDtypeStruct((bsz, ATTN_WIDTH, seq), F32),
        scratch_shapes=[
            pltpu.VMEM((hps, seq, 2 * LANES), BF16),
            pltpu.VMEM((hps, seq + Q_BLOCK, 2 * LANES), BF16),
            pltpu.VMEM((hps, seq // SLC_BLOCK, Q_BLOCK), F32),
        ],
        compiler_params=pltpu.CompilerParams(
            dimension_semantics=("parallel", "parallel", "arbitrary"),
            vmem_limit_bytes=VMEM_LIMIT),
    )(q_t, gates_t, kc, vc_t, ks, kw, vs_t, vw_t)


def _rope_tables(seq):
    pos = jnp.arange(seq, dtype=F32)
    inv = ROPE_THETA ** (-jnp.arange(0, HEAD_DIM, 2, dtype=F32) / HEAD_DIM)
    ang = pos[:, None] * inv[None, :]
    cos, sin = jnp.cos(ang), jnp.sin(ang)
    reps = LANES // HEAD_DIM
    return (jnp.tile(jnp.concatenate([cos, cos], axis=1), (1, reps)),
            jnp.tile(jnp.concatenate([-sin, sin], axis=1), (1, reps)))


def _compress_params(w1, w2, pe):
    eye = jnp.eye(HEADS_PER_TILE, dtype=w1.dtype)
    w1 = jnp.einsum('ldf,hk->lhdkf', w1, eye).reshape(CMP_BLOCK, LANES, HEADS_PER_TILE * CMP_HIDDEN)
    pe = jnp.tile(pe, (1, HEADS_PER_TILE)).reshape(CMP_BLOCK, 1, LANES)
    return pe, w1.astype(BF16), w2.astype(BF16)


def kernel(x, ffn1_norm, ffn1_wg, ffn1_wu, ffn1_wd, mix_norm, w_in, conv_w, cmp_k_w1, cmp_k_w2,
           cmp_k_pe, cmp_v_w1, cmp_v_w2, cmp_v_pe, conv_out_norm, attn_out_norm, w_out, ffn2_norm,
           ffn2_wg, ffn2_wu, ffn2_wd, final_norm):
    bsz, seq, d = x.shape
    assert CMP_BLOCK == 2 * CMP_STRIDE and seq % TOKEN_TILE == 0 and seq % KEY_TILE == 0
    assert x.shape[0] * seq % TOKEN_TILE == 0 and ffn1_wg.shape[0] == 1
    xt = x.reshape(bsz * seq, d)
    cos, sin = _rope_tables(seq)

    assert w_in.shape[1:] == (D_MODEL, IN_WIDTH)
    w = jnp.pad(w_in[0].astype(BF16), ((0, 0), (0, IN_WIDTH_PADDED - IN_WIDTH)))
    w_vt = jnp.concatenate([w[:, _VS0:_VS0 + KV_WIDTH], w[:, _VW0:_VW0 + KV_WIDTH]], axis=1).T

    x1, wg2, wu2, wd2, wo = _ffn("ffn1", xt, ffn1_norm, ffn1_wg[0].astype(BF16),
                                 ffn1_wu[0].astype(BF16), ffn1_wd[0].astype(BF16),
                                 later_weights=[ffn2_wg[0], ffn2_wu[0], ffn2_wd[0], w_out[0]])
    yc = _conv_proj(x1, mix_norm, w, conv_w[0], conv_out_norm, seq)
    q_t, k_c, v_c, k_s, k_w, vs_t, vw_t, gates_t = _attn_proj(x1, mix_norm, w, w_vt, cos, sin, seq)

    pe_k, w1_k, w2_k = _compress_params(cmp_k_w1[0], cmp_k_w2[0], cmp_k_pe[0])
    pe_v, w1_v, w2_v = _compress_params(cmp_v_w1[0], cmp_v_w2[0], cmp_v_pe[0])
    kc = _compress(k_c, pe_k, w1_k, jnp.pad(w2_k, ((0, 0), (0, LANES - HEAD_DIM))), seq,
                   transposed=False)
    vc_t = _compress(v_c, pe_v, w1_v, w2_v.T, seq, transposed=True)

    per_batch = lambda a: a.reshape(bsz, seq, a.shape[-1])
    ya_t = _nsa(q_t, gates_t, kc, vc_t, per_batch(k_s), per_batch(k_w), vs_t, vw_t)

    x2 = _out_proj(x1, yc, ya_t, attn_out_norm, wo)
    (out,) = _ffn("ffn2", x2, ffn2_norm, wg2, wu2, wd2, final_gain=final_norm.reshape(1, d))
    return out.reshape(bsz, seq, d)
```

```python
import functools

import jax
import jax.numpy as jnp
from jax import lax
from jax.experimental import pallas as pl
from jax.experimental.pallas import tpu as pltpu

F32 = jnp.float32
BF16 = jnp.bfloat16

D_MODEL = 2048
D_FF = 5632
CONV_WIDTH = 1024
N_HEADS = 16
N_KV = 4
GROUP = N_HEADS // N_KV
HEAD_DIM = 64
ATTN_WIDTH = N_HEADS * HEAD_DIM
KV_WIDTH = N_KV * HEAD_DIM
CMP_BLOCK = 32
CMP_STRIDE = 16
CMP_HIDDEN = 256
SLC_BLOCK = 64
SLC_TOPK = 16
N_LOCAL = 2
WINDOW = 512
Q_BLOCK = 128
N_GATES = 3
ROPE_THETA = 10000.0
EPS = 1e-6
NEG = -1e30
FORCE = 1e9

LANES = 128
GROUP_WIDTH = GROUP * HEAD_DIM
GATE_ROWS = LANES
TOKEN_TILE = 512
FFN_TOKEN_TILE = 1024
FF_TILE = 512
KEY_TILE = 256
WIN_KEYS = WINDOW + Q_BLOCK
ONES_ROWS = 16
HEADS_PER_STEP = 4
VMEM_LIMIT = 56 * 1024 * 1024
FFN_VMEM_LIMIT = 62 * 1024 * 1024
Q_SCALE = HEAD_DIM ** -0.5 * 1.4426950408889634


def _rms(x):
    return x * lax.rsqrt(jnp.mean(x * x, axis=-1, keepdims=True) + EPS)


def _dot(a, b):
    return jnp.dot(a, b, preferred_element_type=F32)


def _dot_nt(a, b):
    return lax.dot_general(a, b, (((1,), (1,)), ((), ())), preferred_element_type=F32)


def _div_pow2(x, n):
    assert n & (n - 1) == 0
    return jnp.right_shift(x, n.bit_length() - 1)


def _iota(shape, dim):
    return lax.broadcasted_iota(jnp.int32, shape, dim)


def _resident(shape):
    return pl.BlockSpec(shape, lambda *_: (0,) * len(shape), pipeline_mode=pl.Buffered(1))


def _round_robin(gens, skew=1):
    results = [None] * len(gens)
    live = {}
    waiting = dict(enumerate(gens))
    rnd = 0
    while waiting or live:
        for k in [k for k in waiting if rnd >= k * skew]:
            live[k] = waiting.pop(k)
        for k in sorted(live):
            try:
                next(live[k])
            except StopIteration as done:
                results[k] = done.value
                del live[k]
        rnd += 1
    return results


def _ffn_sweep(make_h, finish, wg_ref, wu_ref, wd_ref, h_ref, acc_ref):
    j = pl.program_id(1)
    last = pl.num_programs(1) - 1
    tm = h_ref.shape[0]
    halves = [slice(0, tm // 2), slice(tm // 2, tm)]

    def hidden(h):
        a = _dot(h, wg_ref[...])
        u = _dot(h, wu_ref[...])
        return (jax.nn.silu(a) * u).astype(BF16)

    @pl.when(j == 0)
    def _():
        def first(rows):
            h = yield from make_h(rows)
            h_ref[rows, :] = h
            yield
            act = hidden(h)
            yield
            acc_ref[rows, :] = _dot(act, wd_ref[...])

        _round_robin([first(rows) for rows in halves])

    @pl.when((j > 0) & (j < last))
    def _():
        acc_ref[...] += _dot(hidden(h_ref[...]), wd_ref[...])

    @pl.when(j == last)
    def _():
        def final(rows):
            act = hidden(h_ref[rows, :])
            yield
            finish(rows, acc_ref[rows, :] + _dot(act, wd_ref[...]))

        _round_robin([final(rows) for rows in halves])


def _ffn_kernel(x_ref, g_ref, wg_ref, wu_ref, wd_ref, *refs, n_cast, final_norm):
    refs = list(refs)
    fin_ref = refs.pop(0) if final_norm else None
    cast_in = refs[:n_cast]
    o_ref = refs[n_cast]
    cast_out = refs[n_cast + 1:2 * n_cast + 1]
    (h_ref,) = refs[2 * n_cast + 1:]
    acc_ref = o_ref
    for src, dst in zip(cast_in, cast_out):
        dst[...] = src[...].astype(BF16)

    def make_h(rows):
        return (_rms(x_ref[rows, :]) * g_ref[...]).astype(BF16)
        yield

    def finish(rows, acc):
        y = x_ref[rows, :] + 0.5 * acc
        o_ref[rows, :] = _rms(y) * fin_ref[...] if final_norm else y

    _ffn_sweep(make_h, finish, wg_ref, wu_ref, wd_ref, h_ref, acc_ref)


def _cast_block(shape, ni, nj):
    rows, cols = shape
    sub, lane = 16, LANES
    if rows % ni == 0 and cols % nj == 0 and (rows // ni) % sub == 0 and (cols // nj) % lane == 0:
        return pl.BlockSpec((rows // ni, cols // nj), lambda i, j: (i, j))
    if rows % (ni * nj) == 0 and (rows // (ni * nj)) % sub == 0:
        return pl.BlockSpec((rows // (ni * nj), cols), lambda i, j: (i * nj + j, 0))
    assert rows % ni == 0 and (rows // ni) % sub == 0
    return pl.BlockSpec((rows // ni, cols), lambda i, j: (i, 0))


def _ffn(name, x, g, wg, wu, wd, later_weights=(), final_gain=None):
    t, d = x.shape
    f = wg.shape[1]
    tm = FFN_TOKEN_TILE
    grid = (t // tm, f // FF_TILE)
    cast_specs = [_cast_block(w.shape, *grid) for w in later_weights]
    gain = lambda: pl.BlockSpec((1, d), lambda i, j: (0, 0))
    final = [] if final_gain is None else [final_gain]
    return pl.pallas_call(
        functools.partial(_ffn_kernel, n_cast=len(later_weights), final_norm=bool(final)),
        name=name,
        grid=grid,
        in_specs=[
            pl.BlockSpec((tm, d), lambda i, j: (i, 0)),
            gain(),
            pl.BlockSpec((d, FF_TILE), lambda i, j: (0, j)),
            pl.BlockSpec((d, FF_TILE), lambda i, j: (0, j)),
            pl.BlockSpec((FF_TILE, d), lambda i, j: (j, 0)),
        ] + [gain() for _ in final] + cast_specs,
        out_specs=[pl.BlockSpec((tm, d), lambda i, j: (i, 0))] + cast_specs,
        out_shape=[jax.ShapeDtypeStruct((t, d), F32)] + [
            jax.ShapeDtypeStruct(w.shape, BF16) for w in later_weights],
        scratch_shapes=[pltpu.VMEM((tm, d), BF16)],
        compiler_params=pltpu.CompilerParams(
            dimension_semantics=("arbitrary", "arbitrary"), vmem_limit_bytes=FFN_VMEM_LIMIT),
    )(x, g, wg, wu, wd, *final, *later_weights)


def _out_proj_kernel(x_ref, yc_ref, ya_ref, ga_ref, wo_ref, o_ref):
    tm = x_ref.shape[0]

    def one_half(rows):
        ya = (_rms(ya_ref[0, :, rows].T) * ga_ref[...]).astype(BF16)
        yield
        mix = _dot(yc_ref[rows, :], wo_ref[0:CONV_WIDTH, :]) + _dot(ya, wo_ref[CONV_WIDTH:, :])
        yield
        o_ref[rows, :] = x_ref[rows, :] + mix

    _round_robin([one_half(slice(0, tm // 2)), one_half(slice(tm // 2, tm))])


def _out_proj(x1, yc, ya_t, ga, wo):
    t, d = x1.shape
    tm = TOKEN_TILE
    tps = ya_t.shape[2] // tm
    return pl.pallas_call(
        _out_proj_kernel,
        name="out_proj",
        grid=(t // tm,),
        in_specs=[
            pl.BlockSpec((tm, d), lambda i: (i, 0)),
            pl.BlockSpec((tm, CONV_WIDTH), lambda i: (i, 0)),
            pl.BlockSpec((1, ATTN_WIDTH, tm), lambda i: (i // tps, 0, i % tps)),
            pl.BlockSpec((1, ATTN_WIDTH), lambda i: (0, 0)),
            _resident(wo.shape),
        ],
        out_specs=pl.BlockSpec((tm, d), lambda i: (i, 0)),
        out_shape=jax.ShapeDtypeStruct((t, d), F32),
        compiler_params=pltpu.CompilerParams(
            dimension_semantics=("parallel",), vmem_limit_bytes=VMEM_LIMIT),
    )(x1, yc, ya_t, ga, wo)


def _conv_proj_kernel(x_ref, g_ref, w_ref, cw_ref, gc_ref, o_ref, carry_ref, *, tiles_per_seq):
    i = pl.program_id(0)

    @pl.when(i % tiles_per_seq == 0)
    def _():
        carry_ref[...] = jnp.zeros_like(carry_ref)

    tm = x_ref.shape[0]
    half = tm // 2
    tails = {-1: carry_ref[...]}

    def one_half(k):
        rows = slice(k * half, (k + 1) * half)
        h = (_rms(x_ref[rows, :]) * g_ref[...]).astype(BF16)
        yield
        c_h = _dot(h, w_ref[:, 0:CONV_WIDTH])
        c_c = _dot(h, w_ref[:, 2 * CONV_WIDTH:3 * CONV_WIDTH])
        u = c_c * c_h
        tails[k] = u[half - 8:half, :]
        c_b = _dot(h, w_ref[:, CONV_WIDTH:2 * CONV_WIDTH])
        yield
        row = lax.broadcasted_iota(jnp.int32, u.shape, 0)
        prev1 = tails[k - 1][7:8, :]
        prev2 = tails[k - 1][6:7, :]
        u1 = jnp.where(row == 0, prev1, pltpu.roll(u, 1, 0))
        u2 = jnp.where(row == 0, prev2, jnp.where(row == 1, prev1, pltpu.roll(u, 2, 0)))
        y = c_b * (cw_ref[0:1, :] * u2 + cw_ref[1:2, :] * u1 + cw_ref[2:3, :] * u)
        o_ref[rows, :] = (_rms(y) * gc_ref[...]).astype(BF16)

    _round_robin([one_half(0), one_half(1)])
    carry_ref[...] = tails[1]


def _conv_proj(x1, g, w_in, conv_w, gc, seq):
    t, d = x1.shape
    tm = TOKEN_TILE
    return pl.pallas_call(
        functools.partial(_conv_proj_kernel, tiles_per_seq=seq // tm),
        name="conv_proj",
        grid=(t // tm,),
        in_specs=[
            pl.BlockSpec((tm, d), lambda i: (i, 0)),
            pl.BlockSpec((1, d), lambda i: (0, 0)),
            _w_in_window(_CONV0, 3 * CONV_WIDTH),
            pl.BlockSpec(conv_w.shape, lambda i: (0, 0)),
            pl.BlockSpec((1, CONV_WIDTH), lambda i: (0, 0)),
        ],
        out_specs=pl.BlockSpec((tm, CONV_WIDTH), lambda i: (i, 0)),
        out_shape=jax.ShapeDtypeStruct((t, CONV_WIDTH), BF16),
        scratch_shapes=[pltpu.VMEM((8, CONV_WIDTH), F32)],
        compiler_params=pltpu.CompilerParams(
            dimension_semantics=("arbitrary",), vmem_limit_bytes=VMEM_LIMIT),
    )(x1, g, w_in, conv_w, gc)


def _rope(x, cos, sin_signed, first_half):
    outs = []
    for k in range(x.shape[1] // LANES):
        xc = x[:, k * LANES:(k + 1) * LANES]
        partner = jnp.where(first_half, pltpu.roll(xc, LANES - HEAD_DIM // 2, 1),
                            pltpu.roll(xc, HEAD_DIM // 2, 1))
        outs.append(xc * cos + partner * sin_signed)
    return outs


_CONV0 = 0
_Q0 = 3 * CONV_WIDTH
_KC0 = _Q0 + ATTN_WIDTH
_VC0 = _KC0 + KV_WIDTH
_KS0 = _VC0 + KV_WIDTH
_VS0 = _KS0 + KV_WIDTH
_KW0 = _VS0 + KV_WIDTH
_VW0 = _KW0 + KV_WIDTH
_G0 = _VW0 + KV_WIDTH
IN_WIDTH = _G0 + N_HEADS * N_GATES
IN_WIDTH_PADDED = -(-IN_WIDTH // LANES) * LANES


def _w_in_window(c0, width):
    assert c0 % width == 0
    return pl.BlockSpec((D_MODEL, width), lambda *_: (0, c0 // width), pipeline_mode=pl.Buffered(1))


def _attn_proj_kernel(x_ref, g_ref, wq_ref, wkc_ref, wvc_ref, wks_ref, wkw_ref, wgate_ref, wvt_ref,
                      cos_ref, sin_ref,
                      q_ref, kc_ref, vc_ref, ks_ref, kw_ref, vs_ref, vw_ref, gate_ref):
    tm = x_ref.shape[0]
    half = tm // 2
    lane = lax.broadcasted_iota(jnp.int32, (half, LANES), 1)
    first_half = (lane & (HEAD_DIM - 1)) < HEAD_DIM // 2

    def one_half(k):
        rows = slice(k * half, (k + 1) * half)
        h = (_rms(x_ref[rows, :]) * g_ref[...]).astype(BF16)
        cos = cos_ref[rows, :]
        sin = sin_ref[rows, :]

        def proj(w_ref):
            return _dot(h, w_ref[...])

        def rope_to(ref, w_ref, dtype):
            for n, piece in enumerate(_rope(proj(w_ref), cos, sin, first_half)):
                ref[rows, n * LANES:(n + 1) * LANES] = piece.astype(dtype)

        yield
        for n, piece in enumerate(_rope(proj(wq_ref), cos, sin, first_half)):
            q_ref[0, n * LANES:(n + 1) * LANES, rows] = (piece * Q_SCALE).T.astype(BF16)
        yield
        rope_to(kc_ref, wkc_ref, F32)
        vc_ref[rows, :] = proj(wvc_ref)
        rope_to(ks_ref, wks_ref, BF16)
        rope_to(kw_ref, wkw_ref, BF16)
        yield
        gate_ref[0, :, rows] = jax.nn.sigmoid(proj(wgate_ref)).T
        vt = _dot_nt(wvt_ref[...], h)
        chunks = half // Q_BLOCK
        for n in range(chunks):
            cs = slice(n * Q_BLOCK, (n + 1) * Q_BLOCK)
            vs_ref[0, k * chunks + n] = vt[0:KV_WIDTH, cs].astype(BF16)
            vw_ref[0, k * chunks + n] = vt[KV_WIDTH:2 * KV_WIDTH, cs].astype(BF16)

    _round_robin([one_half(0), one_half(1)])


def _attn_proj(x1, g, w_in, w_vt, cos, sin, seq):
    t, d = x1.shape
    tm = TOKEN_TILE
    tps = seq // tm
    bsz = t // seq
    cpt = tm // Q_BLOCK
    row = lambda w: pl.BlockSpec((tm, w), lambda i: (i, 0))
    col = lambda w: pl.BlockSpec((1, w, tm), lambda i: (i // tps, 0, i % tps))
    chunked = pl.BlockSpec((1, cpt, KV_WIDTH, Q_BLOCK), lambda i: (i // tps, i % tps, 0, 0))
    kv_dtypes = [F32, F32, BF16, BF16]
    out_specs = [col(ATTN_WIDTH)] + [row(KV_WIDTH)] * 4 + [chunked, chunked, col(GATE_ROWS)]
    out_shape = ([jax.ShapeDtypeStruct((bsz, ATTN_WIDTH, seq), BF16)]
                 + [jax.ShapeDtypeStruct((t, KV_WIDTH), dt) for dt in kv_dtypes]
                 + [jax.ShapeDtypeStruct((bsz, seq // Q_BLOCK, KV_WIDTH, Q_BLOCK), BF16)] * 2
                 + [jax.ShapeDtypeStruct((bsz, GATE_ROWS, seq), F32)])
    windows = [_w_in_window(_Q0, ATTN_WIDTH)] + [
        _w_in_window(c0, KV_WIDTH) for c0 in (_KC0, _VC0, _KS0, _KW0)] + [
        _w_in_window(_G0, GATE_ROWS)]
    return pl.pallas_call(
        _attn_proj_kernel,
        name="attn_proj",
        grid=(t // tm,),
        in_specs=[row(d), pl.BlockSpec((1, d), lambda i: (0, 0))] + windows + [
            _resident(w_vt.shape),
            pl.BlockSpec((tm, LANES), lambda i: (i % tps, 0)),
            pl.BlockSpec((tm, LANES), lambda i: (i % tps, 0)),
        ],
        out_specs=out_specs,
        out_shape=out_shape,
        compiler_params=pltpu.CompilerParams(
            dimension_semantics=("parallel",), vmem_limit_bytes=VMEM_LIMIT),
    )(x1, g, *([w_in] * len(windows)), w_vt, cos, sin)


HEADS_PER_TILE = LANES // HEAD_DIM


def _compress_kernel(*refs, transposed):
    tok_refs = refs[:N_KV // HEADS_PER_TILE]
    pe_ref, w1_ref, w2_ref, o_ref = refs[N_KV // HEADS_PER_TILE:]
    n = tok_refs[0].shape[0] // CMP_STRIDE
    for t, tok_ref in enumerate(tok_refs):
        first = jnp.zeros((n, HEADS_PER_TILE * CMP_HIDDEN), F32)
        second = jnp.zeros((n, HEADS_PER_TILE * CMP_HIDDEN), F32)
        for l in range(CMP_STRIDE):
            rows = tok_ref[pl.ds(l, n, stride=CMP_STRIDE), :]
            first = first + _dot((rows + pe_ref[l]).astype(BF16), w1_ref[l])
            second = second + _dot((rows + pe_ref[l + CMP_STRIDE]).astype(BF16),
                                   w1_ref[l + CMP_STRIDE])
        hid = jax.nn.gelu(first + pltpu.roll(second, n - 1, 0)).astype(BF16)
        for k in range(HEADS_PER_TILE):
            hid_h = hid[:, k * CMP_HIDDEN:(k + 1) * CMP_HIDDEN]
            h = t * HEADS_PER_TILE + k
            if transposed:
                o_ref[0, h] = _dot_nt(w2_ref[...], hid_h).astype(BF16)
            else:
                o_ref[0, h] = _dot(hid_h, w2_ref[...]).astype(BF16)


def _compress(tok, pe, w1, w2, seq, transposed):
    bsz = tok.shape[0] // seq
    n = seq // CMP_STRIDE
    out_block = (1, N_KV, HEAD_DIM, n) if transposed else (1, N_KV, n, LANES)
    lane_tile = lambda t: pl.BlockSpec((seq, LANES), lambda b: (b, t))
    return pl.pallas_call(
        functools.partial(_compress_kernel, transposed=transposed),
        name="compress_v" if transposed else "compress_k",
        grid=(bsz,),
        in_specs=[lane_tile(t) for t in range(N_KV // HEADS_PER_TILE)] + [
            _resident(pe.shape),
            _resident(w1.shape),
            _resident(w2.shape),
        ],
        out_specs=pl.BlockSpec(out_block, lambda b: (b, 0, 0, 0)),
        out_shape=jax.ShapeDtypeStruct((bsz,) + out_block[1:], BF16),
        compiler_params=pltpu.CompilerParams(
            dimension_semantics=("parallel",), vmem_limit_bytes=VMEM_LIMIT),
    )(*([tok] * (N_KV // HEADS_PER_TILE)), pe, w1, w2)


def _tile4(x):
    return jnp.concatenate([x] * GROUP, axis=1)


def _col_max(x):
    return jnp.max(x, axis=0, keepdims=True)


def _col_sum(x):
    return jnp.sum(x, axis=0, keepdims=True)


def _with_ones(v):
    ones = jnp.where(_iota((ONES_ROWS, v.shape[1]), 0) == 0, 1.0, 0.0).astype(v.dtype)
    return jnp.concatenate([v, ones], axis=0)


def _nsa_kernel(q_ref, gate_ref, kc_ref, vc_ref, ks_ref, kw_ref, vs_ref, vw_ref, o_ref,
                kse_ref, kwe_ref, imp_ref, *, seq):
    hp = pl.program_id(1)
    c = pl.program_id(2)
    t0 = c * Q_BLOCK
    n_cmp = kc_ref.shape[2]
    n_slc = seq // SLC_BLOCK
    cols = GROUP * Q_BLOCK
    n_chunks = seq // Q_BLOCK
    chunks_per_tile = KEY_TILE // Q_BLOCK
    heads = range(HEADS_PER_STEP)
    assert n_slc + HEAD_DIM == LANES and n_slc % 8 == 0

    @pl.when(c == 0)
    def _():
        r = _iota((KV_WIDTH, 2 * LANES), 0)
        ln = _iota((KV_WIDTH, 2 * LANES), 1)
        picks = [((r == (hp * HEADS_PER_STEP + hh) * HEAD_DIM + ln) & (ln < HEAD_DIM)).astype(BF16)
                 for hh in heads]

        def build_k(i, _):
            sl = pl.ds(pl.multiple_of(i * KEY_TILE, KEY_TILE), KEY_TILE)
            pos = i * KEY_TILE + _iota((KEY_TILE, 2 * LANES), 0)
            lane = _iota((KEY_TILE, 2 * LANES), 1)
            in_chunk = lane - LANES == (pos & (Q_BLOCK - 1))
            in_block = lane - HEAD_DIM == _div_pow2(pos, SLC_BLOCK)
            hot_s = jnp.where(in_chunk | in_block, 1.0, 0.0)
            hot_w = jnp.where(in_chunk, 1.0, 0.0)
            ks = ks_ref[0, sl, :]
            kw = kw_ref[0, sl, :]
            for hh in heads:
                kse_ref[hh, sl, :] = (_dot(ks, picks[hh]) + hot_s).astype(BF16)
                kwe_ref[hh, sl, :] = (_dot(kw, picks[hh]) + hot_w).astype(BF16)
            return 0

        lax.fori_loop(0, seq // KEY_TILE, build_k, 0)
        for hh in heads:
            kwe_ref[hh, pl.ds(seq, Q_BLOCK), :] = jnp.where(
                _iota((Q_BLOCK, 2 * LANES), 1) == HEAD_DIM, 1.0, 0.0).astype(BF16)

    def v_chunk(ref, hh, idx):
        return _with_ones(ref[0, idx, hh * HEAD_DIM:(hh + 1) * HEAD_DIM, :])

    def v_tile(hh, kt):
        return _with_ones(jnp.concatenate(
            [vs_ref[0, kt * chunks_per_tile + k, hh * HEAD_DIM:(hh + 1) * HEAD_DIM, :]
             for k in range(chunks_per_tile)], axis=1))

    def soft(scores, m):
        return [jnp.exp2(s - m).astype(BF16) for s in scores]

    kpos = _iota((Q_BLOCK, Q_BLOCK), 0)
    tpos = _iota((Q_BLOCK, Q_BLOCK), 1)
    causal = _tile4(jnp.where(kpos <= tpos, 0.0, NEG).astype(BF16))
    recent = _tile4(jnp.where(kpos > tpos, 0.0, NEG).astype(BF16))
    pad = jnp.zeros((HEAD_DIM, cols), BF16)
    dummy_row = jnp.where(_iota((HEAD_DIM, cols), 0) == 0, NEG, 0.0).astype(BF16)
    jj = _iota((n_slc, n_cmp), 0)
    ii = _iota((n_slc, n_cmp), 1)
    ov = (jnp.minimum(ii * CMP_STRIDE + CMP_BLOCK, jj * SLC_BLOCK + SLC_BLOCK)
          - jnp.maximum(ii * CMP_STRIDE, jj * SLC_BLOCK))
    ov = (jnp.maximum(ov, 0).astype(F32) * (1.0 / CMP_BLOCK)).astype(BF16)
    c_ok = (_iota((n_cmp, Q_BLOCK), 0) * CMP_STRIDE + (CMP_BLOCK - 1)
            <= t0 + _iota((n_cmp, Q_BLOCK), 1))
    c_bias = _tile4(jnp.where(c_ok, 0.0, NEG))
    any_ok = t0 + (_iota((1, cols), 1) & (Q_BLOCK - 1)) >= CMP_BLOCK - 1
    jb = _iota((n_slc, Q_BLOCK), 0)
    tb = _div_pow2(t0 + _iota((n_slc, Q_BLOCK), 1), SLC_BLOCK)
    forced = (jb == 0) | ((jb <= tb) & (jb > tb - N_LOCAL))
    sub = _iota((8, Q_BLOCK), 0)
    n_win = WIN_KEYS // Q_BLOCK
    first = c - (n_win - 1)
    k_chunk = [jnp.where(first + i >= 0, first + i, n_chunks) for i in range(n_win)]
    v_idx = [jnp.maximum(first + i, 0) for i in range(n_win)]
    k_rows = [pl.ds(pl.multiple_of(ch * Q_BLOCK, Q_BLOCK), Q_BLOCK) for ch in k_chunk]

    def before_sweep(hh):
        qt = q_ref[0, hh * GROUP_WIDTH:(hh + 1) * GROUP_WIDTH, :]
        wq = jnp.concatenate([qt[g * HEAD_DIM:(g + 1) * HEAD_DIM, :] for g in range(GROUP)], axis=1)

        sc = _dot(kc_ref[0, hh], jnp.concatenate([wq, pad], axis=0)) + c_bias
        yield
        pc = jnp.exp2(sc - _col_max(sc))
        pc = pc * jnp.where(any_ok, 1.0 / _col_sum(pc), 0.0)
        yield
        o_cmp = _dot(vc_ref[0, hh], pc.astype(BF16))

        pcs = pc[:, 0:Q_BLOCK]
        for g in range(1, GROUP):
            pcs = pcs + pc[:, g * Q_BLOCK:(g + 1) * Q_BLOCK]
        pcs_hi = pcs.astype(BF16)
        pcs_lo = (pcs - pcs_hi.astype(F32)).astype(BF16)
        imp = _dot(ov, pcs_hi) + _dot(ov, pcs_lo)
        yield
        imp = jnp.where(forced, FORCE, imp)
        imp = jnp.where(jb <= tb, imp, NEG)
        imp_ref[hh] = imp

        w_causal = jnp.concatenate([wq, pad, causal], axis=0)
        w_mid = jnp.concatenate([wq, dummy_row], axis=0)
        w_recent = jnp.concatenate([wq, dummy_row, recent], axis=0)
        sw = [_dot(kwe_ref[hh, k_rows[0], :], w_recent)]
        sw += [_dot(kwe_ref[hh, k_rows[i], 0:LANES], w_mid) for i in range(1, n_win - 1)]
        sw += [_dot(kwe_ref[hh, k_rows[n_win - 1], :], w_causal)]
        s_d = _dot(kse_ref[hh, pl.ds(pl.multiple_of(t0, Q_BLOCK), Q_BLOCK), :], w_causal)
        yield

        n_groups = n_slc // 8
        mine = [imp[8 * r:8 * r + 8, :] for r in range(n_groups)]
        ahead_count = [jnp.zeros((8, Q_BLOCK), F32)] * n_groups
        for j in range(n_slc):
            other = imp_ref[hh, j:j + 1, :]
            for r in range(n_groups):
                if 8 * r > j:
                    ahead = other >= mine[r]
                elif 8 * r + 7 <= j:
                    ahead = other > mine[r]
                else:
                    ahead = (other > mine[r]) | ((other == mine[r]) & (sub > j - 8 * r))
                ahead_count[r] = ahead_count[r] + jnp.where(ahead, 1.0, 0.0)
            if j % 8 == 7:
                yield
        sel = (jnp.concatenate(ahead_count, axis=0) < SLC_TOPK) & (imp > 0.5 * NEG)
        sel_bias = jnp.where(sel & (jb * SLC_BLOCK < t0), 0.0, NEG).astype(BF16)
        w_sel = jnp.concatenate([wq, _tile4(sel_bias)], axis=0)

        m_w = _col_max(sw[0])
        for s in sw[1:]:
            m_w = jnp.maximum(m_w, _col_max(s))
        yield
        p_w = soft(sw, m_w)
        yield
        o_win = None
        for idx, p in zip(v_idx, p_w):
            part = _dot(v_chunk(vw_ref, hh, idx), p)
            o_win = part if o_win is None else o_win + part
        m_d = _col_max(s_d)
        acc_d = _dot(v_chunk(vs_ref, hh, c), soft([s_d], m_d)[0])
        return w_sel, o_cmp, o_win, (m_d, acc_d)

    pre = _round_robin([before_sweep(hh) for hh in heads])

    def scores(hh, kt):
        k0 = pl.multiple_of(kt * KEY_TILE, KEY_TILE)
        return _dot(kse_ref[hh, pl.ds(k0, KEY_TILE), 0:LANES], pre[hh][0])

    def sweep(items, states, lookahead=2):
        states = list(states)
        s, m_new, p = {}, {}, {}

        def issue(n):
            s[n] = scores(*items[n])

        def value_matmul(n):
            hh, kt = items[n]
            m, acc = states[hh]
            states[hh] = m_new[n], jnp.exp2(m - m_new[n]) * acc + _dot(v_tile(hh, kt), p[n])

        for n in range(min(lookahead, len(items))):
            issue(n)
        for n, (hh, _) in enumerate(items):
            m_new[n] = jnp.maximum(states[hh][0], _col_max(s[n]))
            if n + lookahead < len(items):
                issue(n + lookahead)
            p[n] = jnp.exp2(s[n] - m_new[n]).astype(BF16)
            if n >= 1:
                value_matmul(n - 1)
        value_matmul(len(items) - 1)
        return tuple(states)

    def tiles(first, count):
        return lambda states: sweep([(hh, first + k) for k in range(count) for hh in heads], states)

    n_tiles = _div_pow2(t0 + KEY_TILE - 1, KEY_TILE)
    keep = lambda states: states
    swept = lax.fori_loop(0, _div_pow2(n_tiles, 4), lambda i, st: tiles(4 * i, 4)(st),
                          tuple(p[3] for p in pre))
    swept = lax.cond((n_tiles & 2) == 2, tiles(n_tiles & ~3, 2), keep, swept)
    swept = lax.cond((n_tiles & 1) == 1, tiles(n_tiles - 1, 1), keep, swept)

    for hh in heads:
        _, o_cmp, o_win, _ = pre[hh]
        o_slc = swept[hh][1]
        gates = gate_ref[0, hh * GROUP * N_GATES:(hh + 1) * GROUP * N_GATES, :]
        inv_ls = 1.0 / o_slc[HEAD_DIM:HEAD_DIM + 1, :]
        inv_lw = 1.0 / o_win[HEAD_DIM:HEAD_DIM + 1, :]
        pieces = []
        for g in range(GROUP):
            cs = slice(g * Q_BLOCK, (g + 1) * Q_BLOCK)
            row = g * N_GATES
            pieces.append(gates[row:row + 1, :] * o_cmp[:, cs]
                          + gates[row + 1:row + 2, :] * (o_slc[0:HEAD_DIM, cs] * inv_ls[:, cs])
                          + gates[row + 2:row + 3, :] * (o_win[0:HEAD_DIM, cs] * inv_lw[:, cs]))
        o_ref[0, hh * GROUP_WIDTH:(hh + 1) * GROUP_WIDTH, :] = jnp.concatenate(pieces, axis=0)


def _nsa(q_t, gates_t, kc, vc_t, ks, kw, vs_t, vw_t):
    bsz, _, seq = q_t.shape
    n_cmp = kc.shape[2]
    hps = HEADS_PER_STEP
    assert hps == N_KV
    k_spec = pl.BlockSpec((1, seq, KV_WIDTH), lambda b, h, c: (b, 0, 0))
    v_spec = pl.BlockSpec((1, seq // Q_BLOCK, hps * HEAD_DIM, Q_BLOCK), lambda b, h, c: (b, 0, h, 0))
    return pl.pallas_call(
        functools.partial(_nsa_kernel, seq=seq),
        name="nsa",
        grid=(bsz, N_KV // hps, seq // Q_BLOCK),
        in_specs=[
            pl.BlockSpec((1, hps * GROUP_WIDTH, Q_BLOCK), lambda b, h, c: (b, h, c)),
            pl.BlockSpec((1, GATE_ROWS, Q_BLOCK), lambda b, h, c: (b, 0, c)),
            pl.BlockSpec((1, hps, n_cmp, LANES), lambda b, h, c: (b, h, 0, 0)),
            pl.BlockSpec((1, hps, HEAD_DIM, n_cmp), lambda b, h, c: (b, h, 0, 0)),
            k_spec, k_spec, v_spec, v_spec,
        ],
        out_specs=pl.BlockSpec((1, hps * GROUP_WIDTH, Q_BLOCK), lambda b, h, c: (b, h, c)),
        out_shape=jax.ShapeDtypeStruct((bsz, ATTN_WIDTH, seq), F32),
        scratch_shapes=[
            pltpu.VMEM((hps, seq, 2 * LANES), BF16),
            pltpu.VMEM((hps, seq + Q_BLOCK, 2 * LANES), BF16),
            pltpu.VMEM((hps, seq // SLC_BLOCK, Q_BLOCK), F32),
        ],
        compiler_params=pltpu.CompilerParams(
            dimension_semantics=("parallel", "parallel", "arbitrary"),
            vmem_limit_bytes=VMEM_LIMIT),
    )(q_t, gates_t, kc, vc_t, ks, kw, vs_t, vw_t)


def _rope_tables(seq):
    pos = jnp.arange(seq, dtype=F32)
    inv = ROPE_THETA ** (-jnp.arange(0, HEAD_DIM, 2, dtype=F32) / HEAD_DIM)
    ang = pos[:, None] * inv[None, :]
    cos, sin = jnp.cos(ang), jnp.sin(ang)
    reps = LANES // HEAD_DIM
    return (jnp.tile(jnp.concatenate([cos, cos], axis=1), (1, reps)),
            jnp.tile(jnp.concatenate([-sin, sin], axis=1), (1, reps)))


def _compress_params(w1, w2, pe):
    eye = jnp.eye(HEADS_PER_TILE, dtype=w1.dtype)
    w1 = jnp.einsum('ldf,hk->lhdkf', w1, eye).reshape(CMP_BLOCK, LANES, HEADS_PER_TILE * CMP_HIDDEN)
    pe = jnp.tile(pe, (1, HEADS_PER_TILE)).reshape(CMP_BLOCK, 1, LANES)
    return pe, w1.astype(BF16), w2.astype(BF16)


def kernel(x, ffn1_norm, ffn1_wg, ffn1_wu, ffn1_wd, mix_norm, w_in, conv_w, cmp_k_w1, cmp_k_w2,
           cmp_k_pe, cmp_v_w1, cmp_v_w2, cmp_v_pe, conv_out_norm, attn_out_norm, w_out, ffn2_norm,
           ffn2_wg, ffn2_wu, ffn2_wd, final_norm):
    bsz, seq, d = x.shape
    assert CMP_BLOCK == 2 * CMP_STRIDE and seq % TOKEN_TILE == 0 and seq % KEY_TILE == 0
    assert x.shape[0] * seq % TOKEN_TILE == 0 and ffn1_wg.shape[0] == 1
    xt = x.reshape(bsz * seq, d)
    cos, sin = _rope_tables(seq)

    assert w_in.shape[1:] == (D_MODEL, IN_WIDTH)
    w = jnp.pad(w_in[0].astype(BF16), ((0, 0), (0, IN_WIDTH_PADDED - IN_WIDTH)))
    w_vt = jnp.concatenate([w[:, _VS0:_VS0 + KV_WIDTH], w[:, _VW0:_VW0 + KV_WIDTH]], axis=1).T

    x1, wg2, wu2, wd2, wo = _ffn("ffn1", xt, ffn1_norm, ffn1_wg[0].astype(BF16),
                                 ffn1_wu[0].astype(BF16), ffn1_wd[0].astype(BF16),
                                 later_weights=[ffn2_wg[0], ffn2_wu[0], ffn2_wd[0], w_out[0]])
    yc = _conv_proj(x1, mix_norm, w, conv_w[0], conv_out_norm, seq)
    q_t, k_c, v_c, k_s, k_w, vs_t, vw_t, gates_t = _attn_proj(x1, mix_norm, w, w_vt, cos, sin, seq)

    pe_k, w1_k, w2_k = _compress_params(cmp_k_w1[0], cmp_k_w2[0], cmp_k_pe[0])
    pe_v, w1_v, w2_v = _compress_params(cmp_v_w1[0], cmp_v_w2[0], cmp_v_pe[0])
    kc = _compress(k_c, pe_k, w1_k, jnp.pad(w2_k, ((0, 0), (0, LANES - HEAD_DIM))), seq,
                   transposed=False)
    vc_t = _compress(v_c, pe_v, w1_v, w2_v.T, seq, transposed=True)

    per_batch = lambda a: a.reshape(bsz, seq, a.shape[-1])
    ya_t = _nsa(q_t, gates_t, kc, vc_t, per_batch(k_s), per_batch(k_w), vs_t, vw_t)

    x2 = _out_proj(x1, yc, ya_t, attn_out_norm, wo)
    (out,) = _ffn("ffn2", x2, ffn2_norm, wg2, wu2, wd2, final_gain=final_norm.reshape(1, d))
    return out.reshape(bsz, seq, d)
```
